```python
import jax
import jax.numpy as jnp
from jax import lax
import numpy as np

D_MODEL = 2048
BATCH = 8
SEQ = 2048
DEPTH = 2

CTX_LEN = 256
GRID_W = 64
N_MIXERS = 2
EPS = 1e-6
N_MOD = 6
RG_WIDTH = D_MODEL
RG_HEADS = 16
RG_HEAD_DIM = RG_WIDTH // RG_HEADS
RG_CONV = 4
RG_CONV_LEFT = 2
RG_C = 8.0
RG_A_MIN = 0.9
RG_A_MAX = 0.999
HG_HEADS = 16
HG_DK = D_MODEL // HG_HEADS
HG_DV = D_MODEL // HG_HEADS
HG_CHUNK = 32
D_FF = ((8 * D_MODEL + 3 * 256 - 1) // (3 * 256)) * 256
N_RG = (DEPTH + N_MIXERS - 1) // N_MIXERS
N_HG = DEPTH // N_MIXERS

kernel_name = 'hybrid_rglru_hgrn2_flow_block'


def _rms(x, g):
    xf = x.astype(jnp.float32)
    y = xf * lax.rsqrt(jnp.mean(xf * xf, axis=-1, keepdims=True) + EPS)
    return y.astype(x.dtype) * g


def _ada(cvec, w, b):
    m = jax.nn.silu(cvec) @ w + b
    return [t[:, None, :] for t in jnp.split(m, N_MOD, axis=-1)]


def _modulate(x, g, shift, scale):
    return _rms(x, g) * (1.0 + scale) + shift


def _swiglu(h, w_in, w_out):
    gate, up = jnp.split(h @ w_in, 2, axis=-1)
    return (jax.nn.silu(gate) * up) @ w_out


def _to_col_major(h, rows):
    bn, l, d = h.shape
    return h.reshape(bn, rows, GRID_W, d).transpose(0, 2, 1, 3).reshape(bn, l, d)


def _from_col_major(h, rows):
    bn, l, d = h.shape
    return h.reshape(bn, GRID_W, rows, d).transpose(0, 2, 1, 3).reshape(bn, l, d)


def _centred_conv(x, w, b):
    l = x.shape[1]
    xp = jnp.pad(x, ((0, 0), (RG_CONV_LEFT, RG_CONV - 1 - RG_CONV_LEFT), (0, 0)))
    y = b
    for k in range(RG_CONV):
        y = y + w[k] * xp[:, k:k + l]
    return y


def _rglru_coeffs(xc, w_a, b_a, w_i, b_i, lam):
    bn, l, wd = xc.shape
    xh = xc.reshape(bn, l, RG_HEADS, RG_HEAD_DIM)
    r = jax.nn.sigmoid(jnp.einsum('blhd,hde->blhe', xh, w_a) + b_a).reshape(bn, l, wd).astype(jnp.float32)
    ig = jax.nn.sigmoid(jnp.einsum('blhd,hde->blhe', xh, w_i) + b_i).reshape(bn, l, wd).astype(jnp.float32)
    log_a = -RG_C * r * jax.nn.softplus(-lam.astype(jnp.float32))
    a = jnp.exp(log_a)
    b = jnp.sqrt(-jnp.expm1(2.0 * log_a)) * ig * xc.astype(jnp.float32)
    return a, b


def _linear_scan(a, b, h0):
    def comb(lft, rgt):
        return (lft[0] * rgt[0], rgt[0] * lft[1] + rgt[1])
    a_cum, b_cum = lax.associative_scan(comb, (a, b), axis=1)
    h = a_cum * h0[:, None, :] + b_cum
    return h, h[:, -1]


def _rglru_dir(xc_ctx, xc_lat, w_a, b_a, w_i, b_i, lam, reverse):
    flip = (lambda t: jnp.flip(t, axis=1)) if reverse else (lambda t: t)
    h0 = jnp.zeros((xc_ctx.shape[0], RG_WIDTH), jnp.float32)
    a_c, b_c = _rglru_coeffs(flip(xc_ctx), w_a, b_a, w_i, b_i, lam)
    h_c, s_c = _linear_scan(a_c, b_c, h0)
    a_l, b_l = _rglru_coeffs(flip(xc_lat), w_a, b_a, w_i, b_i, lam)
    h_l, _ = _linear_scan(a_l, b_l, s_c)
    return flip(h_c), flip(h_l)


def _rglru_mixer(h_ctx, h_lat, w_in, conv_w, conv_b, w_a, b_a, w_i, b_i, lam, w_out, need_ctx):
    xb_c, gb_c = jnp.split(h_ctx @ w_in, 2, axis=-1)
    xb_l, gb_l = jnp.split(h_lat @ w_in, 2, axis=-1)
    xc_c = _centred_conv(xb_c, conv_w, conv_b)
    xc_l = _centred_conv(xb_l, conv_w, conv_b)
    hcf, hlf = _rglru_dir(xc_c, xc_l, w_a[0], b_a[0], w_i[0], b_i[0], lam[0], False)
    hcb, hlb = _rglru_dir(xc_c, xc_l, w_a[1], b_a[1], w_i[1], b_i[1], lam[1], True)
    y_l = ((hlf + hlb).astype(h_lat.dtype) * jax.nn.gelu(gb_l)) @ w_out
    y_c = ((hcf + hcb).astype(h_ctx.dtype) * jax.nn.gelu(gb_c)) @ w_out if need_ctx else None
    return y_c, y_l


def _hgrn2_chunk_scan(q, k, v, log_f, s0):
    bn, l, h, _ = q.shape
    n = l // HG_CHUNK

    def chunks(t):
        return t.reshape(bn, n, HG_CHUNK, h, t.shape[-1]).transpose(1, 0, 3, 2, 4)

    mask = jnp.tril(jnp.ones((HG_CHUNK, HG_CHUNK), dtype=bool))

    def step(s, inp):
        qc, kc, vc, gc = inp
        b = jnp.cumsum(gc, axis=2)
        o_inter = jnp.einsum('bhjd,bhde->bhje', qc * jnp.exp(b), s)
        diff = jnp.where(mask[:, :, None], b[:, :, :, None, :] - b[:, :, None, :, :], -jnp.inf)
        att = jnp.einsum('bhjsd,bhsd->bhjs', qc[:, :, :, None, :] * jnp.exp(diff), kc)
        o_intra = jnp.einsum('bhjs,bhse->bhje', att, vc)
        b_last = b[:, :, -1:, :]
        s_new = jnp.exp(b_last[:, :, 0, :, None]) * s + jnp.einsum('bhsd,bhse->bhde', kc * jnp.exp(b_last - b), vc)
        return s_new, o_inter + o_intra

    s_fin, o = lax.scan(step, s0, (chunks(q), chunks(k), chunks(v), chunks(log_f)))
    o = o.transpose(1, 0, 3, 2, 4).reshape(bn, l, h, v.shape[-1])
    return o, s_fin


def _hgrn2_feats(h, w_in, lb):
    bn, l, _ = h.shape
    q, f_fwd, f_bwd, v, g = jnp.split(h @ w_in, 5, axis=-1)
    heads = lambda t: t.reshape(bn, l, HG_HEADS, -1).astype(jnp.float32)
    q = heads(q) * (HG_DK ** -0.5)
    gates = []
    for d, fz in enumerate((f_fwd, f_bwd)):
        z = heads(fz)
        lbd = lb[d]
        log_f = jnp.logaddexp(jnp.log(lbd), jnp.log1p(-lbd) + jax.nn.log_sigmoid(z))
        k = (1.0 - lbd) * jax.nn.sigmoid(-z)
        gates.append((k, log_f))
    return q, heads(v), heads(g), gates


def _hgrn2_dir(q_c, k_c, v_c, lf_c, q_l, k_l, v_l, lf_l, reverse):
    flip = (lambda t: jnp.flip(t, axis=1)) if reverse else (lambda t: t)
    s0 = jnp.zeros((q_c.shape[0], HG_HEADS, HG_DK, HG_DV), jnp.float32)
    o_c, s_c = _hgrn2_chunk_scan(flip(q_c), flip(k_c), flip(v_c), flip(lf_c), s0)
    o_l, _ = _hgrn2_chunk_scan(flip(q_l), flip(k_l), flip(v_l), flip(lf_l), s_c)
    return flip(o_c), flip(o_l)


def _hgrn2_out(o, g, norm_w, w_out, dtype):
    bn, l = o.shape[:2]
    o = o * lax.rsqrt(jnp.mean(o * o, axis=-1, keepdims=True) + EPS) * norm_w * jax.nn.silu(g)
    return o.reshape(bn, l, HG_HEADS * HG_DV).astype(dtype) @ w_out


def _hgrn2_mixer(h_ctx, h_lat, w_in, lb, norm_w, w_out, need_ctx):
    qc, vc, gc, gates_c = _hgrn2_feats(h_ctx, w_in, lb)
    ql, vl, gl, gates_l = _hgrn2_feats(h_lat, w_in, lb)
    ocf, olf = _hgrn2_dir(qc, gates_c[0][0], vc, gates_c[0][1], ql, gates_l[0][0], vl, gates_l[0][1], False)
    ocb, olb = _hgrn2_dir(qc, gates_c[1][0], vc, gates_c[1][1], ql, gates_l[1][0], vl, gates_l[1][1], True)
    y_l = _hgrn2_out(olf + olb, gl, norm_w, w_out, h_lat.dtype)
    y_c = _hgrn2_out(ocf + ocb, gc, norm_w, w_out, h_ctx.dtype) if need_ctx else None
    return y_c, y_l


def setup_inputs(seed: int = 0) -> dict:
    key = jax.random.key(seed)
    ks = jax.random.split(key, 32)
    f32 = jnp.float32
    nrm = lambda k, shape, s: jax.random.normal(k, shape, f32) * s
    D, W, F = D_MODEL, RG_WIDTH, D_FF
    u = jax.random.uniform(ks[14], (N_RG, 2, W), f32, RG_A_MIN, RG_A_MAX)
    s = u ** (1.0 / RG_C)
    return {
        'x': nrm(ks[0], (BATCH, SEQ, D), 1.0),
        'c': nrm(ks[1], (BATCH, D), 1.0),
        'ctx': nrm(ks[2], (BATCH, CTX_LEN, D), 1.0),
        'c_ctx': nrm(ks[3], (D,), 1.0),
        'w_ada': nrm(ks[4], (DEPTH, D, N_MOD * D), 0.5 * D ** -0.5),
        'b_ada': nrm(ks[5], (DEPTH, N_MOD * D), 0.01),
        'g_mix': 1.0 + nrm(ks[6], (DEPTH, D), 0.02),
        'g_ffn': 1.0 + nrm(ks[7], (DEPTH, D), 0.02),
        'g_final': 1.0 + nrm(ks[8], (D,), 0.02),
        'w_ffn_in': nrm(ks[9], (DEPTH, D, 2 * F), D ** -0.5),
        'w_ffn_out': nrm(ks[10], (DEPTH, F, D), F ** -0.5),
        'rg_w_in': nrm(ks[11], (N_RG, D, 2 * W), D ** -0.5),
        'rg_conv_w': nrm(ks[12], (N_RG, RG_CONV, W), RG_CONV ** -0.5),
        'rg_conv_b': nrm(ks[13], (N_RG, W), 0.01),
        'rg_w_a': nrm(ks[15], (N_RG, 2, RG_HEADS, RG_HEAD_DIM, RG_HEAD_DIM), RG_HEAD_DIM ** -0.5),
        'rg_b_a': nrm(ks[16], (N_RG, 2, RG_HEADS, RG_HEAD_DIM), 0.01),
        'rg_w_i': nrm(ks[17], (N_RG, 2, RG_HEADS, RG_HEAD_DIM, RG_HEAD_DIM), RG_HEAD_DIM ** -0.5),
        'rg_b_i': nrm(ks[18], (N_RG, 2, RG_HEADS, RG_HEAD_DIM), 0.01),
        'rg_lam': jnp.log(s) - jnp.log1p(-s),
        'rg_w_out': nrm(ks[19], (N_RG, W, D), W ** -0.5),
        'hg_w_in': nrm(ks[20], (N_HG, D, 5 * D), D ** -0.5),
        'hg_lb': nrm(ks[21], (DEPTH, 2, HG_HEADS * HG_DK), 0.1),
        'hg_norm': 1.0 + nrm(ks[22], (N_HG, HG_DV), 0.02),
        'hg_w_out': nrm(ks[23], (N_HG, HG_HEADS * HG_DV, D), D ** -0.5),
    }


def reference(x, c, ctx, c_ctx, w_ada, b_ada, g_mix, g_ffn, g_final, w_ffn_in, w_ffn_out,
              rg_w_in, rg_conv_w, rg_conv_b, rg_w_a, rg_b_a, rg_w_i, rg_b_i, rg_lam, rg_w_out,
              hg_w_in, hg_lb, hg_norm, hg_w_out):
    rows = x.shape[1] // GRID_W
    lb_all = jnp.cumsum(jax.nn.softmax(hg_lb.astype(jnp.float32), axis=0), axis=0)
    xl, xc = x, ctx
    for i in range(DEPTH):
        last = i == DEPTH - 1
        sh1, sc1, ga1, sh2, sc2, ga2 = _ada(c, w_ada[i], b_ada[i])
        csh1, csc1, cga1, csh2, csc2, cga2 = _ada(c_ctx[None, :], w_ada[i], b_ada[i])
        hl = _modulate(xl, g_mix[i], sh1, sc1)
        hc = _modulate(xc, g_mix[i], csh1, csc1)
        j = i // N_MIXERS
        if i % N_MIXERS == 0:
            yc, yl = _rglru_mixer(hc, hl, rg_w_in[j], rg_conv_w[j], rg_conv_b[j], rg_w_a[j], rg_b_a[j],
                                  rg_w_i[j], rg_b_i[j], rg_lam[j], rg_w_out[j], not last)
        else:
            lb = (lb_all[i] - lb_all[0]).reshape(2, HG_HEADS, HG_DK)
            yc, yl = _hgrn2_mixer(hc, _to_col_major(hl, rows), hg_w_in[j], lb, hg_norm[j], hg_w_out[j], not last)
            yl = _from_col_major(yl, rows)
        xl = xl + ga1 * yl
        xl = xl + ga2 * _swiglu(_modulate(xl, g_ffn[i], sh2, sc2), w_ffn_in[i], w_ffn_out[i])
        if not last:
            xc = xc + cga1 * yc
            xc = xc + cga2 * _swiglu(_modulate(xc, g_ffn[i], csh2, csc2), w_ffn_in[i], w_ffn_out[i])
    return _rms(xl, g_final)
```

```python
import functools

import jax
import jax.numpy as jnp
from jax import lax
from jax.experimental import pallas as pl
from jax.experimental.pallas import tpu as pltpu

F32 = jnp.float32
BF16 = jnp.bfloat16

EPS = 1e-6
RG_C = 8.0
GRID_W = 64
N_MOD = 6
CONV_TAPS = 4
CONV_LEFT = 2

V7X_LANES = 128
V7X_SUBLANES = 8
V7X_VMEM_LIMIT_CAP = 56 * 1024 * 1024

HG_CHUNK = 32
EXP_CLAMP = 80.0


def _vmem_limit(nbytes):
    return int(min(V7X_VMEM_LIMIT_CAP, nbytes))


def _col_tile(n, preferred=512):
    t = preferred
    while n % t:
        t -= V7X_LANES
    return t


def _params(sem, nbytes):
    return pltpu.CompilerParams(dimension_semantics=sem, vmem_limit_bytes=_vmem_limit(nbytes))


def _ada_kernel(c_ref, w_ref, b_ref, o_ref):
    c = c_ref[...]
    s = (c * jax.nn.sigmoid(c)).astype(BF16)
    o_ref[...] = jnp.dot(s, w_ref[...].astype(BF16), preferred_element_type=F32) + b_ref[...]


def _ada_call(cvec, w_ada, b_ada):
    depth, d, n = w_ada.shape
    r = cvec.shape[0]
    tn = 512
    est = 2 * (d * tn * 4) + d * tn * 2 + 4 * r * (d + 2 * tn) * 4 + (4 << 20)
    return pl.pallas_call(
        _ada_kernel,
        grid=(depth, n // tn),
        in_specs=[
            pl.BlockSpec((r, d), lambda l, j: (0, 0)),
            pl.BlockSpec((None, d, tn), lambda l, j: (l, 0, j)),
            pl.BlockSpec((None, 1, tn), lambda l, j: (l, 0, j)),
        ],
        out_specs=pl.BlockSpec((None, r, tn), lambda l, j: (l, 0, j)),
        out_shape=jax.ShapeDtypeStruct((depth, r, n), F32),
        compiler_params=_params(("parallel", "parallel"), est),
        name="ada_mod",
    )(cvec, w_ada, b_ada.reshape(depth, 1, n))


ROW_CHUNK = 128


def _is_ctx_rows(tm, tiles_per_batch, ctx_len, first_row=0, nrows=None):
    nrows = tm if nrows is None else nrows
    bt = pl.program_id(0) % tiles_per_batch
    row = bt * tm + first_row + lax.broadcasted_iota(jnp.int32, (nrows, 1), 0)
    return row < ctx_len


def _for_row_chunks(tm, fn):
    def body(r, _):
        fn(pl.ds(pl.multiple_of(r * ROW_CHUNK, ROW_CHUNK), ROW_CHUNK), r * ROW_CHUNK)
        return 0

    lax.fori_loop(0, tm // ROW_CHUNK, body, 0)


def _rms_modulate_into(h_ref, x_ref, g_ref, shb_ref, scb_ref, shc_ref, scc_ref, *, tm, tiles_per_batch, ctx_len):
    def chunk(rows, first_row):
        x = x_ref[rows, :]
        is_ctx = _is_ctx_rows(tm, tiles_per_batch, ctx_len, first_row, ROW_CHUNK)
        y = x * lax.rsqrt(jnp.mean(x * x, axis=-1, keepdims=True) + EPS) * g_ref[...]
        shift = jnp.where(is_ctx, shc_ref[...], shb_ref[...])
        scale = jnp.where(is_ctx, scc_ref[...], scb_ref[...])
        h_ref[rows, :] = (y * (1.0 + scale) + shift).astype(h_ref.dtype)

    _for_row_chunks(tm, chunk)


def _mod_specs(d, tiles_per_batch, n_batch, ncols=None):
    if ncols is None:
        return [pl.BlockSpec((None, 1, d), lambda i, j: (i // tiles_per_batch, 0, 0)),
                pl.BlockSpec((None, 1, d), lambda i, j: (n_batch, 0, 0))]
    return [pl.BlockSpec((None, 1, ncols), lambda i, j: (i // tiles_per_batch, 0, j)),
            pl.BlockSpec((None, 1, ncols), lambda i, j: (n_batch, 0, j))]


def _proj_kernel(x_ref, g_ref, shb_ref, shc_ref, scb_ref, scc_ref, w_ref, o_ref, h_ref, *, tm, tiles_per_batch, ctx_len):
    @pl.when(pl.program_id(1) == 0)
    def _():
        _rms_modulate_into(h_ref, x_ref, g_ref, shb_ref, scb_ref, shc_ref, scc_ref,
                           tm=tm, tiles_per_batch=tiles_per_batch, ctx_len=ctx_len)

    o_ref[...] = jnp.dot(h_ref[...], w_ref[...], preferred_element_type=F32)


def _proj_call(xs, g, shift, scale, w, *, tm, lb, ctx_len, n_batch):
    t, d = xs.shape
    n = w.shape[1]
    tn = _col_tile(n)
    tpb = lb // tm
    est = 2 * tm * d * 4 + tm * d * 2 + 2 * d * tn * 2 + 3 * tm * tn * 4 + 8 * ROW_CHUNK * d * 4 + (2 << 20)
    kern = functools.partial(_proj_kernel, tm=tm, tiles_per_batch=tpb, ctx_len=ctx_len)
    return pl.pallas_call(
        kern,
        grid=(t // tm, n // tn),
        in_specs=[pl.BlockSpec((tm, d), lambda i, j: (i, 0)),
                  pl.BlockSpec((1, d), lambda i, j: (0, 0)),
                  *_mod_specs(d, tpb, n_batch), *_mod_specs(d, tpb, n_batch),
                  pl.BlockSpec((d, tn), lambda i, j: (0, j))],
        out_specs=pl.BlockSpec((tm, tn), lambda i, j: (i, j)),
        out_shape=jax.ShapeDtypeStruct((t, n), F32),
        scratch_shapes=[pltpu.VMEM((tm, d), BF16)],
        compiler_params=_params(("parallel", "arbitrary"), est),
        name="norm_mod_proj",
    )(xs, g, shift, shift, scale, scale, w)


def _outproj_kernel(a_ref, w_ref, x_ref, gab_ref, gac_ref, o_ref, *, tm, tiles_per_batch, ctx_len):
    acc = jnp.dot(a_ref[...].astype(BF16), w_ref[...], preferred_element_type=F32)
    gate = jnp.where(_is_ctx_rows(tm, tiles_per_batch, ctx_len), gac_ref[...], gab_ref[...])
    o_ref[...] = x_ref[...] + gate * acc


def _outproj_call(a, w, xs, gate, *, tm, lb, ctx_len, n_batch):
    t, k = a.shape
    n = w.shape[1]
    tn = _col_tile(n)
    tpb = lb // tm
    est = 2 * tm * k * a.dtype.itemsize + tm * k * 2 + 2 * k * tn * 2 + 4 * tm * tn * 4 + 2 * tm * tn * 4 + (2 << 20)
    kern = functools.partial(_outproj_kernel, tm=tm, tiles_per_batch=tpb, ctx_len=ctx_len)
    return pl.pallas_call(
        kern,
        grid=(t // tm, n // tn),
        in_specs=[pl.BlockSpec((tm, k), lambda i, j: (i, 0)),
                  pl.BlockSpec((k, tn), lambda i, j: (0, j)),
                  pl.BlockSpec((tm, tn), lambda i, j: (i, j)),
                  *_mod_specs(n, tpb, n_batch, ncols=tn)],
        out_specs=pl.BlockSpec((tm, tn), lambda i, j: (i, j)),
        out_shape=jax.ShapeDtypeStruct((t, n), F32),
        compiler_params=_params(("parallel", "parallel"), est),
        name="outproj_residual",
    )(a, w, xs, gate, gate)


def _ffn_kernel(x_ref, g_ref, shb_ref, shc_ref, scb_ref, scc_ref, gab_ref, gac_ref, wg_ref, wu_ref, wo_ref,
                o_ref, h_ref, *, tm, tiles_per_batch, ctx_len):
    f = pl.program_id(1)

    @pl.when(f == 0)
    def _():
        _rms_modulate_into(h_ref, x_ref, g_ref, shb_ref, scb_ref, shc_ref, scc_ref,
                           tm=tm, tiles_per_batch=tiles_per_batch, ctx_len=ctx_len)
        o_ref[...] = jnp.zeros_like(o_ref)

    h = h_ref[...]
    gate = jnp.dot(h, wg_ref[...], preferred_element_type=F32)
    up = jnp.dot(h, wu_ref[...], preferred_element_type=F32)
    act = (gate * jax.nn.sigmoid(gate) * up).astype(BF16)
    tn = _col_tile(o_ref.shape[1])
    for c0 in range(0, o_ref.shape[1], tn):
        o_ref[:, c0:c0 + tn] += jnp.dot(act, wo_ref[:, c0:c0 + tn], preferred_element_type=F32)

    @pl.when(f == pl.num_programs(1) - 1)
    def _():
        def chunk(rows, first_row):
            is_ctx = _is_ctx_rows(tm, tiles_per_batch, ctx_len, first_row, ROW_CHUNK)
            res_gate = jnp.where(is_ctx, gac_ref[...], gab_ref[...])
            o_ref[rows, :] = x_ref[rows, :] + res_gate * o_ref[rows, :]

        _for_row_chunks(tm, chunk)


def _ffn_call(xs, g, shift, scale, res_gate, w_in, w_out, *, tm, lb, ctx_len, n_batch):
    t, d = xs.shape
    ff = w_out.shape[0]
    tf = _col_tile(ff)
    nf = ff // tf
    tpb = lb // tm
    est = (4 * tm * d * 4 + tm * d * 2 + 2 * (2 * d * tf * 2 + tf * d * 2) + 4 * tm * tf * 4 + tm * tf * 2
           + 8 * ROW_CHUNK * d * 4 + (2 << 20))
    kern = functools.partial(_ffn_kernel, tm=tm, tiles_per_batch=tpb, ctx_len=ctx_len)
    return pl.pallas_call(
        kern,
        grid=(t // tm, nf),
        in_specs=[pl.BlockSpec((tm, d), lambda i, f: (i, 0)),
                  pl.BlockSpec((1, d), lambda i, f: (0, 0)),
                  *_mod_specs(d, tpb, n_batch), *_mod_specs(d, tpb, n_batch), *_mod_specs(d, tpb, n_batch),
                  pl.BlockSpec((d, tf), lambda i, f: (0, f)),
                  pl.BlockSpec((d, tf), lambda i, f: (0, nf + f)),
                  pl.BlockSpec((tf, d), lambda i, f: (f, 0))],
        out_specs=pl.BlockSpec((tm, d), lambda i, f: (i, 0)),
        out_shape=jax.ShapeDtypeStruct((t, d), F32),
        scratch_shapes=[pltpu.VMEM((tm, d), BF16)],
        compiler_params=_params(("parallel", "arbitrary"), est),
        name="swiglu_ffn",
    )(xs, g, shift, shift, scale, scale, res_gate, res_gate, w_in, w_in, w_out)


def _final_kernel(x_ref, g_ref, o_ref):
    x = x_ref[...]
    o_ref[...] = x * lax.rsqrt(jnp.mean(x * x, axis=-1, keepdims=True) + EPS) * g_ref[...]


def _final_call(xs, g, *, n_batch, ctx_len, seq_len):
    d = xs.shape[1]
    tr = ctx_len
    per = (ctx_len + seq_len) // tr
    est = 4 * tr * d * 4 + 2 * tr * d * 4 + (2 << 20)
    return pl.pallas_call(
        _final_kernel,
        grid=(n_batch, seq_len // tr),
        in_specs=[pl.BlockSpec((tr, d), lambda b, k: (b * per + 1 + k, 0)),
                  pl.BlockSpec((1, d), lambda b, k: (0, 0))],
        out_specs=pl.BlockSpec((None, tr, d), lambda b, k: (b, k, 0)),
        out_shape=jax.ShapeDtypeStruct((n_batch, seq_len, d), F32),
        compiler_params=_params(("parallel", "parallel"), est),
        name="final_rmsnorm",
    )(xs, g)


def _softplus(x):
    return jnp.maximum(x, 0.0) + jnp.log1p(jnp.exp(-jnp.abs(x)))


def _chunk_carries(h_loc, p_loc, h_init, reverse):
    n = h_loc.shape[0]
    carries = [None] * n
    c = h_init
    for s in (reversed(range(n)) if reverse else range(n)):
        carries[s] = c
        c = p_loc[s:s + 1] * c + h_loc[s:s + 1]
    return carries, c


def _strided_scan(af_ref, bf_ref, ab_ref, bb_ref, n, stride):
    nsub = V7X_SUBLANES
    lanes = af_ref.shape[1]

    def body(i, carry):
        hf, pf, hb, pb = carry
        rows_f = pl.ds(i, nsub, stride=stride)
        rows_b = pl.ds(n - 1 - i, nsub, stride=stride)
        a = af_ref[rows_f, :]
        hf = a * hf + bf_ref[rows_f, :]
        pf = a * pf
        bf_ref[rows_f, :] = hf
        af_ref[rows_f, :] = pf
        a = ab_ref[rows_b, :]
        hb = a * hb + bb_ref[rows_b, :]
        pb = a * pb
        bb_ref[rows_b, :] = hb
        ab_ref[rows_b, :] = pb
        return hf, pf, hb, pb

    zero = jnp.zeros((nsub, lanes), F32)
    one = jnp.ones((nsub, lanes), F32)
    return lax.fori_loop(0, n, body, (zero, one, zero, one), unroll=8)


def _rg_kernel(xb_ref, gb_ref, cw_ref, cb_ref, wg_ref, bg_ref, lam_ref, y_ref,
               xp_ref, laf, lbf, lab, lbb, caf, cbf, cab, cbb, carry_ref, *, ctx_len, seq_len):
    nsub = V7X_SUBLANES
    lanes = xb_ref.shape[1]
    rb = ctx_len
    nblk = 1 + seq_len // rb
    halo = nsub
    lat_stride = rb + nsub
    ctx_n = ctx_len // nsub
    ctx_stride = ctx_n + nsub

    xp_ref[pl.ds(0, halo), :] = jnp.zeros((halo, lanes), F32)
    xp_ref[pl.ds(halo + ctx_len + seq_len, halo), :] = jnp.zeros((halo, lanes), F32)
    xp_ref[pl.ds(halo, ctx_len + seq_len), :] = xb_ref[...]

    cw = cw_ref[...]
    cb = cb_ref[...]
    softplus_neg_lam = _softplus(-lam_ref[...])
    local_row = lax.broadcasted_iota(jnp.int32, (rb, 1), 0)

    def coeffs(blk):
        start = pl.multiple_of(blk * rb, rb)
        win = xp_ref[pl.ds(start, rb + 2 * halo), :]
        seg_first = blk <= 1
        seg_last = jnp.logical_or(blk == 0, blk == nblk - 1)
        taps = []
        for k in range(CONV_TAPS):
            off = k - CONV_LEFT
            tap = win[halo + off:halo + off + rb]
            if off < 0:
                tap = jnp.where(jnp.logical_or(local_row >= -off, jnp.logical_not(seg_first)), tap, 0.0)
            elif off > 0:
                tap = jnp.where(jnp.logical_or(local_row < rb - off, jnp.logical_not(seg_last)), tap, 0.0)
            taps.append(tap)
        xc = cb
        for k in range(CONV_TAPS):
            xc = xc + cw[k:k + 1] * taps[k]
        gates = jnp.dot(xc.astype(BF16), wg_ref[...], preferred_element_type=F32) + bg_ref[...]
        out = []
        for d in range(2):
            r = jax.nn.sigmoid(gates[:, (2 * d) * lanes:(2 * d + 1) * lanes])
            ig = jax.nn.sigmoid(gates[:, (2 * d + 1) * lanes:(2 * d + 2) * lanes])
            log_a = (-RG_C) * r * softplus_neg_lam[d:d + 1]
            a = jnp.exp(log_a)
            t = jnp.tanh(log_a)
            mult = jnp.sqrt((-2.0) * t / (1.0 - t))
            out.append((a, mult * ig * xc))
        return out

    (a_f, b_f), (a_b, b_b) = coeffs(0)
    for c in range(nsub):
        dst = pl.ds(c * ctx_stride, ctx_n)
        src = slice(c * ctx_n, (c + 1) * ctx_n)
        caf[dst, :] = a_f[src]
        cbf[dst, :] = b_f[src]
        cab[dst, :] = a_b[src]
        cbb[dst, :] = b_b[src]

    def fill(blk, _):
        (a_f, b_f), (a_b, b_b) = coeffs(blk)
        dst = pl.ds(pl.multiple_of((blk - 1) * lat_stride, nsub), rb)
        laf[dst, :] = a_f
        lbf[dst, :] = b_f
        lab[dst, :] = a_b
        lbb[dst, :] = b_b
        return 0

    lax.fori_loop(1, nblk, fill, 0)

    zero_state = jnp.zeros((1, lanes), F32)
    hf, pf, hb, pb = _strided_scan(caf, cbf, cab, cbb, ctx_n, ctx_stride)
    ctx_in_f, ctx_state_f = _chunk_carries(hf, pf, zero_state, reverse=False)
    ctx_in_b, ctx_state_b = _chunk_carries(hb, pb, zero_state, reverse=True)
    hf, pf, hb, pb = _strided_scan(laf, lbf, lab, lbb, rb, lat_stride)
    lat_in_f, _ = _chunk_carries(hf, pf, ctx_state_f, reverse=False)
    lat_in_b, _ = _chunk_carries(hb, pb, ctx_state_b, reverse=True)
    for s in range(nsub):
        carry_ref[pl.ds(s, 1), :] = lat_in_f[s]
        carry_ref[pl.ds(nsub + s, 1), :] = lat_in_b[s]

    for c in range(nsub):
        src = pl.ds(c * ctx_stride, ctx_n)
        h = cbf[src, :] + caf[src, :] * ctx_in_f[c] + cbb[src, :] + cab[src, :] * ctx_in_b[c]
        rows = pl.ds(c * ctx_n, ctx_n)
        y_ref[rows, :] = h * jax.nn.gelu(gb_ref[rows, :])

    def emit(blk, _):
        src = pl.ds(pl.multiple_of((blk - 1) * lat_stride, nsub), rb)
        cf = carry_ref[pl.ds(blk - 1, 1), :]
        cbw = carry_ref[pl.ds(nsub + blk - 1, 1), :]
        h = lbf[src, :] + laf[src, :] * cf + lbb[src, :] + lab[src, :] * cbw
        rows = pl.ds(pl.multiple_of(blk * rb, rb), rb)
        y_ref[rows, :] = h * jax.nn.gelu(gb_ref[rows, :])
        return 0

    lax.fori_loop(1, nblk, emit, 0)


def _rg_call(p, conv_w, conv_b, w_gates, b_gates, lam, *, n_batch, ctx_len, seq_len):
    t, w2 = p.shape
    w = w2 // 2
    lanes = V7X_LANES
    nh = w // lanes
    lb = ctx_len + seq_len
    nsub = V7X_SUBLANES
    lat_rows = nsub * (seq_len // nsub + nsub)
    ctx_rows = nsub * (ctx_len // nsub + nsub)
    blk_bytes = lb * lanes * 4
    est = 6 * blk_bytes + (lb + 2 * nsub) * lanes * 4 + 4 * (lat_rows + ctx_rows) * lanes * 4 + 24 * ctx_len * lanes * 4 + (4 << 20)
    kern = functools.partial(_rg_kernel, ctx_len=ctx_len, seq_len=seq_len)
    return pl.pallas_call(
        kern,
        grid=(n_batch, nh),
        in_specs=[pl.BlockSpec((lb, lanes), lambda b, h: (b, h)),
                  pl.BlockSpec((lb, lanes), lambda b, h: (b, nh + h)),
                  pl.BlockSpec((CONV_TAPS, lanes), lambda b, h: (0, h)),
                  pl.BlockSpec((1, lanes), lambda b, h: (0, h)),
                  pl.BlockSpec((None, lanes, 4 * lanes), lambda b, h: (h, 0, 0)),
                  pl.BlockSpec((None, 1, 4 * lanes), lambda b, h: (h, 0, 0)),
                  pl.BlockSpec((2, lanes), lambda b, h: (0, h))],
        out_specs=pl.BlockSpec((lb, lanes), lambda b, h: (b, h)),
        out_shape=jax.ShapeDtypeStruct((t, w), F32),
        scratch_shapes=[pltpu.VMEM((lb + 2 * nsub, lanes), F32)]
        + [pltpu.VMEM((lat_rows, lanes), F32)] * 4
        + [pltpu.VMEM((ctx_rows, lanes), F32)] * 4
        + [pltpu.VMEM((2 * nsub, lanes), F32)],
        compiler_params=_params(("parallel", "parallel"), est),
        name="rglru_mixer",
    )(p, p, conv_w, conv_b, w_gates, b_gates, lam)


def _group_cumsum(x, group, reverse):
    n = x.shape[0]
    pos = lax.broadcasted_iota(jnp.int32, (n, 1), 0) % group
    step = 1
    while step < group:
        if reverse:
            shifted = pltpu.roll(x, n - step, axis=0)
            keep = pos < group - step
        else:
            shifted = pltpu.roll(x, step, axis=0)
            keep = pos >= step
        x = x + jnp.where(keep, shifted, 0.0)
        step *= 2
    return x


def _hg_kernel(q_ref, ff_ref, fb_ref, v_ref, g_ref, lb_ref, nw_ref, y_ref,
               qs, fs, bs, vs, gs, qd_ref, oacc, kvt_ref, dec_ref, stin_ref, *, ctx_len, seq_len, layer, dk):
    lanes = q_ref.shape[1]
    ch = HG_CHUNK
    lb_rows = ctx_len + seq_len
    nchunk = lb_rows // ch
    ctx_chunks = ctx_len // ch
    sb = ctx_len
    nsb = lb_rows // sb
    cps = sb // ch
    ncol = seq_len // ch

    rows = pl.ds(0, ctx_len)
    qs[rows, :] = q_ref[rows, :]
    fs[rows, :] = ff_ref[rows, :]
    bs[rows, :] = fb_ref[rows, :]
    vs[rows, :] = v_ref[rows, :]
    gs[rows, :] = g_ref[rows, :]

    def gather(n, _):
        src = pl.ds(ctx_len + n, ch, stride=ncol)
        dst = pl.ds(pl.multiple_of(ctx_len + n * ch, ch), ch)
        qs[dst, :] = q_ref[src, :]
        fs[dst, :] = ff_ref[src, :]
        bs[dst, :] = fb_ref[src, :]
        vs[dst, :] = v_ref[src, :]
        gs[dst, :] = g_ref[src, :]
        return 0

    lax.fori_loop(0, ncol, gather, 0)

    lbp = lb_ref[...]
    e = jnp.exp(lbp - jnp.max(lbp, axis=0, keepdims=True))
    sm = e / jnp.sum(e, axis=0, keepdims=True)
    lower = jnp.zeros(sm.shape[1:], F32)
    for l in range(1, layer + 1):
        lower = lower + sm[l]

    q_scale = dk ** -0.5
    j_idx = lax.broadcasted_iota(jnp.int32, (cps, ch, ch), 1)
    s_idx = lax.broadcasted_iota(jnp.int32, (cps, ch, ch), 2)

    for d in range(2):
        reverse = d == 1
        f_src = bs if reverse else fs
        lo = lower[d:d + 1]

        def local_part(i, _):
            rows = pl.ds(pl.multiple_of(i * sb, sb), sb)
            z = f_src[rows, :]
            log_f = jnp.log(lo + (1.0 - lo) * jax.nn.sigmoid(z))
            k = (1.0 - lo) * jax.nn.sigmoid(-z)
            cum = _group_cumsum(log_f, ch, reverse)
            cum3 = cum.reshape(cps, ch, lanes)
            total3 = cum3[:, 0:1, :] if reverse else cum3[:, ch - 1:ch, :]
            q = qs[rows, :] * q_scale
            v3 = vs[rows, :].reshape(cps, ch, lanes).astype(BF16)
            qd = (q * jnp.exp(cum)).astype(BF16)
            kd = (k * jnp.exp(jnp.minimum(-cum, EXP_CLAMP))).astype(BF16)
            kl = (k.reshape(cps, ch, lanes) * jnp.exp(total3 - cum3)).astype(BF16)
            qd3 = qd.reshape(cps, ch, lanes)
            att = jnp.einsum("cjd,csd->cjs", qd3, kd.reshape(cps, ch, lanes), preferred_element_type=F32)
            att = jnp.where((j_idx <= s_idx) if reverse else (j_idx >= s_idx), att, 0.0)
            o_intra = jnp.einsum("cjs,cse->cje", att.astype(BF16), v3, preferred_element_type=F32)
            if reverse:
                oacc[rows, :] += o_intra.reshape(sb, lanes)
            else:
                oacc[rows, :] = o_intra.reshape(sb, lanes)
            qd_ref[rows, :] = qd
            for c in range(cps):
                kvt = lax.dot_general(v3[c], kl[c], (((0,), (0,)), ((), ())), preferred_element_type=F32)
                kvt_ref[i * cps + c] = kvt
            dec_ref[pl.ds(i * cps, cps)] = jnp.exp(total3)
            return 0

        lax.fori_loop(0, nsb, local_part, 0)

        def state_step(n, st):
            stin_ref[n] = st.astype(BF16)
            return dec_ref[n] * st + kvt_ref[n]

        st = jnp.zeros((lanes, lanes), F32)
        if reverse:
            st = lax.fori_loop(0, ctx_chunks, lambda i, s: state_step(ctx_chunks - 1 - i, s), st)
            st = lax.fori_loop(0, nchunk - ctx_chunks, lambda i, s: state_step(nchunk - 1 - i, s), st)
        else:
            st = lax.fori_loop(0, nchunk, state_step, st)

        def inter_part(i, _):
            rows = pl.ds(pl.multiple_of(i * sb, sb), sb)
            qd3 = qd_ref[rows, :].reshape(cps, ch, lanes)
            st3 = stin_ref[pl.ds(i * cps, cps)]
            o_inter = jnp.einsum("cjd,ced->cje", qd3, st3, preferred_element_type=F32)
            oacc[rows, :] += o_inter.reshape(sb, lanes)
            return 0

        lax.fori_loop(0, nsb, inter_part, 0)

    nw = nw_ref[...]

    def finish(i, _):
        rows = pl.ds(pl.multiple_of(i * sb, sb), sb)
        o = oacc[rows, :]
        g = gs[rows, :]
        o = o * lax.rsqrt(jnp.mean(o * o, axis=-1, keepdims=True) + EPS) * nw * (g * jax.nn.sigmoid(g))
        oacc[rows, :] = o
        return 0

    lax.fori_loop(0, nsb, finish, 0)

    rows = pl.ds(0, ctx_len)
    y_ref[rows, :] = oacc[rows, :]

    def scatter(n, _):
        src = pl.ds(pl.multiple_of(ctx_len + n * ch, ch), ch)
        y_ref[pl.ds(ctx_len + n, ch, stride=ncol), :] = oacc[src, :]
        return 0

    lax.fori_loop(0, ncol, scatter, 0)


def _hg_call(p, hg_lb, norm_w, *, n_batch, ctx_len, seq_len, layer):
    t, w5 = p.shape
    w = w5 // 5
    lanes = V7X_LANES
    nh = w // lanes
    lb = ctx_len + seq_len
    nchunk = lb // HG_CHUNK
    depth = hg_lb.shape[0]
    blk_bytes = lb * lanes * 4
    est = (12 * blk_bytes + 7 * blk_bytes + nchunk * lanes * lanes * 6 + nchunk * V7X_SUBLANES * lanes * 4
           + 40 * ctx_len * lanes * 4 + (4 << 20))
    kern = functools.partial(_hg_kernel, ctx_len=ctx_len, seq_len=seq_len, layer=layer, dk=lanes)
    col = lambda k: pl.BlockSpec((lb, lanes), lambda b, h: (b, k * nh + h))
    return pl.pallas_call(
        kern,
        grid=(n_batch, nh),
        in_specs=[col(0), col(1), col(2), col(3), col(4),
                  pl.BlockSpec((depth, 2, lanes), lambda b, h: (0, 0, h)),
                  pl.BlockSpec((1, lanes), lambda b, h: (0, 0))],
        out_specs=pl.BlockSpec((lb, lanes), lambda b, h: (b, h)),
        out_shape=jax.ShapeDtypeStruct((t, w), F32),
        scratch_shapes=[pltpu.VMEM((lb, lanes), F32)] * 5
        + [pltpu.VMEM((lb, lanes), BF16), pltpu.VMEM((lb, lanes), F32),
           pltpu.VMEM((nchunk, lanes, lanes), F32), pltpu.VMEM((nchunk, 1, lanes), F32),
           pltpu.VMEM((nchunk, lanes, lanes), BF16)],
        compiler_params=_params(("parallel", "parallel"), est),
        name="hgrn2_mixer",
    )(p, p, p, p, p, hg_lb, norm_w)


def kernel(x, c, ctx, c_ctx, w_ada, b_ada, g_mix, g_ffn, g_final, w_ffn_in, w_ffn_out,
           rg_w_in, rg_conv_w, rg_conv_b, rg_w_a, rg_b_a, rg_w_i, rg_b_i, rg_lam, rg_w_out,
           hg_w_in, hg_lb, hg_norm, hg_w_out):
    n_batch, seq_len, d = x.shape
    ctx_len = ctx.shape[1]
    depth = w_ada.shape[0]
    lanes = V7X_LANES
    lb = ctx_len + seq_len
    assert seq_len == V7X_SUBLANES * ctx_len, "row blocking assumes the latent is 8 context lengths long"
    assert seq_len // GRID_W == HG_CHUNK, "an HGRN2 chunk must be one latent grid column"
    assert d % lanes == 0 and rg_w_a.shape[-1] == lanes and hg_norm.shape[-1] == lanes

    xs = jnp.concatenate([ctx, x], axis=1).reshape(n_batch * lb, d)

    pad = (-(n_batch + 1)) % V7X_SUBLANES
    cvec = jnp.concatenate([c, c_ctx[None, :], jnp.zeros((pad, d), F32)], axis=0)
    mods = _ada_call(cvec, w_ada, b_ada).reshape(depth, cvec.shape[0], N_MOD, d)

    tm_proj = lb // 2
    tm_ffn = lb // 3
    tiles = dict(lb=lb, ctx_len=ctx_len, n_batch=n_batch)
    n_mixers = 2
    for i in range(depth):
        sh1, sc1, ga1, sh2, sc2, ga2 = [mods[i, :, k][:, None, :] for k in range(N_MOD)]
        j = i // n_mixers
        if i % n_mixers == 0:
            nh = rg_w_a.shape[2]
            p = _proj_call(xs, g_mix[i][None], sh1, sc1, rg_w_in[j].astype(BF16), tm=tm_proj, **tiles)
            w_gates = jnp.concatenate([rg_w_a[j, 0], rg_w_i[j, 0], rg_w_a[j, 1], rg_w_i[j, 1]], axis=-1).astype(BF16)
            b_gates = jnp.concatenate([rg_b_a[j, 0], rg_b_i[j, 0], rg_b_a[j, 1], rg_b_i[j, 1]], axis=-1)[:, None, :]
            y = _rg_call(p, rg_conv_w[j], rg_conv_b[j][None], w_gates, b_gates, rg_lam[j],
                         n_batch=n_batch, ctx_len=ctx_len, seq_len=seq_len)
            w_out = rg_w_out[j]
        else:
            p = _proj_call(xs, g_mix[i][None], sh1, sc1, hg_w_in[j].astype(BF16), tm=tm_proj, **tiles)
            y = _hg_call(p, hg_lb, hg_norm[j][None], n_batch=n_batch, ctx_len=ctx_len, seq_len=seq_len, layer=i)
            w_out = hg_w_out[j]
        xs = _outproj_call(y, w_out.astype(BF16), xs, ga1, tm=tm_proj, **tiles)
        xs = _ffn_call(xs, g_ffn[i][None], sh2, sc2, ga2, w_ffn_in[i].astype(BF16), w_ffn_out[i].astype(BF16),
                       tm=tm_ffn, **tiles)
    return _final_call(xs, g_final[None], n_batch=n_batch, ctx_len=ctx_len, seq_len=seq_len)
```

```python
import functools

import jax
import jax.numpy as jnp
from jax import lax
from jax.experimental import pallas as pl
from jax.experimental.pallas import tpu as pltpu

F32 = jnp.float32
BF16 = jnp.bfloat16

EPS = 1e-6
RG_C = 8.0
GRID_W = 64
N_MOD = 6
CONV_TAPS = 4
CONV_LEFT = 2

V7X_LANES = 128
V7X_SUBLANES = 8
V7X_VMEM_LIMIT_CAP = 56 * 1024 * 1024

HG_CHUNK = 32
EXP_CLAMP = 80.0


def _vmem_limit(nbytes):
    return int(min(V7X_VMEM_LIMIT_CAP, nbytes))


def _col_tile(n, preferred=512):
    t = preferred
    while n % t:
        t -= V7X_LANES
    return t


def _params(sem, nbytes):
    return pltpu.CompilerParams(dimension_semantics=sem, vmem_limit_bytes=_vmem_limit(nbytes))


def _ada_kernel(c_ref, w_ref, b_ref, o_ref):
    c = c_ref[...]
    s = (c * jax.nn.sigmoid(c)).astype(BF16)
    o_ref[...] = jnp.dot(s, w_ref[...].astype(BF16), preferred_element_type=F32) + b_ref[...]


def _ada_call(cvec, w_ada, b_ada):
    depth, d, n = w_ada.shape
    r = cvec.shape[0]
    tn = 512
    est = 2 * (d * tn * 4) + d * tn * 2 + 4 * r * (d + 2 * tn) * 4 + (4 << 20)
    return pl.pallas_call(
        _ada_kernel,
        grid=(depth, n // tn),
        in_specs=[
            pl.BlockSpec((r, d), lambda l, j: (0, 0)),
            pl.BlockSpec((None, d, tn), lambda l, j: (l, 0, j)),
            pl.BlockSpec((None, 1, tn), lambda l, j: (l, 0, j)),
        ],
        out_specs=pl.BlockSpec((None, r, tn), lambda l, j: (l, 0, j)),
        out_shape=jax.ShapeDtypeStruct((depth, r, n), F32),
        compiler_params=_params(("parallel", "parallel"), est),
        name="ada_mod",
    )(cvec, w_ada, b_ada.reshape(depth, 1, n))


ROW_CHUNK = 128


def _is_ctx_rows(tm, tiles_per_batch, ctx_len, first_row=0, nrows=None):
    nrows = tm if nrows is None else nrows
    bt = pl.program_id(0) % tiles_per_batch
    row = bt * tm + first_row + lax.broadcasted_iota(jnp.int32, (nrows, 1), 0)
    return row < ctx_len


def _for_row_chunks(tm, fn):
    def body(r, _):
        fn(pl.ds(pl.multiple_of(r * ROW_CHUNK, ROW_CHUNK), ROW_CHUNK), r * ROW_CHUNK)
        return 0

    lax.fori_loop(0, tm // ROW_CHUNK, body, 0)


def _rms_modulate_into(h_ref, x_ref, g_ref, shb_ref, scb_ref, shc_ref, scc_ref, *, tm, tiles_per_batch, ctx_len):
    def chunk(rows, first_row):
        x = x_ref[rows, :]
        is_ctx = _is_ctx_rows(tm, tiles_per_batch, ctx_len, first_row, ROW_CHUNK)
        y = x * lax.rsqrt(jnp.mean(x * x, axis=-1, keepdims=True) + EPS) * g_ref[...]
        shift = jnp.where(is_ctx, shc_ref[...], shb_ref[...])
        scale = jnp.where(is_ctx, scc_ref[...], scb_ref[...])
        h_ref[rows, :] = (y * (1.0 + scale) + shift).astype(h_ref.dtype)

    _for_row_chunks(tm, chunk)


def _mod_specs(d, tiles_per_batch, n_batch, ncols=None):
    if ncols is None:
        return [pl.BlockSpec((None, 1, d), lambda i, j: (i // tiles_per_batch, 0, 0)),
                pl.BlockSpec((None, 1, d), lambda i, j: (n_batch, 0, 0))]
    return [pl.BlockSpec((None, 1, ncols), lambda i, j: (i // tiles_per_batch, 0, j)),
            pl.BlockSpec((None, 1, ncols), lambda i, j: (n_batch, 0, j))]


def _proj_kernel(x_ref, g_ref, shb_ref, shc_ref, scb_ref, scc_ref, w_ref, o_ref, h_ref, *, tm, tiles_per_batch, ctx_len):
    @pl.when(pl.program_id(1) == 0)
    def _():
        _rms_modulate_into(h_ref, x_ref, g_ref, shb_ref, scb_ref, shc_ref, scc_ref,
                           tm=tm, tiles_per_batch=tiles_per_batch, ctx_len=ctx_len)

    o_ref[...] = jnp.dot(h_ref[...], w_ref[...], preferred_element_type=F32)


def _proj_call(xs, g, shift, scale, w, *, tm, lb, ctx_len, n_batch):
    t, d = xs.shape
    n = w.shape[1]
    tn = _col_tile(n)
    tpb = lb // tm
    est = 2 * tm * d * 4 + tm * d * 2 + 2 * d * tn * 2 + 3 * tm * tn * 4 + 8 * ROW_CHUNK * d * 4 + (2 << 20)
    kern = functools.partial(_proj_kernel, tm=tm, tiles_per_batch=tpb, ctx_len=ctx_len)
    return pl.pallas_call(
        kern,
        grid=(t // tm, n // tn),
        in_specs=[pl.BlockSpec((tm, d), lambda i, j: (i, 0)),
                  pl.BlockSpec((1, d), lambda i, j: (0, 0)),
                  *_mod_specs(d, tpb, n_batch), *_mod_specs(d, tpb, n_batch),
                  pl.BlockSpec((d, tn), lambda i, j: (0, j))],
        out_specs=pl.BlockSpec((tm, tn), lambda i, j: (i, j)),
        out_shape=jax.ShapeDtypeStruct((t, n), F32),
        scratch_shapes=[pltpu.VMEM((tm, d), BF16)],
        compiler_params=_params(("parallel", "arbitrary"), est),
        name="norm_mod_proj",
    )(xs, g, shift, shift, scale, scale, w)


def _outproj_kernel(a_ref, w_ref, x_ref, gab_ref, gac_ref, o_ref, *, tm, tiles_per_batch, ctx_len):
    acc = jnp.dot(a_ref[...].astype(BF16), w_ref[...], preferred_element_type=F32)
    gate = jnp.where(_is_ctx_rows(tm, tiles_per_batch, ctx_len), gac_ref[...], gab_ref[...])
    o_ref[...] = x_ref[...] + gate * acc


def _outproj_call(a, w, xs, gate, *, tm, lb, ctx_len, n_batch):
    t, k = a.shape
    n = w.shape[1]
    tn = _col_tile(n)
    tpb = lb // tm
    est = 2 * tm * k * a.dtype.itemsize + tm * k * 2 + 2 * k * tn * 2 + 4 * tm * tn * 4 + 2 * tm * tn * 4 + (2 << 20)
    kern = functools.partial(_outproj_kernel, tm=tm, tiles_per_batch=tpb, ctx_len=ctx_len)
    return pl.pallas_call(
        kern,
        grid=(t // tm, n // tn),
        in_specs=[pl.BlockSpec((tm, k), lambda i, j: (i, 0)),
                  pl.BlockSpec((k, tn), lambda i, j: (0, j)),
                  pl.BlockSpec((tm, tn), lambda i, j: (i, j)),
                  *_mod_specs(n, tpb, n_batch, ncols=tn)],
        out_specs=pl.BlockSpec((tm, tn), lambda i, j: (i, j)),
        out_shape=jax.ShapeDtypeStruct((t, n), F32),
        compiler_params=_params(("parallel", "parallel"), est),
        name="outproj_residual",
    )(a, w, xs, gate, gate)


def _ffn_kernel(x_ref, g_ref, shb_ref, shc_ref, scb_ref, scc_ref, gab_ref, gac_ref, wg_ref, wu_ref, wo_ref,
                o_ref, h_ref, *, tm, tiles_per_batch, ctx_len):
    f = pl.program_id(1)

    @pl.when(f == 0)
    def _():
        _rms_modulate_into(h_ref, x_ref, g_ref, shb_ref, scb_ref, shc_ref, scc_ref,
                           tm=tm, tiles_per_batch=tiles_per_batch, ctx_len=ctx_len)
        o_ref[...] = jnp.zeros_like(o_ref)

    h = h_ref[...]
    gate = jnp.dot(h, wg_ref[...], preferred_element_type=F32)
    up = jnp.dot(h, wu_ref[...], preferred_element_type=F32)
    act = (gate * jax.nn.sigmoid(gate) * up).astype(BF16)
    tn = _col_tile(o_ref.shape[1])
    for c0 in range(0, o_ref.shape[1], tn):
        o_ref[:, c0:c0 + tn] += jnp.dot(act, wo_ref[:, c0:c0 + tn], preferred_element_type=F32)

    @pl.when(f == pl.num_programs(1) - 1)
    def _():
        def chunk(rows, first_row):
            is_ctx = _is_ctx_rows(tm, tiles_per_batch, ctx_len, first_row, ROW_CHUNK)
            res_gate = jnp.where(is_ctx, gac_ref[...], gab_ref[...])
            o_ref[rows, :] = x_ref[rows, :] + res_gate * o_ref[rows, :]

        _for_row_chunks(tm, chunk)


def _ffn_call(xs, g, shift, scale, res_gate, w_in, w_out, *, tm, lb, ctx_len, n_batch):
    t, d = xs.shape
    ff = w_out.shape[0]
    tf = _col_tile(ff)
    nf = ff // tf
    tpb = lb // tm
    est = (4 * tm * d * 4 + tm * d * 2 + 2 * (2 * d * tf * 2 + tf * d * 2) + 4 * tm * tf * 4 + tm * tf * 2
           + 8 * ROW_CHUNK * d * 4 + (2 << 20))
    kern = functools.partial(_ffn_kernel, tm=tm, tiles_per_batch=tpb, ctx_len=ctx_len)
    return pl.pallas_call(
        kern,
        grid=(t // tm, nf),
        in_specs=[pl.BlockSpec((tm, d), lambda i, f: (i, 0)),
                  pl.BlockSpec((1, d), lambda i, f: (0, 0)),
                  *_mod_specs(d, tpb, n_batch), *_mod_specs(d, tpb, n_batch), *_mod_specs(d, tpb, n_batch),
                  pl.BlockSpec((d, tf), lambda i, f: (0, f)),
                  pl.BlockSpec((d, tf), lambda i, f: (0, nf + f)),
                  pl.BlockSpec((tf, d), lambda i, f: (f, 0))],
        out_specs=pl.BlockSpec((tm, d), lambda i, f: (i, 0)),
        out_shape=jax.ShapeDtypeStruct((t, d), F32),
        scratch_shapes=[pltpu.VMEM((tm, d), BF16)],
        compiler_params=_params(("parallel", "arbitrary"), est),
        name="swiglu_ffn",
    )(xs, g, shift, shift, scale, scale, res_gate, res_gate, w_in, w_in, w_out)


def _final_kernel(x_ref, g_ref, o_ref):
    x = x_ref[...]
    o_ref[...] = x * lax.rsqrt(jnp.mean(x * x, axis=-1, keepdims=True) + EPS) * g_ref[...]


def _final_call(xs, g, *, n_batch, ctx_len, seq_len):
    d = xs.shape[1]
    tr = ctx_len
    per = (ctx_len + seq_len) // tr
    est = 4 * tr * d * 4 + 2 * tr * d * 4 + (2 << 20)
    return pl.pallas_call(
        _final_kernel,
        grid=(n_batch, seq_len // tr),
        in_specs=[pl.BlockSpec((tr, d), lambda b, k: (b * per + 1 + k, 0)),
                  pl.BlockSpec((1, d), lambda b, k: (0, 0))],
        out_specs=pl.BlockSpec((None, tr, d), lambda b, k: (b, k, 0)),
        out_shape=jax.ShapeDtypeStruct((n_batch, seq_len, d), F32),
        compiler_params=_params(("parallel", "parallel"), est),
        name="final_rmsnorm",
    )(xs, g)


def _softplus(x):
    return jnp.maximum(x, 0.0) + jnp.log1p(jnp.exp(-jnp.abs(x)))


def _chunk_carries(h_loc, p_loc, h_init, reverse):
    n = h_loc.shape[0]
    carries = [None] * n
    c = h_init
    for s in (reversed(range(n)) if reverse else range(n)):
        carries[s] = c
        c = p_loc[s:s + 1] * c + h_loc[s:s + 1]
    return carries, c


def _strided_scan(coef, res, n, stride):
    nsub = V7X_SUBLANES
    af_ref, bf_ref, ab_ref, bb_ref = coef
    pf_ref, hf_ref, pb_ref, hb_ref = res
    lanes = af_ref.shape[1]

    def body(i, carry):
        hf, pf, hb, pb = carry
        rows_f = pl.ds(i, nsub, stride=stride)
        rows_b = pl.ds(n - 1 - i, nsub, stride=stride)
        a = af_ref[rows_f, :]
        hf = a * hf + bf_ref[rows_f, :]
        pf = a * pf
        hf_ref[rows_f, :] = hf
        pf_ref[rows_f, :] = pf
        a = ab_ref[rows_b, :]
        hb = a * hb + bb_ref[rows_b, :]
        pb = a * pb
        hb_ref[rows_b, :] = hb
        pb_ref[rows_b, :] = pb
        return hf, pf, hb, pb

    zero = jnp.zeros((nsub, lanes), F32)
    one = jnp.ones((nsub, lanes), F32)
    return lax.fori_loop(0, n, body, (zero, one, zero, one), unroll=8)


def _rg_kernel(xb_ref, gb_ref, cw_ref, cb_ref, wg_ref, bg_ref, lam_ref, y_ref,
               xp_ref, laf, lbf, lab, lbb, lpf, lhf, lpb, lhb, caf, cbf, cab, cbb, cpf, chf, cpb, chb, carry_ref,
               *, ctx_len, seq_len):
    nsub = V7X_SUBLANES
    lanes = xb_ref.shape[1]
    rb = ctx_len
    nblk = 1 + seq_len // rb
    halo = nsub
    lat_stride = rb + nsub
    ctx_n = ctx_len // nsub
    ctx_stride = ctx_n + nsub

    xp_ref[pl.ds(0, halo), :] = jnp.zeros((halo, lanes), F32)
    xp_ref[pl.ds(halo + ctx_len + seq_len, halo), :] = jnp.zeros((halo, lanes), F32)
    xp_ref[pl.ds(halo, ctx_len + seq_len), :] = xb_ref[...]

    cw = cw_ref[...]
    cb = cb_ref[...]
    softplus_neg_lam = _softplus(-lam_ref[...])
    local_row = lax.broadcasted_iota(jnp.int32, (rb, 1), 0)

    def coeffs(blk):
        start = pl.multiple_of(blk * rb, rb)
        win = xp_ref[pl.ds(start, rb + 2 * halo), :]
        seg_first = blk <= 1
        seg_last = jnp.logical_or(blk == 0, blk == nblk - 1)
        taps = []
        for k in range(CONV_TAPS):
            off = k - CONV_LEFT
            tap = win[halo + off:halo + off + rb]
            if off < 0:
                tap = jnp.where(jnp.logical_or(local_row >= -off, jnp.logical_not(seg_first)), tap, 0.0)
            elif off > 0:
                tap = jnp.where(jnp.logical_or(local_row < rb - off, jnp.logical_not(seg_last)), tap, 0.0)
            taps.append(tap)
        xc = cb
        for k in range(CONV_TAPS):
            xc = xc + cw[k:k + 1] * taps[k]
        gates = jnp.dot(xc.astype(BF16), wg_ref[...], preferred_element_type=F32) + bg_ref[...]
        out = []
        for d in range(2):
            r = jax.nn.sigmoid(gates[:, (2 * d) * lanes:(2 * d + 1) * lanes])
            ig = jax.nn.sigmoid(gates[:, (2 * d + 1) * lanes:(2 * d + 2) * lanes])
            log_a = (-RG_C) * r * softplus_neg_lam[d:d + 1]
            a = jnp.exp(log_a)
            t = jnp.tanh(log_a)
            mult = jnp.sqrt((-2.0) * t / (1.0 - t))
            out.append((a, mult * ig * xc))
        return out

    (a_f, b_f), (a_b, b_b) = coeffs(0)
    for c in range(nsub):
        dst = pl.ds(c * ctx_stride, ctx_n)
        src = slice(c * ctx_n, (c + 1) * ctx_n)
        caf[dst, :] = a_f[src]
        cbf[dst, :] = b_f[src]
        cab[dst, :] = a_b[src]
        cbb[dst, :] = b_b[src]

    def fill(blk, _):
        (a_f, b_f), (a_b, b_b) = coeffs(blk)
        dst = pl.ds(pl.multiple_of((blk - 1) * lat_stride, nsub), rb)
        laf[dst, :] = a_f
        lbf[dst, :] = b_f
        lab[dst, :] = a_b
        lbb[dst, :] = b_b
        return 0

    lax.fori_loop(1, nblk, fill, 0)

    zero_state = jnp.zeros((1, lanes), F32)
    hf, pf, hb, pb = _strided_scan((caf, cbf, cab, cbb), (cpf, chf, cpb, chb), ctx_n, ctx_stride)
    ctx_in_f, ctx_state_f = _chunk_carries(hf, pf, zero_state, reverse=False)
    ctx_in_b, ctx_state_b = _chunk_carries(hb, pb, zero_state, reverse=True)
    hf, pf, hb, pb = _strided_scan((laf, lbf, lab, lbb), (lpf, lhf, lpb, lhb), rb, lat_stride)
    lat_in_f, _ = _chunk_carries(hf, pf, ctx_state_f, reverse=False)
    lat_in_b, _ = _chunk_carries(hb, pb, ctx_state_b, reverse=True)
    for s in range(nsub):
        carry_ref[pl.ds(s, 1), :] = lat_in_f[s]
        carry_ref[pl.ds(nsub + s, 1), :] = lat_in_b[s]

    for c in range(nsub):
        src = pl.ds(c * ctx_stride, ctx_n)
        h = chf[src, :] + cpf[src, :] * ctx_in_f[c] + chb[src, :] + cpb[src, :] * ctx_in_b[c]
        rows = pl.ds(c * ctx_n, ctx_n)
        y_ref[rows, :] = h * jax.nn.gelu(gb_ref[rows, :])

    def emit(blk, _):
        src = pl.ds(pl.multiple_of((blk - 1) * lat_stride, nsub), rb)
        cf = carry_ref[pl.ds(blk - 1, 1), :]
        cbw = carry_ref[pl.ds(nsub + blk - 1, 1), :]
        h = lhf[src, :] + lpf[src, :] * cf + lhb[src, :] + lpb[src, :] * cbw
        rows = pl.ds(pl.multiple_of(blk * rb, rb), rb)
        y_ref[rows, :] = h * jax.nn.gelu(gb_ref[rows, :])
        return 0

    lax.fori_loop(1, nblk, emit, 0)


def _rg_call(p, conv_w, conv_b, w_gates, b_gates, lam, *, n_batch, ctx_len, seq_len):
    t, w2 = p.shape
    w = w2 // 2
    lanes = V7X_LANES
    nh = w // lanes
    lb = ctx_len + seq_len
    nsub = V7X_SUBLANES
    lat_rows = nsub * (seq_len // nsub + nsub)
    ctx_rows = nsub * (ctx_len // nsub + nsub)
    blk_bytes = lb * lanes * 4
    est = 6 * blk_bytes + (lb + 2 * nsub) * lanes * 4 + 8 * (lat_rows + ctx_rows) * lanes * 4 + 24 * ctx_len * lanes * 4 + (4 << 20)
    kern = functools.partial(_rg_kernel, ctx_len=ctx_len, seq_len=seq_len)
    return pl.pallas_call(
        kern,
        grid=(n_batch, nh),
        in_specs=[pl.BlockSpec((lb, lanes), lambda b, h: (b, h)),
                  pl.BlockSpec((lb, lanes), lambda b, h: (b, nh + h)),
                  pl.BlockSpec((CONV_TAPS, lanes), lambda b, h: (0, h)),
                  pl.BlockSpec((1, lanes), lambda b, h: (0, h)),
                  pl.BlockSpec((None, lanes, 4 * lanes), lambda b, h: (h, 0, 0)),
                  pl.BlockSpec((None, 1, 4 * lanes), lambda b, h: (h, 0, 0)),
                  pl.BlockSpec((2, lanes), lambda b, h: (0, h))],
        out_specs=pl.BlockSpec((lb, lanes), lambda b, h: (b, h)),
        out_shape=jax.ShapeDtypeStruct((t, w), F32),
        scratch_shapes=[pltpu.VMEM((lb + 2 * nsub, lanes), F32)]
        + [pltpu.VMEM((lat_rows, lanes), F32)] * 8
        + [pltpu.VMEM((ctx_rows, lanes), F32)] * 8
        + [pltpu.VMEM((2 * nsub, lanes), F32)],
        compiler_params=_params(("parallel", "parallel"), est),
        name="rglru_mixer",
    )(p, p, conv_w, conv_b, w_gates, b_gates, lam)


def _group_cumsum(x, group, reverse):
    n = x.shape[0]
    pos = lax.broadcasted_iota(jnp.int32, (n, 1), 0) % group
    step = 1
    while step < group:
        if reverse:
            shifted = pltpu.roll(x, n - step, axis=0)
            keep = pos < group - step
        else:
            shifted = pltpu.roll(x, step, axis=0)
            keep = pos >= step
        x = x + jnp.where(keep, shifted, 0.0)
        step *= 2
    return x


def _hg_kernel(q_ref, ff_ref, fb_ref, v_ref, g_ref, lb_ref, nw_ref, y_ref,
               pitch, qs, fs, bs, vs, gs, qd_f, qd_b, oacc, kvt_f, kvt_b, dec_f, dec_b, st_f, st_b,
               k_slow, cum_slow, o_slow, *, ctx_len, seq_len, layer, dk, unroll):
    lanes = q_ref.shape[1]
    ch = HG_CHUNK
    lb_rows = ctx_len + seq_len
    nchunk = lb_rows // ch
    ctx_chunks = ctx_len // ch
    sb = ctx_len
    nsb = lb_rows // sb
    cps = sb // ch
    ncol = seq_len // ch
    grid_pitch = ncol + V7X_SUBLANES
    qd_refs, kvt_refs, dec_refs, st_refs = (qd_f, qd_b), (kvt_f, kvt_b), (dec_f, dec_b), (st_f, st_b)

    def to_scan_order(src_ref, dst_ref):
        rows = pl.ds(0, ctx_len)
        dst_ref[rows, :] = src_ref[rows, :]

        def spread(r, _):
            src = pl.ds(pl.multiple_of(ctx_len + r * ncol, ncol), ncol)
            pitch[pl.ds(pl.multiple_of(r * grid_pitch, V7X_SUBLANES), ncol), :] = src_ref[src, :]
            return 0

        lax.fori_loop(0, ch, spread, 0, unroll=8)

        def column(n, _):
            dst = pl.ds(pl.multiple_of(ctx_len + n * ch, ch), ch)
            dst_ref[dst, :] = pitch[pl.ds(n, ch, stride=grid_pitch), :]
            return 0

        lax.fori_loop(0, ncol, column, 0, unroll=8)

    for src_ref, dst_ref in ((q_ref, qs), (ff_ref, fs), (fb_ref, bs), (v_ref, vs), (g_ref, gs)):
        to_scan_order(src_ref, dst_ref)

    lbp = lb_ref[...]
    e = jnp.exp(lbp - jnp.max(lbp, axis=0, keepdims=True))
    sm = e / jnp.sum(e, axis=0, keepdims=True)
    lower = jnp.zeros(sm.shape[1:], F32)
    for l in range(1, layer + 1):
        lower = lower + sm[l]

    q_scale = dk ** -0.5
    j_idx = lax.broadcasted_iota(jnp.int32, (cps, ch, ch), 1)
    s_idx = lax.broadcasted_iota(jnp.int32, (cps, ch, ch), 2)

    def gate_feats(rows, d):
        lo = lower[d:d + 1]
        sg = jax.nn.sigmoid((bs if d else fs)[rows, :])
        log_f = jnp.log(lo + (1.0 - lo) * sg)
        k = (1.0 - lo) * (1.0 - sg)
        return k, _group_cumsum(log_f, ch, reverse=bool(d))

    def exact_intra(rows):
        q3 = (qs[rows, :] * q_scale).reshape(cps, ch, lanes)
        v3 = vs[rows, :].reshape(cps, ch, lanes)
        pos = lax.broadcasted_iota(jnp.int32, (1, ch, 1), 1)
        for d in range(2):
            k, cum = gate_feats(rows, d)
            k_slow[...] = k.reshape(cps, ch, lanes)
            cum_slow[...] = cum.reshape(cps, ch, lanes)

            def row_j(j, _):
                cum3 = cum_slow[...]
                w = jnp.exp(jnp.minimum(cum_slow[:, pl.ds(j, 1), :] - cum3, 0.0))
                valid = (pos >= j) if d else (pos <= j)
                q_j = jnp.where(pos == j, q3, 0.0).sum(axis=1, keepdims=True)
                col = jnp.sum(jnp.where(valid, q_j * k_slow[...] * w, 0.0), axis=-1, keepdims=True)
                o_j = jnp.sum(col * v3, axis=1, keepdims=True)
                if d:
                    o_slow[:, pl.ds(j, 1), :] += o_j
                else:
                    o_slow[:, pl.ds(j, 1), :] = o_j
                return 0

            lax.fori_loop(0, ch, row_j, 0)
        oacc[rows, :] = o_slow[...].reshape(sb, lanes)

    def local_part(i, _):
        rows = pl.ds(pl.multiple_of(i * sb, sb), sb)
        q = qs[rows, :] * q_scale
        v3 = vs[rows, :].reshape(cps, ch, lanes).astype(BF16)
        o_sum = None
        deepest = None
        for d in range(2):
            k, cum = gate_feats(rows, d)
            cum3 = cum.reshape(cps, ch, lanes)
            total3 = cum3[:, 0:1, :] if d else cum3[:, ch - 1:ch, :]
            qd = (q * jnp.exp(cum)).astype(BF16)
            kd = (k * jnp.exp(jnp.minimum(-cum, EXP_CLAMP))).astype(BF16)
            kl = (k.reshape(cps, ch, lanes) * jnp.exp(total3 - cum3)).astype(BF16)
            att = jnp.einsum("cjd,csd->cjs", qd.reshape(cps, ch, lanes), kd.reshape(cps, ch, lanes),
                             preferred_element_type=F32)
            att = jnp.where((j_idx <= s_idx) if d else (j_idx >= s_idx), att, 0.0)
            o_intra = jnp.einsum("cjs,cse->cje", att.astype(BF16), v3, preferred_element_type=F32)
            o_sum = o_intra if o_sum is None else o_sum + o_intra
            qd_refs[d][rows, :] = qd
            for c in range(cps):
                kvt = lax.dot_general(v3[c], kl[c], (((0,), (0,)), ((), ())), preferred_element_type=F32)
                kvt_refs[d][i * cps + c] = kvt
            dec_refs[d][pl.ds(i * cps, cps)] = jnp.exp(total3)
            low = jnp.min(cum, axis=0, keepdims=True)
            deepest = low if deepest is None else jnp.minimum(deepest, low)
        oacc[rows, :] = o_sum.reshape(sb, lanes)

        @pl.when(jnp.min(deepest) < -EXP_CLAMP)
        def _():
            exact_intra(rows)

        return 0

    lax.fori_loop(0, nsb, local_part, 0, unroll=unroll)

    def state_step(d, n, st):
        st_refs[d][n] = st.astype(BF16)
        return dec_refs[d][n] * st + kvt_refs[d][n]

    zero_state = jnp.zeros((lanes, lanes), F32)
    lax.fori_loop(0, nchunk, functools.partial(state_step, 0), zero_state, unroll=4)
    st = lax.fori_loop(0, ctx_chunks, lambda i, s: state_step(1, ctx_chunks - 1 - i, s), zero_state, unroll=4)
    lax.fori_loop(0, nchunk - ctx_chunks, lambda i, s: state_step(1, nchunk - 1 - i, s), st, unroll=4)

    nw = nw_ref[...]

    def inter_part(i, _):
        rows = pl.ds(pl.multiple_of(i * sb, sb), sb)
        o = oacc[rows, :]
        for d in range(2):
            qd3 = qd_refs[d][rows, :].reshape(cps, ch, lanes)
            st3 = st_refs[d][pl.ds(i * cps, cps)]
            o = o + jnp.einsum("cjd,ced->cje", qd3, st3, preferred_element_type=F32).reshape(sb, lanes)
        g = gs[rows, :]
        o = o * lax.rsqrt(jnp.mean(o * o, axis=-1, keepdims=True) + EPS) * nw * (g * jax.nn.sigmoid(g))
        oacc[rows, :] = o
        return 0

    lax.fori_loop(0, nsb, inter_part, 0, unroll=unroll)

    rows = pl.ds(0, ctx_len)
    y_ref[rows, :] = oacc[rows, :]

    def column(n, _):
        src = pl.ds(pl.multiple_of(ctx_len + n * ch, ch), ch)
        pitch[pl.ds(n, ch, stride=grid_pitch), :] = oacc[src, :]
        return 0

    lax.fori_loop(0, ncol, column, 0, unroll=8)

    def gather_row(r, _):
        dst = pl.ds(pl.multiple_of(ctx_len + r * ncol, ncol), ncol)
        y_ref[dst, :] = pitch[pl.ds(pl.multiple_of(r * grid_pitch, V7X_SUBLANES), ncol), :]
        return 0

    lax.fori_loop(0, ch, gather_row, 0, unroll=8)


def _hg_call(p, hg_lb, norm_w, *, n_batch, ctx_len, seq_len, layer):
    t, w5 = p.shape
    w = w5 // 5
    lanes = V7X_LANES
    nh = w // lanes
    lb = ctx_len + seq_len
    nchunk = lb // HG_CHUNK
    depth = hg_lb.shape[0]
    blk_bytes = lb * lanes * 4
    cps = ctx_len // HG_CHUNK
    pitch_rows = HG_CHUNK * (seq_len // HG_CHUNK + V7X_SUBLANES)
    unroll = 3 if (lb // ctx_len) % 3 == 0 else 1
    est = (12 * blk_bytes + 8 * blk_bytes + pitch_rows * lanes * 4 + 2 * nchunk * lanes * lanes * 6
           + 2 * nchunk * V7X_SUBLANES * lanes * 4 + unroll * 40 * ctx_len * lanes * 4 + (4 << 20))
    kern = functools.partial(_hg_kernel, ctx_len=ctx_len, seq_len=seq_len, layer=layer, dk=lanes, unroll=unroll)
    col = lambda k: pl.BlockSpec((lb, lanes), lambda b, h: (b, k * nh + h))
    return pl.pallas_call(
        kern,
        grid=(n_batch, nh),
        in_specs=[col(0), col(1), col(2), col(3), col(4),
                  pl.BlockSpec((depth, 2, lanes), lambda b, h: (0, 0, h)),
                  pl.BlockSpec((1, lanes), lambda b, h: (0, 0))],
        out_specs=pl.BlockSpec((lb, lanes), lambda b, h: (b, h)),
        out_shape=jax.ShapeDtypeStruct((t, w), F32),
        scratch_shapes=[pltpu.VMEM((pitch_rows, lanes), F32)]
        + [pltpu.VMEM((lb, lanes), F32)] * 5
        + [pltpu.VMEM((lb, lanes), BF16)] * 2
        + [pltpu.VMEM((lb, lanes), F32)]
        + [pltpu.VMEM((nchunk, lanes, lanes), F32)] * 2
        + [pltpu.VMEM((nchunk, 1, lanes), F32)] * 2
        + [pltpu.VMEM((nchunk, lanes, lanes), BF16)] * 2
        + [pltpu.VMEM((cps, HG_CHUNK, lanes), F32)] * 3,
        compiler_params=_params(("parallel", "parallel"), est),
        name="hgrn2_mixer",
    )(p, p, p, p, p, hg_lb, norm_w)


def kernel(x, c, ctx, c_ctx, w_ada, b_ada, g_mix, g_ffn, g_final, w_ffn_in, w_ffn_out,
           rg_w_in, rg_conv_w, rg_conv_b, rg_w_a, rg_b_a, rg_w_i, rg_b_i, rg_lam, rg_w_out,
           hg_w_in, hg_lb, hg_norm, hg_w_out):
    n_batch, seq_len, d = x.shape
    ctx_len = ctx.shape[1]
    depth = w_ada.shape[0]
    lanes = V7X_LANES
    lb = ctx_len + seq_len
    assert seq_len == V7X_SUBLANES * ctx_len, "row blocking assumes the latent is 8 context lengths long"
    assert seq_len // GRID_W == HG_CHUNK, "an HGRN2 chunk must be one latent grid column"
    assert d % lanes == 0 and rg_w_a.shape[-1] == lanes and hg_norm.shape[-1] == lanes

    xs = jnp.concatenate([ctx, x], axis=1).reshape(n_batch * lb, d)

    pad = (-(n_batch + 1)) % V7X_SUBLANES
    cvec = jnp.concatenate([c, c_ctx[None, :], jnp.zeros((pad, d), F32)], axis=0)
    mods = _ada_call(cvec, w_ada, b_ada).reshape(depth, cvec.shape[0], N_MOD, d)

    tm_proj = lb // 2
    tm_ffn = lb // 3
    tiles = dict(lb=lb, ctx_len=ctx_len, n_batch=n_batch)
    n_mixers = 2
    for i in range(depth):
        sh1, sc1, ga1, sh2, sc2, ga2 = [mods[i, :, k][:, None, :] for k in range(N_MOD)]
        j = i // n_mixers
        if i % n_mixers == 0:
            nh = rg_w_a.shape[2]
            p = _proj_call(xs, g_mix[i][None], sh1, sc1, rg_w_in[j].astype(BF16), tm=tm_proj, **tiles)
            w_gates = jnp.concatenate([rg_w_a[j, 0], rg_w_i[j, 0], rg_w_a[j, 1], rg_w_i[j, 1]], axis=-1).astype(BF16)
            b_gates = jnp.concatenate([rg_b_a[j, 0], rg_b_i[j, 0], rg_b_a[j, 1], rg_b_i[j, 1]], axis=-1)[:, None, :]
            y = _rg_call(p, rg_conv_w[j], rg_conv_b[j][None], w_gates, b_gates, rg_lam[j],
                         n_batch=n_batch, ctx_len=ctx_len, seq_len=seq_len)
            w_out = rg_w_out[j]
        else:
            p = _proj_call(xs, g_mix[i][None], sh1, sc1, hg_w_in[j].astype(BF16), tm=tm_proj, **tiles)
            y = _hg_call(p, hg_lb, hg_norm[j][None], n_batch=n_batch, ctx_len=ctx_len, seq_len=seq_len, layer=i)
            w_out = hg_w_out[j]
        xs = _outproj_call(y, w_out.astype(BF16), xs, ga1, tm=tm_proj, **tiles)
        xs = _ffn_call(xs, g_ffn[i][None], sh2, sc2, ga2, w_ffn_in[i].astype(BF16), w_ffn_out[i].astype(BF16),
                       tm=tm_ffn, **tiles)
    return _final_call(xs, g_final[None], n_batch=n_batch, ctx_len=ctx_len, seq_len=seq_len)
```

```python
import functools

import jax
import jax.numpy as jnp
from jax import lax
from jax.experimental import pallas as pl
from jax.experimental.pallas import tpu as pltpu

F32 = jnp.float32
BF16 = jnp.bfloat16

EPS = 1e-6
RG_C = 8.0
GRID_W = 64
N_MOD = 6
N_MIXERS = 2
CONV_TAPS = 4
CONV_LEFT = 2

V7X_LANES = 128
V7X_SUBLANES = 8
V7X_VMEM_LIMIT_CAP = 57 * 1024 * 1024

HG_CHUNK = 32
EXP_CLAMP = 80.0
ROW_CHUNK = 128
ROW_TILE = 1024


def _col_tile(n, preferred=512):
    t = preferred
    while n % t:
        t -= V7X_LANES
    return t


def _params(sem, nbytes):
    return pltpu.CompilerParams(dimension_semantics=sem, vmem_limit_bytes=int(min(V7X_VMEM_LIMIT_CAP, nbytes)))


class _Rows:
    def __init__(self, n_batch, ctx_len, seq_len, tm):
        assert (n_batch * ctx_len) % tm == 0 and seq_len % tm == 0
        self.n_batch, self.ctx_len, self.seq_len, self.tm = n_batch, ctx_len, seq_len, tm
        self.n_ctx_tiles = n_batch * ctx_len // tm
        self.tiles_per_seq = seq_len // tm
        self.n_lat_tiles = n_batch * self.tiles_per_seq
        self.n_tiles = self.n_ctx_tiles + self.n_lat_tiles

    def mod_row(self, tile):
        return jnp.where(tile < self.n_ctx_tiles, self.n_batch, (tile - self.n_ctx_tiles) // self.tiles_per_seq)


def _mod_spec(rows, first_tile, ncols, col_of_j):
    return pl.BlockSpec((None, 1, ncols), lambda i, j: (rows.mod_row(i + first_tile), 0, col_of_j(j)))


def _part_specs(parts, rows, first_tile, ncols, col_of_j):
    if len(parts) == 1:
        (_, off), = parts
        return [pl.BlockSpec((rows.tm, ncols), lambda i, j: (i + first_tile + off, col_of_j(j)))]
    assert first_tile == 0
    (_, off_c), (_, off_l) = parts
    nct = rows.n_ctx_tiles
    return [pl.BlockSpec((rows.tm, ncols), lambda i, j: (jnp.minimum(i, nct - 1) + off_c, col_of_j(j))),
            pl.BlockSpec((rows.tm, ncols), lambda i, j: (jnp.maximum(i - nct, 0) + off_l, col_of_j(j)))]


def _with_part(refs, n_ctx_tiles, fn):
    if len(refs) == 1:
        fn(refs[0])
        return
    is_ctx = pl.program_id(0) < n_ctx_tiles
    pl.when(is_ctx)(lambda: fn(refs[0]))
    pl.when(jnp.logical_not(is_ctx))(lambda: fn(refs[1]))


def _for_row_chunks(tm, fn):
    def body(r, _):
        fn(pl.ds(pl.multiple_of(r * ROW_CHUNK, ROW_CHUNK), ROW_CHUNK))
        return 0

    lax.fori_loop(0, tm // ROW_CHUNK, body, 0)


def _rms_modulate_into(h_ref, inv_ref, x_ref, g_ref, sh_ref, sc_ref, tm):
    def stats(rows):
        x = x_ref[rows, :]
        inv_ref[rows, :] = lax.rsqrt(jnp.mean(x * x, axis=-1, keepdims=True) + EPS)

    _for_row_chunks(tm, stats)
    gain = g_ref[...] * (1.0 + sc_ref[...])
    shift = sh_ref[...]

    def affine(rows):
        h_ref[rows, :] = (x_ref[rows, :] * inv_ref[rows, :] * gain + shift).astype(h_ref.dtype)

    _for_row_chunks(tm, affine)


def _ada_kernel(c_ref, w_ref, b_ref, o_ref):
    c = c_ref[...]
    s = (c * jax.nn.sigmoid(c)).astype(BF16)
    o_ref[...] = jnp.dot(s, w_ref[...].astype(BF16), preferred_element_type=F32) + b_ref[...]


def _ada_call(cvec, w_ada, b_ada):
    depth, d, n = w_ada.shape
    r = cvec.shape[0]
    tn = _col_tile(n)
    est = 2 * (d * tn * 4) + d * tn * 2 + 4 * r * (d + 2 * tn) * 4 + (4 << 20)
    return pl.pallas_call(
        _ada_kernel,
        grid=(depth, n // tn),
        in_specs=[
            pl.BlockSpec((r, d), lambda l, j: (0, 0)),
            pl.BlockSpec((None, d, tn), lambda l, j: (l, 0, j)),
            pl.BlockSpec((None, 1, tn), lambda l, j: (l, 0, j)),
        ],
        out_specs=pl.BlockSpec((None, r, tn), lambda l, j: (l, 0, j)),
        out_shape=jax.ShapeDtypeStruct((depth, r, n), F32),
        compiler_params=_params(("parallel", "parallel"), est),
        name="ada_mod",
    )(cvec, w_ada, b_ada.reshape(depth, 1, n))


def _proj_kernel(*refs, n_x, tm, n_ctx_tiles):
    x_refs = refs[:n_x]
    g_ref, sh_ref, sc_ref, w_ref, o_ref, h_ref, inv_ref = refs[n_x:]

    @pl.when(pl.program_id(1) == 0)
    def _():
        _with_part(x_refs, n_ctx_tiles, lambda x_ref: _rms_modulate_into(h_ref, inv_ref, x_ref, g_ref, sh_ref, sc_ref, tm))

    o_ref[...] = jnp.dot(h_ref[...], w_ref[...], preferred_element_type=F32)


def _proj_call(x_parts, g, shift, scale, w, layer, rows):
    d = x_parts[0][0].shape[1]
    n = w.shape[2]
    tm, tn = rows.tm, _col_tile(n)
    nx = len(x_parts)
    est = 2 * nx * tm * d * 4 + tm * d * 2 + 2 * d * tn * 2 + 3 * tm * tn * 4 + 8 * ROW_CHUNK * d * 4 + (3 << 20)
    kern = functools.partial(_proj_kernel, n_x=nx, tm=tm, n_ctx_tiles=rows.n_ctx_tiles)
    full = lambda j: 0
    return pl.pallas_call(
        kern,
        grid=(rows.n_tiles, n // tn),
        in_specs=[*_part_specs(x_parts, rows, 0, d, full),
                  pl.BlockSpec((1, d), lambda i, j: (0, 0)),
                  _mod_spec(rows, 0, d, full), _mod_spec(rows, 0, d, full),
                  pl.BlockSpec((None, d, tn), lambda i, j: (layer, 0, j))],
        out_specs=pl.BlockSpec((tm, tn), lambda i, j: (i, j)),
        out_shape=jax.ShapeDtypeStruct((rows.n_tiles * tm, n), F32),
        scratch_shapes=[pltpu.VMEM((tm, d), BF16), pltpu.VMEM((tm, 1), F32)],
        compiler_params=_params(("parallel", "arbitrary"), est),
        name="norm_mod_proj",
    )(*[a for a, _ in x_parts], g, shift, scale, w)


def _outproj_kernel(*refs, n_a, n_x, n_ctx_tiles):
    a_refs, x_refs = refs[:n_a], refs[n_a:n_a + n_x]
    w_ref, ga_ref, o_ref = refs[n_a + n_x:]

    def with_a(a_ref):
        acc = jnp.dot(a_ref[...], w_ref[...], preferred_element_type=F32) * ga_ref[...]

        def with_x(x_ref):
            o_ref[...] = x_ref[...] + acc

        _with_part(x_refs, n_ctx_tiles, with_x)

    _with_part(a_refs, n_ctx_tiles, with_a)


def _outproj_call(a_parts, w, layer, x_parts, gate, rows, first_tile):
    k, n = w.shape[1], w.shape[2]
    tm, tn = rows.tm, _col_tile(n)
    na, nx = len(a_parts), len(x_parts)
    n_tiles = rows.n_tiles - first_tile
    est = 2 * na * tm * k * 2 + 2 * k * tn * 2 + 2 * (nx + 1) * tm * tn * 4 + 3 * tm * tn * 4 + (3 << 20)
    kern = functools.partial(_outproj_kernel, n_a=na, n_x=nx, n_ctx_tiles=rows.n_ctx_tiles)
    return pl.pallas_call(
        kern,
        grid=(n_tiles, n // tn),
        in_specs=[*_part_specs(a_parts, rows, first_tile, k, lambda j: 0),
                  *_part_specs(x_parts, rows, first_tile, tn, lambda j: j),
                  pl.BlockSpec((None, k, tn), lambda i, j: (layer, 0, j)),
                  _mod_spec(rows, first_tile, tn, lambda j: j)],
        out_specs=pl.BlockSpec((tm, tn), lambda i, j: (i, j)),
        out_shape=jax.ShapeDtypeStruct((n_tiles * tm, n), F32),
        compiler_params=_params(("parallel", "parallel"), est),
        name="outproj_residual",
    )(*[a for a, _ in a_parts], *[a for a, _ in x_parts], w, gate)


def _ffn_kernel(*refs, tm, final_norm):
    if final_norm:
        x_ref, g_ref, sh_ref, sc_ref, ga_ref, wg_ref, wu_ref, wo_ref, gf_ref, o_ref, h_ref, inv_ref = refs
    else:
        x_ref, g_ref, sh_ref, sc_ref, ga_ref, wg_ref, wu_ref, wo_ref, o_ref, h_ref, inv_ref = refs
    f = pl.program_id(1)

    @pl.when(f == 0)
    def _():
        _rms_modulate_into(h_ref, inv_ref, x_ref, g_ref, sh_ref, sc_ref, tm)
        o_ref[...] = jnp.zeros_like(o_ref)

    tn = _col_tile(o_ref.shape[1])
    half = tm // 2
    for r0 in range(0, tm, half):
        h = h_ref[r0:r0 + half, :]
        gate = jnp.dot(h, wg_ref[...], preferred_element_type=F32)
        up = jnp.dot(h, wu_ref[...], preferred_element_type=F32)
        act = (gate * jax.nn.sigmoid(gate) * up).astype(BF16)
        for c0 in range(0, o_ref.shape[1], tn):
            o_ref[r0:r0 + half, c0:c0 + tn] += jnp.dot(act, wo_ref[:, c0:c0 + tn], preferred_element_type=F32)

    @pl.when(f == pl.num_programs(1) - 1)
    def _():
        res_gate = ga_ref[...]

        def chunk(rows):
            y = x_ref[rows, :] + res_gate * o_ref[rows, :]
            if final_norm:
                y = y * lax.rsqrt(jnp.mean(y * y, axis=-1, keepdims=True) + EPS) * gf_ref[...]
            o_ref[rows, :] = y

        _for_row_chunks(tm, chunk)


def _ffn_call(xs, x_first_tile, g, shift, scale, res_gate, w_in, w_out, layer, rows, first_tile, final_g=None):
    d = xs.shape[1]
    ff = w_out.shape[1]
    tm, tf = rows.tm, _col_tile(ff)
    nf = ff // tf
    n_tiles = rows.n_tiles - first_tile
    est = (4 * tm * d * 4 + tm * d * 2 + 2 * (2 * d * tf * 2 + tf * d * 2) + 2 * tm * tf * 4 + tm * tf * 2
           + 8 * ROW_CHUNK * d * 4 + (3 << 20))
    kern = functools.partial(_ffn_kernel, tm=tm, final_norm=final_g is not None)
    full = lambda f: 0
    extra_specs = [] if final_g is None else [pl.BlockSpec((1, d), lambda i, f: (0, 0))]
    extra_args = [] if final_g is None else [final_g]
    return pl.pallas_call(
        kern,
        grid=(n_tiles, nf),
        in_specs=[pl.BlockSpec((tm, d), lambda i, f: (i + x_first_tile, 0)),
                  pl.BlockSpec((1, d), lambda i, f: (0, 0)),
                  _mod_spec(rows, first_tile, d, full), _mod_spec(rows, first_tile, d, full),
                  _mod_spec(rows, first_tile, d, full),
                  pl.BlockSpec((None, d, tf), lambda i, f: (layer, 0, f)),
                  pl.BlockSpec((None, d, tf), lambda i, f: (layer, 0, nf + f)),
                  pl.BlockSpec((None, tf, d), lambda i, f: (layer, f, 0)),
                  *extra_specs],
        out_specs=pl.BlockSpec((tm, d), lambda i, f: (i, 0)),
        out_shape=jax.ShapeDtypeStruct((n_tiles * tm, d), F32),
        scratch_shapes=[pltpu.VMEM((tm, d), BF16), pltpu.VMEM((tm, 1), F32)],
        compiler_params=_params(("parallel", "arbitrary"), est),
        name="swiglu_ffn",
    )(xs, g, shift, scale, res_gate, w_in, w_in, w_out, *extra_args)


def _softplus(x):
    return jnp.maximum(x, 0.0) + jnp.log1p(jnp.exp(-jnp.abs(x)))


def _chunk_carries(h_loc, p_loc, h_init, reverse):
    n = h_loc.shape[0]
    carries = [None] * n
    c = h_init
    for s in (reversed(range(n)) if reverse else range(n)):
        carries[s] = c
        c = p_loc[s:s + 1] * c + h_loc[s:s + 1]
    return carries, c


def _strided_scan(coef, res, n, stride):
    nsub = V7X_SUBLANES
    af_ref, bf_ref, ab_ref, bb_ref = coef
    pf_ref, hf_ref, pb_ref, hb_ref = res
    lanes = af_ref.shape[1]

    def body(i, carry):
        hf, pf, hb, pb = carry
        rows_f = pl.ds(i, nsub, stride=stride)
        rows_b = pl.ds(n - 1 - i, nsub, stride=stride)
        a = af_ref[rows_f, :]
        hf = a * hf + bf_ref[rows_f, :]
        pf = a * pf
        hf_ref[rows_f, :] = hf
        pf_ref[rows_f, :] = pf
        a = ab_ref[rows_b, :]
        hb = a * hb + bb_ref[rows_b, :]
        pb = a * pb
        hb_ref[rows_b, :] = hb
        pb_ref[rows_b, :] = pb
        return hf, pf, hb, pb

    zero = jnp.zeros((nsub, lanes), F32)
    one = jnp.ones((nsub, lanes), F32)
    return lax.fori_loop(0, n, body, (zero, one, zero, one), unroll=8)


def _rg_kernel(*refs, ctx_len, seq_len, need_ctx):
    xc_ref, xl_ref, gc_ref, gl_ref, cw_ref, cb_ref, wg_ref, bg_ref, lam_ref = refs[:9]
    n_out = 2 if need_ctx else 1
    y_ref = refs[9]
    yc_ref = refs[10] if need_ctx else None
    (xp_ref, laf, lbf, lab, lbb, lpf, lhf, lpb, lhb, caf, cbf, cab, cbb, cpf, chf, cpb, chb, carry_ref) = refs[9 + n_out:]
    nsub = V7X_SUBLANES
    lanes = xc_ref.shape[1]
    rb = ctx_len
    nblk = 1 + seq_len // rb
    halo = nsub
    lat_stride = rb + nsub
    ctx_n = ctx_len // nsub
    ctx_stride = ctx_n + nsub

    xp_ref[pl.ds(0, halo), :] = jnp.zeros((halo, lanes), F32)
    xp_ref[pl.ds(halo + ctx_len + seq_len, halo), :] = jnp.zeros((halo, lanes), F32)
    xp_ref[pl.ds(halo, ctx_len), :] = xc_ref[...]
    xp_ref[pl.ds(halo + ctx_len, seq_len), :] = xl_ref[...]

    cw = cw_ref[...]
    cb = cb_ref[...]
    softplus_neg_lam = _softplus(-lam_ref[...])
    local_row = lax.broadcasted_iota(jnp.int32, (rb, 1), 0)

    def coeffs(blk):
        start = blk * rb if isinstance(blk, int) else pl.multiple_of(blk * rb, rb)
        win = xp_ref[pl.ds(start, rb + 2 * halo), :]
        seg_first = blk <= 1
        seg_last = jnp.logical_or(blk == 0, blk == nblk - 1)
        taps = []
        for k in range(CONV_TAPS):
            off = k - CONV_LEFT
            tap = win[halo + off:halo + off + rb]
            if off < 0:
                tap = jnp.where(jnp.logical_or(local_row >= -off, jnp.logical_not(seg_first)), tap, 0.0)
            elif off > 0:
                tap = jnp.where(jnp.logical_or(local_row < rb - off, jnp.logical_not(seg_last)), tap, 0.0)
            taps.append(tap)
        xc = cb
        for k in range(CONV_TAPS):
            xc = xc + cw[k:k + 1] * taps[k]
        gates = jnp.dot(xc.astype(BF16), wg_ref[...], preferred_element_type=F32) + bg_ref[...]
        out = []
        for d in range(2):
            r = jax.nn.sigmoid(gates[:, (2 * d) * lanes:(2 * d + 1) * lanes])
            ig = jax.nn.sigmoid(gates[:, (2 * d + 1) * lanes:(2 * d + 2) * lanes])
            log_a = (-RG_C) * r * softplus_neg_lam[d:d + 1]
            a = jnp.exp(log_a)
            t = jnp.tanh(log_a)
            mult = jnp.sqrt((-2.0) * t / (1.0 - t))
            out.append((a, mult * ig * xc))
        return out

    (a_f, b_f), (a_b, b_b) = coeffs(0)
    for c in range(nsub):
        dst = pl.ds(c * ctx_stride, ctx_n)
        src = slice(c * ctx_n, (c + 1) * ctx_n)
        caf[dst, :] = a_f[src]
        cbf[dst, :] = b_f[src]
        cab[dst, :] = a_b[src]
        cbb[dst, :] = b_b[src]

    def fill(blk, _):
        (a_f, b_f), (a_b, b_b) = coeffs(blk)
        dst = pl.ds(pl.multiple_of((blk - 1) * lat_stride, nsub), rb)
        laf[dst, :] = a_f
        lbf[dst, :] = b_f
        lab[dst, :] = a_b
        lbb[dst, :] = b_b
        return 0

    lax.fori_loop(1, nblk, fill, 0, unroll=2)

    zero_state = jnp.zeros((1, lanes), F32)
    hf, pf, hb, pb = _strided_scan((caf, cbf, cab, cbb), (cpf, chf, cpb, chb), ctx_n, ctx_stride)
    ctx_in_f, ctx_state_f = _chunk_carries(hf, pf, zero_state, reverse=False)
    ctx_in_b, ctx_state_b = _chunk_carries(hb, pb, zero_state, reverse=True)
    hf, pf, hb, pb = _strided_scan((laf, lbf, lab, lbb), (lpf, lhf, lpb, lhb), rb, lat_stride)
    lat_in_f, _ = _chunk_carries(hf, pf, ctx_state_f, reverse=False)
    lat_in_b, _ = _chunk_carries(hb, pb, ctx_state_b, reverse=True)
    for s in range(nsub):
        carry_ref[pl.ds(s, 1), :] = lat_in_f[s]
        carry_ref[pl.ds(nsub + s, 1), :] = lat_in_b[s]

    if need_ctx:
        for c in range(nsub):
            src = pl.ds(c * ctx_stride, ctx_n)
            h = chf[src, :] + cpf[src, :] * ctx_in_f[c] + chb[src, :] + cpb[src, :] * ctx_in_b[c]
            rows = pl.ds(c * ctx_n, ctx_n)
            yc_ref[rows, :] = (h * jax.nn.gelu(gc_ref[rows, :])).astype(yc_ref.dtype)

    def emit(blk, _):
        src = pl.ds(pl.multiple_of((blk - 1) * lat_stride, nsub), rb)
        cf = carry_ref[pl.ds(blk - 1, 1), :]
        cbw = carry_ref[pl.ds(nsub + blk - 1, 1), :]
        h = lhf[src, :] + lpf[src, :] * cf + lhb[src, :] + lpb[src, :] * cbw
        rows = pl.ds(pl.multiple_of((blk - 1) * rb, rb), rb)
        y_ref[rows, :] = (h * jax.nn.gelu(gl_ref[rows, :])).astype(y_ref.dtype)
        return 0

    lax.fori_loop(1, nblk, emit, 0, unroll=2)


def _mixer_out(t, w, n_batch, ctx_len, seq_len, lanes, need_ctx):
    lat_block0 = n_batch * ctx_len // seq_len
    specs = [pl.BlockSpec((seq_len, lanes), lambda b, h: (lat_block0 + b, h))]
    shapes = [jax.ShapeDtypeStruct((t, w), BF16)]
    if need_ctx:
        specs.append(pl.BlockSpec((ctx_len, lanes), lambda b, h: (b, h)))
        shapes.append(jax.ShapeDtypeStruct((n_batch * ctx_len, w), BF16))
    return specs, shapes


def _rg_call(p, conv_w, conv_b, w_gates, b_gates, lam, *, n_batch, ctx_len, seq_len, need_ctx):
    t, w2 = p.shape
    w = w2 // 2
    lanes = V7X_LANES
    nh = w // lanes
    nsub = V7X_SUBLANES
    lb = ctx_len + seq_len
    lat_block0 = n_batch * ctx_len // seq_len
    lat_rows = nsub * (seq_len // nsub + nsub)
    ctx_rows = nsub * (ctx_len // nsub + nsub)
    est = (4 * lb * lanes * 4 + 2 * lb * lanes * 2 + (lb + 2 * nsub) * lanes * 4 + 8 * (lat_rows + ctx_rows) * lanes * 4
           + 48 * ctx_len * lanes * 4 + (4 << 20))
    kern = functools.partial(_rg_kernel, ctx_len=ctx_len, seq_len=seq_len, need_ctx=need_ctx)
    out_specs, out_shapes = _mixer_out(t, w, n_batch, ctx_len, seq_len, lanes, need_ctx)
    outs = pl.pallas_call(
        kern,
        grid=(n_batch, nh),
        in_specs=[pl.BlockSpec((ctx_len, lanes), lambda b, h: (b, h)),
                  pl.BlockSpec((seq_len, lanes), lambda b, h: (lat_block0 + b, h)),
                  pl.BlockSpec((ctx_len, lanes), lambda b, h: (b, nh + h)),
                  pl.BlockSpec((seq_len, lanes), lambda b, h: (lat_block0 + b, nh + h)),
                  pl.BlockSpec((CONV_TAPS, lanes), lambda b, h: (0, h)),
                  pl.BlockSpec((1, lanes), lambda b, h: (0, h)),
                  pl.BlockSpec((None, lanes, 4 * lanes), lambda b, h: (h, 0, 0)),
                  pl.BlockSpec((None, 1, 4 * lanes), lambda b, h: (h, 0, 0)),
                  pl.BlockSpec((2, lanes), lambda b, h: (0, h))],
        out_specs=out_specs,
        out_shape=out_shapes,
        scratch_shapes=[pltpu.VMEM((lb + 2 * nsub, lanes), F32)]
        + [pltpu.VMEM((lat_rows, lanes), F32)] * 8
        + [pltpu.VMEM((ctx_rows, lanes), F32)] * 8
        + [pltpu.VMEM((2 * nsub, lanes), F32)],
        compiler_params=_params(("parallel", "parallel"), est),
        name="rglru_mixer",
    )(p, p, p, p, conv_w, conv_b, w_gates, b_gates, lam)
    return outs if need_ctx else (outs[0], None)


def _group_cumsum(x, group, reverse):
    n = x.shape[0]
    pos = lax.broadcasted_iota(jnp.int32, (n, 1), 0) % group
    step = 1
    while step < group:
        if reverse:
            shifted = pltpu.roll(x, n - step, axis=0)
            keep = pos < group - step
        else:
            shifted = pltpu.roll(x, step, axis=0)
            keep = pos >= step
        x = x + jnp.where(keep, shifted, 0.0)
        step *= 2
    return x


def _hg_kernel(*refs, ctx_len, seq_len, layer, dk, unroll, need_ctx):
    in_refs = refs[:10]
    lb_ref, nw_ref = refs[10:12]
    n_out = 2 if need_ctx else 1
    y_ref = refs[12]
    yc_ref = refs[13] if need_ctx else None
    (pitch, qs, fs, bs, vs, gs, qd_f, qd_b, oacc, kvt_f, kvt_b, dec_f, dec_b, st_f, st_b,
     k_slow, cum_slow, o_slow) = refs[12 + n_out:]
    lanes = qs.shape[1]
    ch = HG_CHUNK
    lb_rows = ctx_len + seq_len
    nchunk = lb_rows // ch
    ctx_chunks = ctx_len // ch
    sb = ctx_len
    nsb = lb_rows // sb
    cps = sb // ch
    ncol = seq_len // ch
    grid_pitch = ncol + V7X_SUBLANES
    qd_refs, kvt_refs, dec_refs, st_refs = (qd_f, qd_b), (kvt_f, kvt_b), (dec_f, dec_b), (st_f, st_b)

    def to_scan_order(ctx_src, lat_src, dst_ref):
        dst_ref[pl.ds(0, ctx_len), :] = ctx_src[...]

        def spread(r, _):
            src = pl.ds(pl.multiple_of(r * ncol, ncol), ncol)
            pitch[pl.ds(pl.multiple_of(r * grid_pitch, V7X_SUBLANES), ncol), :] = lat_src[src, :]
            return 0

        lax.fori_loop(0, ch, spread, 0, unroll=8)

        def column(n, _):
            dst = pl.ds(pl.multiple_of(ctx_len + n * ch, ch), ch)
            dst_ref[dst, :] = pitch[pl.ds(n, ch, stride=grid_pitch), :]
            return 0

        lax.fori_loop(0, ncol, column, 0, unroll=8)

    for k, dst_ref in enumerate((qs, fs, bs, vs, gs)):
        to_scan_order(in_refs[2 * k], in_refs[2 * k + 1], dst_ref)

    lbp = lb_ref[...]
    e = jnp.exp(lbp - jnp.max(lbp, axis=0, keepdims=True))
    sm = e / jnp.sum(e, axis=0, keepdims=True)
    lower = jnp.zeros(sm.shape[1:], F32)
    for l in range(1, layer + 1):
        lower = lower + sm[l]

    q_scale = dk ** -0.5
    j_idx = lax.broadcasted_iota(jnp.int32, (cps, ch, ch), 1)
    s_idx = lax.broadcasted_iota(jnp.int32, (cps, ch, ch), 2)

    def gate_feats(rows, d):
        lo = lower[d:d + 1]
        sg = jax.nn.sigmoid((bs if d else fs)[rows, :])
        log_f = jnp.log(lo + (1.0 - lo) * sg)
        k = (1.0 - lo) * (1.0 - sg)
        return k, _group_cumsum(log_f, ch, reverse=bool(d))

    def exact_intra(rows):
        q3 = (qs[rows, :] * q_scale).reshape(cps, ch, lanes)
        v3 = vs[rows, :].reshape(cps, ch, lanes)
        pos = lax.broadcasted_iota(jnp.int32, (1, ch, 1), 1)
        for d in range(2):
            k, cum = gate_feats(rows, d)
            k_slow[...] = k.reshape(cps, ch, lanes)
            cum_slow[...] = cum.reshape(cps, ch, lanes)

            def row_j(j, _):
                cum3 = cum_slow[...]
                w = jnp.exp(jnp.minimum(cum_slow[:, pl.ds(j, 1), :] - cum3, 0.0))
                valid = (pos >= j) if d else (pos <= j)
                q_j = jnp.where(pos == j, q3, 0.0).sum(axis=1, keepdims=True)
                col = jnp.sum(jnp.where(valid, q_j * k_slow[...] * w, 0.0), axis=-1, keepdims=True)
                o_j = jnp.sum(col * v3, axis=1, keepdims=True)
                if d:
                    o_slow[:, pl.ds(j, 1), :] += o_j
                else:
                    o_slow[:, pl.ds(j, 1), :] = o_j
                return 0

            lax.fori_loop(0, ch, row_j, 0)
        oacc[rows, :] = o_slow[...].reshape(sb, lanes)

    def local_part(i, _):
        rows = pl.ds(pl.multiple_of(i * sb, sb), sb)
        q = qs[rows, :] * q_scale
        v3 = vs[rows, :].reshape(cps, ch, lanes).astype(BF16)
        o_sum = None
        deepest = None
        for d in range(2):
            k, cum = gate_feats(rows, d)
            cum3 = cum.reshape(cps, ch, lanes)
            total3 = cum3[:, 0:1, :] if d else cum3[:, ch - 1:ch, :]
            qd = (q * jnp.exp(cum)).astype(BF16)
            kd = (k * jnp.exp(jnp.minimum(-cum, EXP_CLAMP))).astype(BF16)
            kl = (k.reshape(cps, ch, lanes) * jnp.exp(total3 - cum3)).astype(BF16)
            att = jnp.einsum("cjd,csd->cjs", qd.reshape(cps, ch, lanes), kd.reshape(cps, ch, lanes),
                             preferred_element_type=F32)
            att = jnp.where((j_idx <= s_idx) if d else (j_idx >= s_idx), att, 0.0)
            o_intra = jnp.einsum("cjs,cse->cje", att.astype(BF16), v3, preferred_element_type=F32)
            o_sum = o_intra if o_sum is None else o_sum + o_intra
            qd_refs[d][rows, :] = qd
            for c in range(cps):
                kvt = lax.dot_general(v3[c], kl[c], (((0,), (0,)), ((), ())), preferred_element_type=F32)
                kvt_refs[d][i * cps + c] = kvt
            dec_refs[d][pl.ds(i * cps, cps)] = jnp.exp(total3)
            low = jnp.min(cum, axis=0, keepdims=True)
            deepest = low if deepest is None else jnp.minimum(deepest, low)
        oacc[rows, :] = o_sum.reshape(sb, lanes)

        @pl.when(jnp.min(deepest) < -EXP_CLAMP)
        def _():
            exact_intra(rows)

        return 0

    lax.fori_loop(0, nsb, local_part, 0, unroll=unroll)

    def state_step(d, n, st):
        st_refs[d][n] = st.astype(BF16)
        return dec_refs[d][n] * st + kvt_refs[d][n]

    zero_state = jnp.zeros((lanes, lanes), F32)
    lax.fori_loop(0, nchunk, functools.partial(state_step, 0), zero_state, unroll=4)
    st = lax.fori_loop(0, ctx_chunks, lambda i, s: state_step(1, ctx_chunks - 1 - i, s), zero_state, unroll=4)
    lax.fori_loop(0, nchunk - ctx_chunks, lambda i, s: state_step(1, nchunk - 1 - i, s), st, unroll=4)

    nw = nw_ref[...]

    def inter_part(i, _):
        rows = pl.ds(pl.multiple_of(i * sb, sb), sb)
        o = oacc[rows, :]
        for d in range(2):
            qd3 = qd_refs[d][rows, :].reshape(cps, ch, lanes)
            st3 = st_refs[d][pl.ds(i * cps, cps)]
            o = o + jnp.einsum("cjd,ced->cje", qd3, st3, preferred_element_type=F32).reshape(sb, lanes)
        g = gs[rows, :]
        o = o * lax.rsqrt(jnp.mean(o * o, axis=-1, keepdims=True) + EPS) * nw * (g * jax.nn.sigmoid(g))
        oacc[rows, :] = o
        return 0

    first_sb = 0 if need_ctx else ctx_len // sb
    lax.fori_loop(first_sb, nsb, inter_part, 0, unroll=unroll if first_sb == 0 else 2)

    if need_ctx:
        yc_ref[...] = oacc[pl.ds(0, ctx_len), :].astype(yc_ref.dtype)

    def column(n, _):
        src = pl.ds(pl.multiple_of(ctx_len + n * ch, ch), ch)
        pitch[pl.ds(n, ch, stride=grid_pitch), :] = oacc[src, :]
        return 0

    lax.fori_loop(0, ncol, column, 0, unroll=8)

    def gather_row(r, _):
        dst = pl.ds(pl.multiple_of(r * ncol, ncol), ncol)
        y_ref[dst, :] = pitch[pl.ds(pl.multiple_of(r * grid_pitch, V7X_SUBLANES), ncol), :].astype(y_ref.dtype)
        return 0

    lax.fori_loop(0, ch, gather_row, 0, unroll=8)


def _hg_call(p, hg_lb, norm_w, *, n_batch, ctx_len, seq_len, layer, need_ctx):
    t, w5 = p.shape
    w = w5 // 5
    lanes = V7X_LANES
    nh = w // lanes
    lb = ctx_len + seq_len
    lat_block0 = n_batch * ctx_len // seq_len
    nchunk = lb // HG_CHUNK
    depth = hg_lb.shape[0]
    blk_bytes = lb * lanes * 4
    cps = ctx_len // HG_CHUNK
    pitch_rows = HG_CHUNK * (seq_len // HG_CHUNK + V7X_SUBLANES)
    unroll = 3 if (lb // ctx_len) % 3 == 0 else 1
    est = (10 * blk_bytes + 2 * lb * lanes * 2 + 8 * blk_bytes + pitch_rows * lanes * 4 + 2 * nchunk * lanes * lanes * 6
           + 2 * nchunk * V7X_SUBLANES * lanes * 4 + unroll * 40 * ctx_len * lanes * 4 + (4 << 20))
    kern = functools.partial(_hg_kernel, ctx_len=ctx_len, seq_len=seq_len, layer=layer, dk=lanes, unroll=unroll,
                             need_ctx=need_ctx)
    in_specs = []
    for k in range(5):
        in_specs.append(pl.BlockSpec((ctx_len, lanes), lambda b, h, k=k: (b, k * nh + h)))
        in_specs.append(pl.BlockSpec((seq_len, lanes), lambda b, h, k=k: (lat_block0 + b, k * nh + h)))
    out_specs, out_shapes = _mixer_out(t, w, n_batch, ctx_len, seq_len, lanes, need_ctx)
    outs = pl.pallas_call(
        kern,
        grid=(n_batch, nh),
        in_specs=in_specs + [pl.BlockSpec((depth, 2, lanes), lambda b, h: (0, 0, h)),
                             pl.BlockSpec((1, lanes), lambda b, h: (0, 0))],
        out_specs=out_specs,
        out_shape=out_shapes,
        scratch_shapes=[pltpu.VMEM((pitch_rows, lanes), F32)]
        + [pltpu.VMEM((lb, lanes), F32)] * 5
        + [pltpu.VMEM((lb, lanes), BF16)] * 2
        + [pltpu.VMEM((lb, lanes), F32)]
        + [pltpu.VMEM((nchunk, lanes, lanes), F32)] * 2
        + [pltpu.VMEM((nchunk, 1, lanes), F32)] * 2
        + [pltpu.VMEM((nchunk, lanes, lanes), BF16)] * 2
        + [pltpu.VMEM((cps, HG_CHUNK, lanes), F32)] * 3,
        compiler_params=_params(("parallel", "parallel"), est),
        name="hgrn2_mixer",
    )(*([p] * 10), hg_lb, norm_w)
    return outs if need_ctx else (outs[0], None)


def kernel(x, c, ctx, c_ctx, w_ada, b_ada, g_mix, g_ffn, g_final, w_ffn_in, w_ffn_out,
           rg_w_in, rg_conv_w, rg_conv_b, rg_w_a, rg_b_a, rg_w_i, rg_b_i, rg_lam, rg_w_out,
           hg_w_in, hg_lb, hg_norm, hg_w_out):
    n_batch, seq_len, d = x.shape
    ctx_len = ctx.shape[1]
    depth = w_ada.shape[0]
    lanes = V7X_LANES
    assert seq_len == V7X_SUBLANES * ctx_len, "row blocking assumes the latent is 8 context lengths long"
    assert seq_len // GRID_W == HG_CHUNK, "an HGRN2 chunk must be one latent grid column"
    assert (n_batch * ctx_len) % seq_len == 0, "latent blocks must stay block-aligned behind the context rows"
    assert d % lanes == 0 and rg_w_a.shape[-1] == lanes and hg_norm.shape[-1] == lanes
    dims = dict(n_batch=n_batch, ctx_len=ctx_len, seq_len=seq_len)
    rows = _Rows(n_batch, ctx_len, seq_len, ROW_TILE)
    nct = rows.n_ctx_tiles

    pad = (-(n_batch + 1)) % V7X_SUBLANES
    cvec = jnp.concatenate([c, c_ctx[None, :], jnp.zeros((pad, d), F32)], axis=0)
    mods = _ada_call(cvec, w_ada, b_ada).reshape(depth, cvec.shape[0], N_MOD, d)

    w_ffn_in_b, w_ffn_out_b = w_ffn_in.astype(BF16), w_ffn_out.astype(BF16)
    rg_w_in_b, rg_w_out_b = rg_w_in.astype(BF16), rg_w_out.astype(BF16)
    hg_w_in_b, hg_w_out_b = hg_w_in.astype(BF16), hg_w_out.astype(BF16)

    x_parts = [(ctx.reshape(n_batch * ctx_len, d), 0), (x.reshape(n_batch * seq_len, d), 0)]
    out = None
    for i in range(depth):
        last = i == depth - 1
        sh1, sc1, ga1, sh2, sc2, ga2 = [mods[i, :, k][:, None, :] for k in range(N_MOD)]
        j = i // N_MIXERS
        if i % N_MIXERS == 0:
            p = _proj_call(x_parts, g_mix[i][None], sh1, sc1, rg_w_in_b, j, rows)
            w_gates = jnp.concatenate([rg_w_a[j, 0], rg_w_i[j, 0], rg_w_a[j, 1], rg_w_i[j, 1]], axis=-1).astype(BF16)
            b_gates = jnp.concatenate([rg_b_a[j, 0], rg_b_i[j, 0], rg_b_a[j, 1], rg_b_i[j, 1]], axis=-1)[:, None, :]
            y, y_ctx = _rg_call(p, rg_conv_w[j], rg_conv_b[j][None], w_gates, b_gates, rg_lam[j], need_ctx=not last, **dims)
            w_out = rg_w_out_b
        else:
            p = _proj_call(x_parts, g_mix[i][None], sh1, sc1, hg_w_in_b, j, rows)
            y, y_ctx = _hg_call(p, hg_lb, hg_norm[j][None], layer=i, need_ctx=not last, **dims)
            w_out = hg_w_out_b
        if last:
            xs = _outproj_latent(y, w_out, j, x_parts, ga1, rows)
            out = _ffn_call(xs, 0, g_ffn[i][None], sh2, sc2, ga2, w_ffn_in_b, w_ffn_out_b, i, rows, nct,
                            final_g=g_final[None])
        else:
            xs = _outproj_call([(y_ctx, 0), (y, nct)], w_out, j, x_parts, ga1, rows, 0)
            xs = _ffn_call(xs, 0, g_ffn[i][None], sh2, sc2, ga2, w_ffn_in_b, w_ffn_out_b, i, rows, 0)
            x_parts = [(xs, 0)]
    return out.reshape(n_batch, seq_len, d)


def _outproj_latent(y, w_out, j, x_parts, gate, rows):
    nct = rows.n_ctx_tiles
    if len(x_parts) == 2:
        x_lat = [(x_parts[1][0], -nct)]
    else:
        x_lat = [(x_parts[0][0], 0)]
    return _outproj_call([(y, 0)], w_out, j, x_lat, gate, rows, nct)
```

```python
import functools

import jax
import jax.numpy as jnp
from jax import lax
from jax.experimental import pallas as pl
from jax.experimental.pallas import tpu as pltpu

F32 = jnp.float32
BF16 = jnp.bfloat16

EPS = 1e-6
RG_C = 8.0
GRID_W = 64
N_MOD = 6
N_MIXERS = 2
CONV_TAPS = 4
CONV_LEFT = 2

V7X_LANES = 128
V7X_SUBLANES = 8
V7X_VMEM_LIMIT_CAP = 57 * 1024 * 1024

HG_CHUNK = 32
EXP_CLAMP = 80.0
LOG2_E = 1.4426950408889634
ROW_CHUNK = 128
ROW_TILE = 1024


def _col_tile(n, preferred=512):
    t = preferred
    while n % t:
        t -= V7X_LANES
    return t


def _params(sem, nbytes):
    return pltpu.CompilerParams(dimension_semantics=sem, vmem_limit_bytes=int(min(V7X_VMEM_LIMIT_CAP, nbytes)))


class _Rows:
    def __init__(self, n_batch, ctx_len, seq_len, tm):
        assert (n_batch * ctx_len) % tm == 0 and seq_len % tm == 0
        self.n_batch, self.ctx_len, self.seq_len, self.tm = n_batch, ctx_len, seq_len, tm
        self.n_ctx_tiles = n_batch * ctx_len // tm
        self.tiles_per_seq = seq_len // tm
        self.n_lat_tiles = n_batch * self.tiles_per_seq
        self.n_tiles = self.n_ctx_tiles + self.n_lat_tiles

    def mod_row(self, tile):
        return jnp.where(tile < self.n_ctx_tiles, self.n_batch, (tile - self.n_ctx_tiles) // self.tiles_per_seq)


def _mod_spec(rows, first_tile, ncols, col_of_j):
    return pl.BlockSpec((None, 1, ncols), lambda i, j: (rows.mod_row(i + first_tile), 0, col_of_j(j)))


def _part_specs(parts, rows, first_tile, ncols, col_of_j):
    if len(parts) == 1:
        (_, off), = parts
        return [pl.BlockSpec((rows.tm, ncols), lambda i, j: (i + first_tile + off, col_of_j(j)))]
    assert first_tile == 0
    (_, off_c), (_, off_l) = parts
    nct = rows.n_ctx_tiles
    return [pl.BlockSpec((rows.tm, ncols), lambda i, j: (jnp.minimum(i, nct - 1) + off_c, col_of_j(j))),
            pl.BlockSpec((rows.tm, ncols), lambda i, j: (jnp.maximum(i - nct, 0) + off_l, col_of_j(j)))]


def _with_part(refs, n_ctx_tiles, fn):
    if len(refs) == 1:
        fn(refs[0])
        return
    is_ctx = pl.program_id(0) < n_ctx_tiles
    pl.when(is_ctx)(lambda: fn(refs[0]))
    pl.when(jnp.logical_not(is_ctx))(lambda: fn(refs[1]))


def _for_row_chunks(tm, fn):
    def body(r, _):
        fn(pl.ds(pl.multiple_of(r * ROW_CHUNK, ROW_CHUNK), ROW_CHUNK))
        return 0

    lax.fori_loop(0, tm // ROW_CHUNK, body, 0)


def _rms_modulate_into(h_ref, inv_ref, x_ref, g_ref, sh_ref, sc_ref, tm):
    def stats(rows):
        x = x_ref[rows, :]
        inv_ref[rows, :] = lax.rsqrt(jnp.mean(x * x, axis=-1, keepdims=True) + EPS)

    _for_row_chunks(tm, stats)
    gain = g_ref[...] * (1.0 + sc_ref[...])
    shift = sh_ref[...]

    def affine(rows):
        h_ref[rows, :] = (x_ref[rows, :] * inv_ref[rows, :] * gain + shift).astype(h_ref.dtype)

    _for_row_chunks(tm, affine)


def _ada_kernel(c_ref, w_ref, b_ref, o_ref):
    c = c_ref[...]
    s = (c * jax.nn.sigmoid(c)).astype(BF16)
    o_ref[...] = jnp.dot(s, w_ref[...].astype(BF16), preferred_element_type=F32) + b_ref[...]


def _ada_call(cvec, w_ada, b_ada):
    depth, d, n = w_ada.shape
    r = cvec.shape[0]
    tn = _col_tile(n)
    est = 2 * (d * tn * 4) + d * tn * 2 + 4 * r * (d + 2 * tn) * 4 + (4 << 20)
    return pl.pallas_call(
        _ada_kernel,
        grid=(depth, n // tn),
        in_specs=[
            pl.BlockSpec((r, d), lambda l, j: (0, 0)),
            pl.BlockSpec((None, d, tn), lambda l, j: (l, 0, j)),
            pl.BlockSpec((None, 1, tn), lambda l, j: (l, 0, j)),
        ],
        out_specs=pl.BlockSpec((None, r, tn), lambda l, j: (l, 0, j)),
        out_shape=jax.ShapeDtypeStruct((depth, r, n), F32),
        compiler_params=_params(("parallel", "parallel"), est),
        name="ada_mod",
    )(cvec, w_ada, b_ada.reshape(depth, 1, n))


def _proj_kernel(*refs, n_x, tm, n_ctx_tiles):
    x_refs = refs[:n_x]
    g_ref, sh_ref, sc_ref, w_ref, o_ref, h_ref, inv_ref = refs[n_x:]

    @pl.when(pl.program_id(1) == 0)
    def _():
        _with_part(x_refs, n_ctx_tiles, lambda x_ref: _rms_modulate_into(h_ref, inv_ref, x_ref, g_ref, sh_ref, sc_ref, tm))

    o_ref[...] = jnp.dot(h_ref[...], w_ref[...], preferred_element_type=F32)


def _proj_call(x_parts, g, shift, scale, w, layer, rows):
    d = x_parts[0][0].shape[1]
    n = w.shape[2]
    nx = len(x_parts)
    tm, tn = rows.tm, _col_tile(n, 1024 if nx == 1 else 512)
    est = 2 * nx * tm * d * 4 + tm * d * 2 + 2 * d * tn * 2 + 3 * tm * tn * 4 + 8 * ROW_CHUNK * d * 4 + (3 << 20)
    kern = functools.partial(_proj_kernel, n_x=nx, tm=tm, n_ctx_tiles=rows.n_ctx_tiles)
    full = lambda j: 0
    return pl.pallas_call(
        kern,
        grid=(rows.n_tiles, n // tn),
        in_specs=[*_part_specs(x_parts, rows, 0, d, full),
                  pl.BlockSpec((1, d), lambda i, j: (0, 0)),
                  _mod_spec(rows, 0, d, full), _mod_spec(rows, 0, d, full),
                  pl.BlockSpec((None, d, tn), lambda i, j: (layer, 0, j))],
        out_specs=pl.BlockSpec((tm, tn), lambda i, j: (i, j)),
        out_shape=jax.ShapeDtypeStruct((rows.n_tiles * tm, n), F32),
        scratch_shapes=[pltpu.VMEM((tm, d), BF16), pltpu.VMEM((tm, 1), F32)],
        compiler_params=_params(("parallel", "arbitrary"), est),
        name="norm_mod_proj",
    )(*[a for a, _ in x_parts], g, shift, scale, w)


def _outproj_kernel(*refs, n_a, n_x, n_ctx_tiles):
    a_refs, x_refs = refs[:n_a], refs[n_a:n_a + n_x]
    w_ref, ga_ref, o_ref = refs[n_a + n_x:]

    def with_a(a_ref):
        acc = jnp.dot(a_ref[...], w_ref[...], preferred_element_type=F32) * ga_ref[...]

        def with_x(x_ref):
            o_ref[...] = x_ref[...] + acc

        _with_part(x_refs, n_ctx_tiles, with_x)

    _with_part(a_refs, n_ctx_tiles, with_a)


def _outproj_call(a_parts, w, layer, x_parts, gate, rows, first_tile):
    k, n = w.shape[1], w.shape[2]
    tm, tn = rows.tm, _col_tile(n)
    na, nx = len(a_parts), len(x_parts)
    n_tiles = rows.n_tiles - first_tile
    est = 2 * na * tm * k * 2 + 2 * k * tn * 2 + 2 * (nx + 1) * tm * tn * 4 + 3 * tm * tn * 4 + (3 << 20)
    kern = functools.partial(_outproj_kernel, n_a=na, n_x=nx, n_ctx_tiles=rows.n_ctx_tiles)
    return pl.pallas_call(
        kern,
        grid=(n_tiles, n // tn),
        in_specs=[*_part_specs(a_parts, rows, first_tile, k, lambda j: 0),
                  *_part_specs(x_parts, rows, first_tile, tn, lambda j: j),
                  pl.BlockSpec((None, k, tn), lambda i, j: (layer, 0, j)),
                  _mod_spec(rows, first_tile, tn, lambda j: j)],
        out_specs=pl.BlockSpec((tm, tn), lambda i, j: (i, j)),
        out_shape=jax.ShapeDtypeStruct((n_tiles * tm, n), F32),
        compiler_params=_params(("parallel", "parallel"), est),
        name="outproj_residual",
    )(*[a for a, _ in a_parts], *[a for a, _ in x_parts], w, gate)


def _ffn_kernel(*refs, tm, final_norm):
    if final_norm:
        x_ref, g_ref, sh_ref, sc_ref, ga_ref, wg_ref, wu_ref, wo_ref, gf_ref, o_ref, h_ref, inv_ref = refs
    else:
        x_ref, g_ref, sh_ref, sc_ref, ga_ref, wg_ref, wu_ref, wo_ref, o_ref, h_ref, inv_ref = refs
    f = pl.program_id(1)

    @pl.when(f == 0)
    def _():
        _rms_modulate_into(h_ref, inv_ref, x_ref, g_ref, sh_ref, sc_ref, tm)
        o_ref[...] = jnp.zeros_like(o_ref)

    tn = _col_tile(o_ref.shape[1])
    half = tm // 2
    for r0 in range(0, tm, half):
        h = h_ref[r0:r0 + half, :]
        gate = jnp.dot(h, wg_ref[...], preferred_element_type=F32)
        up = jnp.dot(h, wu_ref[...], preferred_element_type=F32)
        act = (gate * jax.nn.sigmoid(gate) * up).astype(BF16)
        for c0 in range(0, o_ref.shape[1], tn):
            o_ref[r0:r0 + half, c0:c0 + tn] += jnp.dot(act, wo_ref[:, c0:c0 + tn], preferred_element_type=F32)

    @pl.when(f == pl.num_programs(1) - 1)
    def _():
        res_gate = ga_ref[...]

        def chunk(rows):
            y = x_ref[rows, :] + res_gate * o_ref[rows, :]
            if final_norm:
                y = y * lax.rsqrt(jnp.mean(y * y, axis=-1, keepdims=True) + EPS) * gf_ref[...]
            o_ref[rows, :] = y

        _for_row_chunks(tm, chunk)


def _ffn_call(xs, x_first_tile, g, shift, scale, res_gate, w_in, w_out, layer, rows, first_tile, final_g=None):
    d = xs.shape[1]
    ff = w_out.shape[1]
    tm, tf = rows.tm, _col_tile(ff)
    nf = ff // tf
    n_tiles = rows.n_tiles - first_tile
    est = (4 * tm * d * 4 + tm * d * 2 + 2 * (2 * d * tf * 2 + tf * d * 2) + 2 * tm * tf * 4 + tm * tf * 2
           + 8 * ROW_CHUNK * d * 4 + (3 << 20))
    kern = functools.partial(_ffn_kernel, tm=tm, final_norm=final_g is not None)
    full = lambda f: 0
    extra_specs = [] if final_g is None else [pl.BlockSpec((1, d), lambda i, f: (0, 0))]
    extra_args = [] if final_g is None else [final_g]
    return pl.pallas_call(
        kern,
        grid=(n_tiles, nf),
        in_specs=[pl.BlockSpec((tm, d), lambda i, f: (i + x_first_tile, 0)),
                  pl.BlockSpec((1, d), lambda i, f: (0, 0)),
                  _mod_spec(rows, first_tile, d, full), _mod_spec(rows, first_tile, d, full),
                  _mod_spec(rows, first_tile, d, full),
                  pl.BlockSpec((None, d, tf), lambda i, f: (layer, 0, f)),
                  pl.BlockSpec((None, d, tf), lambda i, f: (layer, 0, nf + f)),
                  pl.BlockSpec((None, tf, d), lambda i, f: (layer, f, 0)),
                  *extra_specs],
        out_specs=pl.BlockSpec((tm, d), lambda i, f: (i, 0)),
        out_shape=jax.ShapeDtypeStruct((n_tiles * tm, d), F32),
        scratch_shapes=[pltpu.VMEM((tm, d), BF16), pltpu.VMEM((tm, 1), F32)],
        compiler_params=_params(("parallel", "arbitrary"), est),
        name="swiglu_ffn",
    )(xs, g, shift, scale, res_gate, w_in, w_in, w_out, *extra_args)


def _softplus(x):
    return jnp.maximum(x, 0.0) + jnp.log1p(jnp.exp(-jnp.abs(x)))


def _chunk_carries(h_loc, p_loc, h_init, reverse):
    n = h_loc.shape[0]
    carries = [None] * n
    c = h_init
    for s in (reversed(range(n)) if reverse else range(n)):
        carries[s] = c
        c = p_loc[s:s + 1] * c + h_loc[s:s + 1]
    return carries, c


def _strided_scan(coef, res, n, stride):
    nsub = V7X_SUBLANES
    af_ref, bf_ref, ab_ref, bb_ref = coef
    pf_ref, hf_ref, pb_ref, hb_ref = res
    lanes = af_ref.shape[1]

    def body(i, carry):
        hf, pf, hb, pb = carry
        rows_f = pl.ds(i, nsub, stride=stride)
        rows_b = pl.ds(n - 1 - i, nsub, stride=stride)
        a = af_ref[rows_f, :]
        hf = a * hf + bf_ref[rows_f, :]
        pf = a * pf
        hf_ref[rows_f, :] = hf
        pf_ref[rows_f, :] = pf
        a = ab_ref[rows_b, :]
        hb = a * hb + bb_ref[rows_b, :]
        pb = a * pb
        hb_ref[rows_b, :] = hb
        pb_ref[rows_b, :] = pb
        return hf, pf, hb, pb

    zero = jnp.zeros((nsub, lanes), F32)
    one = jnp.ones((nsub, lanes), F32)
    return lax.fori_loop(0, n, body, (zero, one, zero, one), unroll=8)


def _rg_kernel(*refs, ctx_len, seq_len, need_ctx):
    xc_ref, xl_ref, gc_ref, gl_ref, cw_ref, cb_ref, wg_ref, bg_ref, lam_ref = refs[:9]
    n_out = 2 if need_ctx else 1
    y_ref = refs[9]
    yc_ref = refs[10] if need_ctx else None
    (xp_ref, laf, lbf, lab, lbb, lpf, lhf, lpb, lhb, caf, cbf, cab, cbb, cpf, chf, cpb, chb, carry_ref) = refs[9 + n_out:]
    nsub = V7X_SUBLANES
    lanes = xc_ref.shape[1]
    rb = ctx_len
    nblk = 1 + seq_len // rb
    halo = nsub
    lat_stride = rb + nsub
    ctx_n = ctx_len // nsub
    ctx_stride = ctx_n + nsub

    xp_ref[pl.ds(0, halo), :] = jnp.zeros((halo, lanes), F32)
    xp_ref[pl.ds(halo + ctx_len + seq_len, halo), :] = jnp.zeros((halo, lanes), F32)
    xp_ref[pl.ds(halo, ctx_len), :] = xc_ref[...]
    xp_ref[pl.ds(halo + ctx_len, seq_len), :] = xl_ref[...]

    cw = cw_ref[...]
    cb = cb_ref[...]
    softplus_neg_lam = _softplus(-lam_ref[...])
    local_row = lax.broadcasted_iota(jnp.int32, (rb, 1), 0)

    def coeffs(blk):
        start = blk * rb if isinstance(blk, int) else pl.multiple_of(blk * rb, rb)
        win = xp_ref[pl.ds(start, rb + 2 * halo), :]
        seg_first = blk <= 1
        seg_last = jnp.logical_or(blk == 0, blk == nblk - 1)
        taps = []
        for k in range(CONV_TAPS):
            off = k - CONV_LEFT
            tap = win[halo + off:halo + off + rb]
            if off < 0:
                tap = jnp.where(jnp.logical_or(local_row >= -off, jnp.logical_not(seg_first)), tap, 0.0)
            elif off > 0:
                tap = jnp.where(jnp.logical_or(local_row < rb - off, jnp.logical_not(seg_last)), tap, 0.0)
            taps.append(tap)
        xc = cb
        for k in range(CONV_TAPS):
            xc = xc + cw[k:k + 1] * taps[k]
        half_z = jnp.dot(xc.astype(BF16), wg_ref[...], preferred_element_type=F32) + bg_ref[...]
        half_xc = 0.5 * xc
        out = []
        for d in range(2):
            th_r = jnp.tanh(half_z[:, (2 * d) * lanes:(2 * d + 1) * lanes])
            th_i = jnp.tanh(half_z[:, (2 * d + 1) * lanes:(2 * d + 2) * lanes])
            scale = (-0.5 * RG_C) * softplus_neg_lam[d:d + 1]
            log_a = scale * th_r + scale
            a = jnp.exp(log_a)
            t = jnp.tanh(log_a)
            u = (-2.0) * t / (1.0 - t)
            mult = jnp.where(u > 0.0, u * lax.rsqrt(u), 0.0)
            out.append((a, mult * half_xc * (1.0 + th_i)))
        return out

    (a_f, b_f), (a_b, b_b) = coeffs(0)
    for c in range(nsub):
        dst = pl.ds(c * ctx_stride, ctx_n)
        src = slice(c * ctx_n, (c + 1) * ctx_n)
        caf[dst, :] = a_f[src]
        cbf[dst, :] = b_f[src]
        cab[dst, :] = a_b[src]
        cbb[dst, :] = b_b[src]

    def fill(blk, _):
        (a_f, b_f), (a_b, b_b) = coeffs(blk)
        dst = pl.ds(pl.multiple_of((blk - 1) * lat_stride, nsub), rb)
        laf[dst, :] = a_f
        lbf[dst, :] = b_f
        lab[dst, :] = a_b
        lbb[dst, :] = b_b
        return 0

    lax.fori_loop(1, nblk, fill, 0, unroll=2)

    zero_state = jnp.zeros((1, lanes), F32)
    hf, pf, hb, pb = _strided_scan((caf, cbf, cab, cbb), (cpf, chf, cpb, chb), ctx_n, ctx_stride)
    ctx_in_f, ctx_state_f = _chunk_carries(hf, pf, zero_state, reverse=False)
    ctx_in_b, ctx_state_b = _chunk_carries(hb, pb, zero_state, reverse=True)
    hf, pf, hb, pb = _strided_scan((laf, lbf, lab, lbb), (lpf, lhf, lpb, lhb), rb, lat_stride)
    lat_in_f, _ = _chunk_carries(hf, pf, ctx_state_f, reverse=False)
    lat_in_b, _ = _chunk_carries(hb, pb, ctx_state_b, reverse=True)
    for s in range(nsub):
        carry_ref[pl.ds(s, 1), :] = lat_in_f[s]
        carry_ref[pl.ds(nsub + s, 1), :] = lat_in_b[s]

    if need_ctx:
        for c in range(nsub):
            src = pl.ds(c * ctx_stride, ctx_n)
            h = chf[src, :] + cpf[src, :] * ctx_in_f[c] + chb[src, :] + cpb[src, :] * ctx_in_b[c]
            rows = pl.ds(c * ctx_n, ctx_n)
            yc_ref[rows, :] = (h * jax.nn.gelu(gc_ref[rows, :])).astype(yc_ref.dtype)

    def emit(blk, _):
        src = pl.ds(pl.multiple_of((blk - 1) * lat_stride, nsub), rb)
        cf = carry_ref[pl.ds(blk - 1, 1), :]
        cbw = carry_ref[pl.ds(nsub + blk - 1, 1), :]
        h = lhf[src, :] + lpf[src, :] * cf + lhb[src, :] + lpb[src, :] * cbw
        rows = pl.ds(pl.multiple_of((blk - 1) * rb, rb), rb)
        y_ref[rows, :] = (h * jax.nn.gelu(gl_ref[rows, :])).astype(y_ref.dtype)
        return 0

    lax.fori_loop(1, nblk, emit, 0, unroll=2)


def _mixer_out(t, w, n_batch, ctx_len, seq_len, lanes, need_ctx):
    lat_block0 = n_batch * ctx_len // seq_len
    specs = [pl.BlockSpec((seq_len, lanes), lambda b, h: (lat_block0 + b, h))]
    shapes = [jax.ShapeDtypeStruct((t, w), BF16)]
    if need_ctx:
        specs.append(pl.BlockSpec((ctx_len, lanes), lambda b, h: (b, h)))
        shapes.append(jax.ShapeDtypeStruct((n_batch * ctx_len, w), BF16))
    return specs, shapes


def _rg_call(p, conv_w, conv_b, w_gates, b_gates, lam, *, n_batch, ctx_len, seq_len, need_ctx):
    t, w2 = p.shape
    w = w2 // 2
    lanes = V7X_LANES
    nh = w // lanes
    nsub = V7X_SUBLANES
    lb = ctx_len + seq_len
    lat_block0 = n_batch * ctx_len // seq_len
    lat_rows = nsub * (seq_len // nsub + nsub)
    ctx_rows = nsub * (ctx_len // nsub + nsub)
    est = (4 * lb * lanes * 4 + 2 * lb * lanes * 2 + (lb + 2 * nsub) * lanes * 4 + 8 * (lat_rows + ctx_rows) * lanes * 4
           + 48 * ctx_len * lanes * 4 + (4 << 20))
    kern = functools.partial(_rg_kernel, ctx_len=ctx_len, seq_len=seq_len, need_ctx=need_ctx)
    out_specs, out_shapes = _mixer_out(t, w, n_batch, ctx_len, seq_len, lanes, need_ctx)
    outs = pl.pallas_call(
        kern,
        grid=(n_batch, nh),
        in_specs=[pl.BlockSpec((ctx_len, lanes), lambda b, h: (b, h)),
                  pl.BlockSpec((seq_len, lanes), lambda b, h: (lat_block0 + b, h)),
                  pl.BlockSpec((ctx_len, lanes), lambda b, h: (b, nh + h)),
                  pl.BlockSpec((seq_len, lanes), lambda b, h: (lat_block0 + b, nh + h)),
                  pl.BlockSpec((CONV_TAPS, lanes), lambda b, h: (0, h)),
                  pl.BlockSpec((1, lanes), lambda b, h: (0, h)),
                  pl.BlockSpec((None, lanes, 4 * lanes), lambda b, h: (h, 0, 0)),
                  pl.BlockSpec((None, 1, 4 * lanes), lambda b, h: (h, 0, 0)),
                  pl.BlockSpec((2, lanes), lambda b, h: (0, h))],
        out_specs=out_specs,
        out_shape=out_shapes,
        scratch_shapes=[pltpu.VMEM((lb + 2 * nsub, lanes), F32)]
        + [pltpu.VMEM((lat_rows, lanes), F32)] * 8
        + [pltpu.VMEM((ctx_rows, lanes), F32)] * 8
        + [pltpu.VMEM((2 * nsub, lanes), F32)],
        compiler_params=_params(("parallel", "parallel"), est),
        name="rglru_mixer",
    )(p, p, p, p, conv_w, conv_b, w_gates, b_gates, lam)
    return outs if need_ctx else (outs[0], None)


def _group_cumsum(x, group, reverse):
    n = x.shape[0]
    pos = lax.broadcasted_iota(jnp.int32, (n, 1), 0) % group
    step = 1
    while step < group:
        if reverse:
            shifted = pltpu.roll(x, n - step, axis=0)
            keep = pos < group - step
        else:
            shifted = pltpu.roll(x, step, axis=0)
            keep = pos >= step
        x = x + jnp.where(keep, shifted, 0.0)
        step *= 2
    return x


def _hg_kernel(*refs, ctx_len, seq_len, layer, dk, unroll, need_ctx):
    in_refs = refs[:10]
    lb_ref, nw_ref = refs[10:12]
    n_out = 2 if need_ctx else 1
    y_ref = refs[12]
    yc_ref = refs[13] if need_ctx else None
    (pitch, qs, fs, bs, vs, gs, qd_f, qd_b, oacc, kv_f, kv_b, st_f, st_b,
     k_slow, cum_slow, o_slow) = refs[12 + n_out:]
    lanes = qs.shape[1]
    ch = HG_CHUNK
    lb_rows = ctx_len + seq_len
    nchunk = lb_rows // ch
    ctx_chunks = ctx_len // ch
    sb = ctx_len
    nsb = lb_rows // sb
    cps = sb // ch
    ncol = seq_len // ch
    grid_pitch = ncol + V7X_SUBLANES
    qd_refs, kv_refs, st_refs = (qd_f, qd_b), (kv_f, kv_b), (st_f, st_b)

    def to_scan_order(ctx_src, lat_src, dst_ref):
        dst_ref[pl.ds(0, ctx_len), :] = ctx_src[...]

        def spread(r, _):
            src = pl.ds(pl.multiple_of(r * ncol, ncol), ncol)
            pitch[pl.ds(pl.multiple_of(r * grid_pitch, V7X_SUBLANES), ncol), :] = lat_src[src, :]
            return 0

        lax.fori_loop(0, ch, spread, 0, unroll=8)

        def column(n, _):
            dst = pl.ds(pl.multiple_of(ctx_len + n * ch, ch), ch)
            dst_ref[dst, :] = pitch[pl.ds(n, ch, stride=grid_pitch), :]
            return 0

        lax.fori_loop(0, ncol, column, 0, unroll=8)

    for k, dst_ref in enumerate((qs, fs, bs, vs, gs)):
        to_scan_order(in_refs[2 * k], in_refs[2 * k + 1], dst_ref)

    lbp = lb_ref[...]
    e = jnp.exp(lbp - jnp.max(lbp, axis=0, keepdims=True))
    sm = e / jnp.sum(e, axis=0, keepdims=True)
    lower = jnp.zeros(sm.shape[1:], F32)
    for l in range(1, layer + 1):
        lower = lower + sm[l]

    q_scale = dk ** -0.5
    j_idx = lax.broadcasted_iota(jnp.int32, (cps, ch, ch), 1)
    s_idx = lax.broadcasted_iota(jnp.int32, (cps, ch, ch), 2)
    causal = (j_idx >= s_idx, j_idx <= s_idx)
    aux = 2 * V7X_SUBLANES
    aux_row = lax.broadcasted_iota(jnp.int32, (1, aux, 1), 1)

    def bf16_pieces(x):
        hi = x.astype(BF16)
        rest = x - hi.astype(F32)
        mid = rest.astype(BF16)
        lo = (rest - mid.astype(F32)).astype(BF16)
        return hi, mid, lo

    def gate_feats(rows, d):
        lo = lower[d:d + 1]
        half_open = 0.5 * (1.0 - lo)
        th = jnp.tanh((bs if d else fs)[rows, :])
        log_f = jnp.log(lo + half_open * (1.0 + th))
        k = half_open * (1.0 - th)
        return k, _group_cumsum(log_f, ch, reverse=bool(d))

    def exact_intra(rows):
        q3 = (qs[rows, :] * q_scale).reshape(cps, ch, lanes)
        v3 = vs[rows, :].reshape(cps, ch, lanes)
        pos = lax.broadcasted_iota(jnp.int32, (1, ch, 1), 1)
        for d in range(2):
            k, cum = gate_feats(rows, d)
            k_slow[...] = k.reshape(cps, ch, lanes)
            cum_slow[...] = cum.reshape(cps, ch, lanes)

            def row_j(j, _):
                cum3 = cum_slow[...]
                w = jnp.exp(jnp.minimum(cum_slow[:, pl.ds(j, 1), :] - cum3, 0.0))
                valid = (pos >= j) if d else (pos <= j)
                q_j = jnp.where(pos == j, q3, 0.0).sum(axis=1, keepdims=True)
                col = jnp.sum(jnp.where(valid, q_j * k_slow[...] * w, 0.0), axis=-1, keepdims=True)
                o_j = jnp.sum(col * v3, axis=1, keepdims=True)
                if d:
                    o_slow[:, pl.ds(j, 1), :] += o_j
                else:
                    o_slow[:, pl.ds(j, 1), :] = o_j
                return 0

            lax.fori_loop(0, ch, row_j, 0)
        oacc[rows, :] = o_slow[...].reshape(sb, lanes)

    def local_part(i, _):
        rows = pl.ds(pl.multiple_of(i * sb, sb), sb)
        q = qs[rows, :] * q_scale
        v3 = vs[rows, :].reshape(cps, ch, lanes).astype(BF16)
        zeros_v = jnp.zeros((cps, ch, lanes), BF16)
        pad_rows = jnp.concatenate([jnp.zeros((cps, aux, lanes), BF16), jnp.ones((cps, aux, lanes), BF16)], axis=2)
        kv_rhs = jnp.concatenate([jnp.concatenate([v3, zeros_v], axis=2), pad_rows], axis=1)
        o_sum = None
        deepest = None
        for d in range(2):
            k, cum = gate_feats(rows, d)
            cum_l = cum * LOG2_E
            cum_l3 = cum_l.reshape(cps, ch, lanes)
            total_l3 = cum_l3[:, 0:1, :] if d else cum_l3[:, ch - 1:ch, :]
            qd = (q * jnp.exp2(cum_l)).astype(BF16)
            kd = (k * jnp.exp2(jnp.minimum(cum * (-LOG2_E), EXP_CLAMP * LOG2_E))).astype(BF16)
            kl = (k.reshape(cps, ch, lanes) * jnp.exp2(total_l3 - cum_l3)).astype(BF16)
            att = jnp.einsum("cjd,csd->cjs", qd.reshape(cps, ch, lanes), kd.reshape(cps, ch, lanes),
                             preferred_element_type=F32)
            att = jnp.where(causal[d], att, 0.0)
            o_intra = jnp.einsum("cjs,cse->cje", att.astype(BF16), v3, preferred_element_type=F32)
            o_sum = o_intra if o_sum is None else o_sum + o_intra
            qd_refs[d][rows, :] = qd
            d_hi, d_mid, d_lo = [p.astype(F32) for p in bf16_pieces(jnp.exp2(total_l3))]
            dec_rows = jnp.where(aux_row == 0, d_hi, jnp.where(aux_row == 1, d_mid, jnp.where(aux_row == 2, d_lo, 0.0)))
            kv_lhs = jnp.concatenate([kl, dec_rows.astype(BF16)], axis=1)
            for c in range(cps):
                kv_refs[d][i * cps + c] = lax.dot_general(kv_lhs[c], kv_rhs[c], (((0,), (0,)), ((), ())),
                                                          preferred_element_type=F32)
            low = jnp.min(total_l3, axis=0)
            deepest = low if deepest is None else jnp.minimum(deepest, low)
        oacc[rows, :] = o_sum.reshape(sb, lanes)

        @pl.when(jnp.min(deepest) < -EXP_CLAMP * LOG2_E)
        def _():
            exact_intra(rows)

        return 0

    lax.fori_loop(0, nsb, local_part, 0, unroll=unroll)

    def state_step(d, n, st):
        st_refs[d][n] = st.astype(BF16)
        kv = kv_refs[d][n]
        return kv[:, lanes:] * st + kv[:, :lanes]

    zero_state = jnp.zeros((lanes, lanes), F32)
    lax.fori_loop(0, nchunk, functools.partial(state_step, 0), zero_state, unroll=4)
    st = lax.fori_loop(0, ctx_chunks, lambda i, s: state_step(1, ctx_chunks - 1 - i, s), zero_state, unroll=4)
    lax.fori_loop(0, nchunk - ctx_chunks, lambda i, s: state_step(1, nchunk - 1 - i, s), st, unroll=4)

    nw = nw_ref[...]

    def inter_part(i, _):
        rows = pl.ds(pl.multiple_of(i * sb, sb), sb)
        o = oacc[rows, :]
        for d in range(2):
            qd3 = qd_refs[d][rows, :].reshape(cps, ch, lanes)
            st3 = st_refs[d][pl.ds(i * cps, cps)]
            o = o + jnp.einsum("cjd,cde->cje", qd3, st3, preferred_element_type=F32).reshape(sb, lanes)
        g = gs[rows, :]
        o = o * lax.rsqrt(jnp.mean(o * o, axis=-1, keepdims=True) + EPS) * nw * (g * jax.nn.sigmoid(g))
        oacc[rows, :] = o
        return 0

    first_sb = 0 if need_ctx else ctx_len // sb
    lax.fori_loop(first_sb, nsb, inter_part, 0, unroll=unroll if first_sb == 0 else 4)

    if need_ctx:
        yc_ref[...] = oacc[pl.ds(0, ctx_len), :].astype(yc_ref.dtype)

    def column(n, _):
        src = pl.ds(pl.multiple_of(ctx_len + n * ch, ch), ch)
        pitch[pl.ds(n, ch, stride=grid_pitch), :] = oacc[src, :]
        return 0

    lax.fori_loop(0, ncol, column, 0, unroll=8)

    def gather_row(r, _):
        dst = pl.ds(pl.multiple_of(r * ncol, ncol), ncol)
        y_ref[dst, :] = pitch[pl.ds(pl.multiple_of(r * grid_pitch, V7X_SUBLANES), ncol), :].astype(y_ref.dtype)
        return 0

    lax.fori_loop(0, ch, gather_row, 0, unroll=8)


def _hg_call(p, hg_lb, norm_w, *, n_batch, ctx_len, seq_len, layer, need_ctx):
    t, w5 = p.shape
    w = w5 // 5
    lanes = V7X_LANES
    nh = w // lanes
    lb = ctx_len + seq_len
    lat_block0 = n_batch * ctx_len // seq_len
    nchunk = lb // HG_CHUNK
    depth = hg_lb.shape[0]
    blk_bytes = lb * lanes * 4
    cps = ctx_len // HG_CHUNK
    pitch_rows = HG_CHUNK * (seq_len // HG_CHUNK + V7X_SUBLANES)
    unroll = 3 if (lb // ctx_len) % 3 == 0 else 1
    est = (10 * blk_bytes + 2 * lb * lanes * 2 + 8 * blk_bytes + pitch_rows * lanes * 4 + 2 * nchunk * lanes * lanes * 10
           + unroll * 40 * ctx_len * lanes * 4 + (4 << 20))
    kern = functools.partial(_hg_kernel, ctx_len=ctx_len, seq_len=seq_len, layer=layer, dk=lanes, unroll=unroll,
                             need_ctx=need_ctx)
    in_specs = []
    for k in range(5):
        in_specs.append(pl.BlockSpec((ctx_len, lanes), lambda b, h, k=k: (b, k * nh + h)))
        in_specs.append(pl.BlockSpec((seq_len, lanes), lambda b, h, k=k: (lat_block0 + b, k * nh + h)))
    out_specs, out_shapes = _mixer_out(t, w, n_batch, ctx_len, seq_len, lanes, need_ctx)
    outs = pl.pallas_call(
        kern,
        grid=(n_batch, nh),
        in_specs=in_specs + [pl.BlockSpec((depth, 2, lanes), lambda b, h: (0, 0, h)),
                             pl.BlockSpec((1, lanes), lambda b, h: (0, 0))],
        out_specs=out_specs,
        out_shape=out_shapes,
        scratch_shapes=[pltpu.VMEM((pitch_rows, lanes), F32)]
        + [pltpu.VMEM((lb, lanes), F32)] * 5
        + [pltpu.VMEM((lb, lanes), BF16)] * 2
        + [pltpu.VMEM((lb, lanes), F32)]
        + [pltpu.VMEM((nchunk, lanes, 2 * lanes), F32)] * 2
        + [pltpu.VMEM((nchunk, lanes, lanes), BF16)] * 2
        + [pltpu.VMEM((cps, HG_CHUNK, lanes), F32)] * 3,
        compiler_params=_params(("parallel", "parallel"), est),
        name="hgrn2_mixer",
    )(*([p] * 10), hg_lb, norm_w)
    return outs if need_ctx else (outs[0], None)


def kernel(x, c, ctx, c_ctx, w_ada, b_ada, g_mix, g_ffn, g_final, w_ffn_in, w_ffn_out,
           rg_w_in, rg_conv_w, rg_conv_b, rg_w_a, rg_b_a, rg_w_i, rg_b_i, rg_lam, rg_w_out,
           hg_w_in, hg_lb, hg_norm, hg_w_out):
    n_batch, seq_len, d = x.shape
    ctx_len = ctx.shape[1]
    depth = w_ada.shape[0]
    lanes = V7X_LANES
    assert seq_len == V7X_SUBLANES * ctx_len, "row blocking assumes the latent is 8 context lengths long"
    assert seq_len // GRID_W == HG_CHUNK, "an HGRN2 chunk must be one latent grid column"
    assert (n_batch * ctx_len) % seq_len == 0, "latent blocks must stay block-aligned behind the context rows"
    assert d % lanes == 0 and rg_w_a.shape[-1] == lanes and hg_norm.shape[-1] == lanes
    dims = dict(n_batch=n_batch, ctx_len=ctx_len, seq_len=seq_len)
    rows = _Rows(n_batch, ctx_len, seq_len, ROW_TILE)
    nct = rows.n_ctx_tiles

    pad = (-(n_batch + 1)) % V7X_SUBLANES
    cvec = jnp.concatenate([c, c_ctx[None, :], jnp.zeros((pad, d), F32)], axis=0)
    mods = _ada_call(cvec, w_ada, b_ada).reshape(depth, cvec.shape[0], N_MOD, d)

    w_ffn_in_b, w_ffn_out_b = w_ffn_in.astype(BF16), w_ffn_out.astype(BF16)
    rg_w_in_b, rg_w_out_b = rg_w_in.astype(BF16), rg_w_out.astype(BF16)
    fifth = jnp.arange(hg_w_in.shape[-1]) // d
    hg_w_in_b = (hg_w_in * jnp.where((fifth == 1) | (fifth == 2), 0.5, 1.0)).astype(BF16)
    hg_w_out_b = hg_w_out.astype(BF16)

    x_parts = [(ctx.reshape(n_batch * ctx_len, d), 0), (x.reshape(n_batch * seq_len, d), 0)]
    out = None
    for i in range(depth):
        last = i == depth - 1
        sh1, sc1, ga1, sh2, sc2, ga2 = [mods[i, :, k][:, None, :] for k in range(N_MOD)]
        j = i // N_MIXERS
        if i % N_MIXERS == 0:
            p = _proj_call(x_parts, g_mix[i][None], sh1, sc1, rg_w_in_b, j, rows)
            w_gates = (0.5 * jnp.concatenate([rg_w_a[j, 0], rg_w_i[j, 0], rg_w_a[j, 1], rg_w_i[j, 1]], axis=-1)).astype(BF16)
            b_gates = 0.5 * jnp.concatenate([rg_b_a[j, 0], rg_b_i[j, 0], rg_b_a[j, 1], rg_b_i[j, 1]], axis=-1)[:, None, :]
            y, y_ctx = _rg_call(p, rg_conv_w[j], rg_conv_b[j][None], w_gates, b_gates, rg_lam[j], need_ctx=not last, **dims)
            w_out = rg_w_out_b
        else:
            p = _proj_call(x_parts, g_mix[i][None], sh1, sc1, hg_w_in_b, j, rows)
            y, y_ctx = _hg_call(p, hg_lb, hg_norm[j][None], layer=i, need_ctx=not last, **dims)
            w_out = hg_w_out_b
        if last:
            xs = _outproj_latent(y, w_out, j, x_parts, ga1, rows)
            out = _ffn_call(xs, 0, g_ffn[i][None], sh2, sc2, ga2, w_ffn_in_b, w_ffn_out_b, i, rows, nct,
                            final_g=g_final[None])
        else:
            xs = _outproj_call([(y_ctx, 0), (y, nct)], w_out, j, x_parts, ga1, rows, 0)
            xs = _ffn_call(xs, 0, g_ffn[i][None], sh2, sc2, ga2, w_ffn_in_b, w_ffn_out_b, i, rows, 0)
            x_parts = [(xs, 0)]
    return out.reshape(n_batch, seq_len, d)


def _outproj_latent(y, w_out, j, x_parts, gate, rows):
    nct = rows.n_ctx_tiles
    if len(x_parts) == 2:
        x_lat = [(x_parts[1][0], -nct)]
    else:
        x_lat = [(x_parts[0][0], 0)]
    return _outproj_call([(y, 0)], w_out, j, x_lat, gate, rows, nct)
```

```python
import functools

import jax
import jax.numpy as jnp
from jax import lax
from jax.experimental import pallas as pl
from jax.experimental.pallas import tpu as pltpu

F32 = jnp.float32
BF16 = jnp.bfloat16

EPS = 1e-6
RG_C = 8.0
GRID_W = 64
N_MOD = 6
N_MIXERS = 2
CONV_TAPS = 4
CONV_LEFT = 2

V7X_LANES = 128
V7X_SUBLANES = 8
V7X_VMEM_LIMIT_CAP = 57 * 1024 * 1024

HG_CHUNK = 32
EXP_CLAMP = 80.0
LOG2_E = 1.4426950408889634
ROW_CHUNK = 128
ROW_TILE = 1024


def _col_tile(n, preferred=512):
    t = preferred
    while n % t:
        t -= V7X_LANES
    return t


def _params(sem, nbytes):
    return pltpu.CompilerParams(dimension_semantics=sem, vmem_limit_bytes=int(min(V7X_VMEM_LIMIT_CAP, nbytes)))


class _Rows:
    def __init__(self, n_batch, ctx_len, seq_len, tm):
        assert (n_batch * ctx_len) % tm == 0 and seq_len % tm == 0
        self.n_batch, self.ctx_len, self.seq_len, self.tm = n_batch, ctx_len, seq_len, tm
        self.n_ctx_tiles = n_batch * ctx_len // tm
        self.tiles_per_seq = seq_len // tm
        self.n_lat_tiles = n_batch * self.tiles_per_seq
        self.n_tiles = self.n_ctx_tiles + self.n_lat_tiles

    def mod_row(self, tile):
        return jnp.where(tile < self.n_ctx_tiles, self.n_batch, (tile - self.n_ctx_tiles) // self.tiles_per_seq)


def _mod_spec(rows, first_tile, ncols, col_of_j):
    return pl.BlockSpec((None, 1, ncols), lambda i, j: (rows.mod_row(i + first_tile), 0, col_of_j(j)))


def _part_specs(parts, rows, first_tile, ncols, col_of_j):
    if len(parts) == 1:
        (_, off), = parts
        return [pl.BlockSpec((rows.tm, ncols), lambda i, j: (i + first_tile + off, col_of_j(j)))]
    assert first_tile == 0
    (_, off_c), (_, off_l) = parts
    nct = rows.n_ctx_tiles
    return [pl.BlockSpec((rows.tm, ncols), lambda i, j: (jnp.minimum(i, nct - 1) + off_c, col_of_j(j))),
            pl.BlockSpec((rows.tm, ncols), lambda i, j: (jnp.maximum(i - nct, 0) + off_l, col_of_j(j)))]


def _with_part(refs, n_ctx_tiles, fn):
    if len(refs) == 1:
        fn(refs[0])
        return
    is_ctx = pl.program_id(0) < n_ctx_tiles
    pl.when(is_ctx)(lambda: fn(refs[0]))
    pl.when(jnp.logical_not(is_ctx))(lambda: fn(refs[1]))


def _for_row_chunks(tm, fn):
    def body(r, _):
        fn(pl.ds(pl.multiple_of(r * ROW_CHUNK, ROW_CHUNK), ROW_CHUNK))
        return 0

    lax.fori_loop(0, tm // ROW_CHUNK, body, 0)


def _rms_modulate_into(h_ref, inv_ref, x_ref, g_ref, sh_ref, sc_ref, tm):
    def stats(rows):
        x = x_ref[rows, :]
        inv_ref[rows, :] = lax.rsqrt(jnp.mean(x * x, axis=-1, keepdims=True) + EPS)

    _for_row_chunks(tm, stats)
    gain = g_ref[...] * (1.0 + sc_ref[...])
    shift = sh_ref[...]

    def affine(rows):
        h_ref[rows, :] = (x_ref[rows, :] * inv_ref[rows, :] * gain + shift).astype(h_ref.dtype)

    _for_row_chunks(tm, affine)


def _ada_kernel(c_ref, w_ref, b_ref, o_ref):
    c = c_ref[...]
    s = (c * jax.nn.sigmoid(c)).astype(BF16)
    o_ref[...] = jnp.dot(s, w_ref[...].astype(BF16), preferred_element_type=F32) + b_ref[...]


def _ada_call(cvec, w_ada, b_ada):
    depth, d, n = w_ada.shape
    r = cvec.shape[0]
    tn = _col_tile(n)
    est = 2 * (d * tn * 4) + d * tn * 2 + 4 * r * (d + 2 * tn) * 4 + (4 << 20)
    return pl.pallas_call(
        _ada_kernel,
        grid=(depth, n // tn),
        in_specs=[
            pl.BlockSpec((r, d), lambda l, j: (0, 0)),
            pl.BlockSpec((None, d, tn), lambda l, j: (l, 0, j)),
            pl.BlockSpec((None, 1, tn), lambda l, j: (l, 0, j)),
        ],
        out_specs=pl.BlockSpec((None, r, tn), lambda l, j: (l, 0, j)),
        out_shape=jax.ShapeDtypeStruct((depth, r, n), F32),
        compiler_params=_params(("parallel", "parallel"), est),
        name="ada_mod",
    )(cvec, w_ada, b_ada.reshape(depth, 1, n))


def _proj_kernel(*refs, n_x, tm, n_ctx_tiles):
    x_refs = refs[:n_x]
    g_ref, sh_ref, sc_ref, w_ref, o_ref, h_ref, inv_ref = refs[n_x:]

    @pl.when(pl.program_id(1) == 0)
    def _():
        _with_part(x_refs, n_ctx_tiles, lambda x_ref: _rms_modulate_into(h_ref, inv_ref, x_ref, g_ref, sh_ref, sc_ref, tm))

    o_ref[...] = jnp.dot(h_ref[...], w_ref[...], preferred_element_type=F32)


def _proj_call(x_parts, g, shift, scale, w, layer, rows):
    d = x_parts[0][0].shape[1]
    n = w.shape[2]
    nx = len(x_parts)
    tm, tn = rows.tm, _col_tile(n, 1024 if nx == 1 else 512)
    est = 2 * nx * tm * d * 4 + tm * d * 2 + 2 * d * tn * 2 + 3 * tm * tn * 4 + 8 * ROW_CHUNK * d * 4 + (3 << 20)
    kern = functools.partial(_proj_kernel, n_x=nx, tm=tm, n_ctx_tiles=rows.n_ctx_tiles)
    full = lambda j: 0
    return pl.pallas_call(
        kern,
        grid=(rows.n_tiles, n // tn),
        in_specs=[*_part_specs(x_parts, rows, 0, d, full),
                  pl.BlockSpec((1, d), lambda i, j: (0, 0)),
                  _mod_spec(rows, 0, d, full), _mod_spec(rows, 0, d, full),
                  pl.BlockSpec((None, d, tn), lambda i, j: (layer, 0, j))],
        out_specs=pl.BlockSpec((tm, tn), lambda i, j: (i, j)),
        out_shape=jax.ShapeDtypeStruct((rows.n_tiles * tm, n), F32),
        scratch_shapes=[pltpu.VMEM((tm, d), BF16), pltpu.VMEM((tm, 1), F32)],
        compiler_params=_params(("parallel", "arbitrary"), est),
        name="norm_mod_proj",
    )(*[a for a, _ in x_parts], g, shift, scale, w)


def _outproj_kernel(*refs, n_a, n_x, n_ctx_tiles):
    a_refs, x_refs = refs[:n_a], refs[n_a:n_a + n_x]
    w_ref, ga_ref, o_ref = refs[n_a + n_x:]

    def with_a(a_ref):
        acc = jnp.dot(a_ref[...], w_ref[...], preferred_element_type=F32) * ga_ref[...]

        def with_x(x_ref):
            o_ref[...] = x_ref[...] + acc

        _with_part(x_refs, n_ctx_tiles, with_x)

    _with_part(a_refs, n_ctx_tiles, with_a)


def _outproj_call(a_parts, w, layer, x_parts, gate, rows, first_tile):
    k, n = w.shape[1], w.shape[2]
    tm, tn = rows.tm, _col_tile(n, 1024)
    na, nx = len(a_parts), len(x_parts)
    n_tiles = rows.n_tiles - first_tile
    est = 2 * na * tm * k * 2 + 2 * k * tn * 2 + 2 * (nx + 1) * tm * tn * 4 + 3 * tm * tn * 4 + (3 << 20)
    kern = functools.partial(_outproj_kernel, n_a=na, n_x=nx, n_ctx_tiles=rows.n_ctx_tiles)
    return pl.pallas_call(
        kern,
        grid=(n_tiles, n // tn),
        in_specs=[*_part_specs(a_parts, rows, first_tile, k, lambda j: 0),
                  *_part_specs(x_parts, rows, first_tile, tn, lambda j: j),
                  pl.BlockSpec((None, k, tn), lambda i, j: (layer, 0, j)),
                  _mod_spec(rows, first_tile, tn, lambda j: j)],
        out_specs=pl.BlockSpec((tm, tn), lambda i, j: (i, j)),
        out_shape=jax.ShapeDtypeStruct((n_tiles * tm, n), F32),
        compiler_params=_params(("parallel", "parallel"), est),
        name="outproj_residual",
    )(*[a for a, _ in a_parts], *[a for a, _ in x_parts], w, gate)


def _ffn_kernel(*refs, tm, final_norm):
    if final_norm:
        x_ref, g_ref, sh_ref, sc_ref, ga_ref, wg_ref, wu_ref, wo_ref, gf_ref, o_ref, h_ref, inv_ref = refs
    else:
        x_ref, g_ref, sh_ref, sc_ref, ga_ref, wg_ref, wu_ref, wo_ref, o_ref, h_ref, inv_ref = refs
    f = pl.program_id(1)

    @pl.when(f == 0)
    def _():
        _rms_modulate_into(h_ref, inv_ref, x_ref, g_ref, sh_ref, sc_ref, tm)
        o_ref[...] = jnp.zeros_like(o_ref)

    tn = _col_tile(o_ref.shape[1])
    half = tm // 2
    for r0 in range(0, tm, half):
        h = h_ref[r0:r0 + half, :]
        gate = jnp.dot(h, wg_ref[...], preferred_element_type=F32)
        up = jnp.dot(h, wu_ref[...], preferred_element_type=F32)
        act = (gate * jax.nn.sigmoid(gate) * up).astype(BF16)
        for c0 in range(0, o_ref.shape[1], tn):
            o_ref[r0:r0 + half, c0:c0 + tn] += jnp.dot(act, wo_ref[:, c0:c0 + tn], preferred_element_type=F32)

    @pl.when(f == pl.num_programs(1) - 1)
    def _():
        res_gate = ga_ref[...]

        def chunk(rows):
            y = x_ref[rows, :] + res_gate * o_ref[rows, :]
            if final_norm:
                y = y * lax.rsqrt(jnp.mean(y * y, axis=-1, keepdims=True) + EPS) * gf_ref[...]
            o_ref[rows, :] = y

        _for_row_chunks(tm, chunk)


def _ffn_call(xs, x_first_tile, g, shift, scale, res_gate, w_in, w_out, layer, rows, first_tile, final_g=None):
    d = xs.shape[1]
    ff = w_out.shape[1]
    tm, tf = rows.tm, _col_tile(ff)
    nf = ff // tf
    n_tiles = rows.n_tiles - first_tile
    est = (4 * tm * d * 4 + tm * d * 2 + 2 * (2 * d * tf * 2 + tf * d * 2) + 2 * tm * tf * 4 + tm * tf * 2
           + 8 * ROW_CHUNK * d * 4 + (3 << 20))
    kern = functools.partial(_ffn_kernel, tm=tm, final_norm=final_g is not None)
    full = lambda f: 0
    extra_specs = [] if final_g is None else [pl.BlockSpec((1, d), lambda i, f: (0, 0))]
    extra_args = [] if final_g is None else [final_g]
    return pl.pallas_call(
        kern,
        grid=(n_tiles, nf),
        in_specs=[pl.BlockSpec((tm, d), lambda i, f: (i + x_first_tile, 0)),
                  pl.BlockSpec((1, d), lambda i, f: (0, 0)),
                  _mod_spec(rows, first_tile, d, full), _mod_spec(rows, first_tile, d, full),
                  _mod_spec(rows, first_tile, d, full),
                  pl.BlockSpec((None, d, tf), lambda i, f: (layer, 0, f)),
                  pl.BlockSpec((None, d, tf), lambda i, f: (layer, 0, nf + f)),
                  pl.BlockSpec((None, tf, d), lambda i, f: (layer, f, 0)),
                  *extra_specs],
        out_specs=pl.BlockSpec((tm, d), lambda i, f: (i, 0)),
        out_shape=jax.ShapeDtypeStruct((n_tiles * tm, d), F32),
        scratch_shapes=[pltpu.VMEM((tm, d), BF16), pltpu.VMEM((tm, 1), F32)],
        compiler_params=_params(("parallel", "arbitrary"), est),
        name="swiglu_ffn",
    )(xs, g, shift, scale, res_gate, w_in, w_in, w_out, *extra_args)


def _softplus(x):
    return jnp.maximum(x, 0.0) + jnp.log1p(jnp.exp(-jnp.abs(x)))


def _chunk_carries(h_loc, p_loc, h_init, reverse):
    n = h_loc.shape[0]
    carries = [None] * n
    c = h_init
    for s in (reversed(range(n)) if reverse else range(n)):
        carries[s] = c
        c = p_loc[s:s + 1] * c + h_loc[s:s + 1]
    return carries, c


def _strided_scan(coef, res, n, stride):
    nsub = V7X_SUBLANES
    af_ref, bf_ref, ab_ref, bb_ref = coef
    pf_ref, hf_ref, pb_ref, hb_ref = res
    lanes = af_ref.shape[1]

    def body(i, carry):
        hf, pf, hb, pb = carry
        rows_f = pl.ds(i, nsub, stride=stride)
        rows_b = pl.ds(n - 1 - i, nsub, stride=stride)
        a = af_ref[rows_f, :]
        hf = a * hf + bf_ref[rows_f, :]
        pf = a * pf
        hf_ref[rows_f, :] = hf
        pf_ref[rows_f, :] = pf
        a = ab_ref[rows_b, :]
        hb = a * hb + bb_ref[rows_b, :]
        pb = a * pb
        hb_ref[rows_b, :] = hb
        pb_ref[rows_b, :] = pb
        return hf, pf, hb, pb

    zero = jnp.zeros((nsub, lanes), F32)
    one = jnp.ones((nsub, lanes), F32)
    return lax.fori_loop(0, n, body, (zero, one, zero, one), unroll=8)


def _rg_kernel(*refs, ctx_len, seq_len, need_ctx):
    xc_ref, xl_ref, gc_ref, gl_ref, cw_ref, cb_ref, wg_ref, bg_ref, lam_ref = refs[:9]
    n_out = 2 if need_ctx else 1
    y_ref = refs[9]
    yc_ref = refs[10] if need_ctx else None
    (xp_ref, laf, lbf, lab, lbb, lpf, lhf, lpb, lhb, caf, cbf, cab, cbb, cpf, chf, cpb, chb, carry_ref) = refs[9 + n_out:]
    nsub = V7X_SUBLANES
    lanes = xc_ref.shape[1]
    rb = ctx_len
    nblk = 1 + seq_len // rb
    halo = nsub
    lat_stride = rb + nsub
    ctx_n = ctx_len // nsub
    ctx_stride = ctx_n + nsub

    xp_ref[pl.ds(0, halo), :] = jnp.zeros((halo, lanes), F32)
    xp_ref[pl.ds(halo + ctx_len + seq_len, halo), :] = jnp.zeros((halo, lanes), F32)
    xp_ref[pl.ds(halo, ctx_len), :] = xc_ref[...]
    xp_ref[pl.ds(halo + ctx_len, seq_len), :] = xl_ref[...]

    cw = cw_ref[...]
    cb = cb_ref[...]
    softplus_neg_lam = _softplus(-lam_ref[...])
    local_row = lax.broadcasted_iota(jnp.int32, (rb, 1), 0)

    def coeffs(blk):
        start = blk * rb if isinstance(blk, int) else pl.multiple_of(blk * rb, rb)
        win = xp_ref[pl.ds(start, rb + 2 * halo), :]
        seg_first = blk <= 1
        seg_last = jnp.logical_or(blk == 0, blk == nblk - 1)
        taps = []
        for k in range(CONV_TAPS):
            off = k - CONV_LEFT
            tap = win[halo + off:halo + off + rb]
            if off < 0:
                tap = jnp.where(jnp.logical_or(local_row >= -off, jnp.logical_not(seg_first)), tap, 0.0)
            elif off > 0:
                tap = jnp.where(jnp.logical_or(local_row < rb - off, jnp.logical_not(seg_last)), tap, 0.0)
            taps.append(tap)
        xc = cb
        for k in range(CONV_TAPS):
            xc = xc + cw[k:k + 1] * taps[k]
        half_z = jnp.dot(xc.astype(BF16), wg_ref[...], preferred_element_type=F32) + bg_ref[...]
        half_xc = 0.5 * xc
        out = []
        for d in range(2):
            th_r = jnp.tanh(half_z[:, (2 * d) * lanes:(2 * d + 1) * lanes])
            th_i = jnp.tanh(half_z[:, (2 * d + 1) * lanes:(2 * d + 2) * lanes])
            scale = (-0.5 * RG_C) * softplus_neg_lam[d:d + 1]
            log_a = scale * th_r + scale
            a = jnp.exp(log_a)
            t = jnp.tanh(log_a)
            u = (-2.0) * t / (1.0 - t)
            mult = jnp.where(u > 0.0, u * lax.rsqrt(u), 0.0)
            out.append((a, mult * half_xc * (1.0 + th_i)))
        return out

    (a_f, b_f), (a_b, b_b) = coeffs(0)
    for c in range(nsub):
        dst = pl.ds(c * ctx_stride, ctx_n)
        src = slice(c * ctx_n, (c + 1) * ctx_n)
        caf[dst, :] = a_f[src]
        cbf[dst, :] = b_f[src]
        cab[dst, :] = a_b[src]
        cbb[dst, :] = b_b[src]

    def fill(blk, _):
        (a_f, b_f), (a_b, b_b) = coeffs(blk)
        dst = pl.ds(pl.multiple_of((blk - 1) * lat_stride, nsub), rb)
        laf[dst, :] = a_f
        lbf[dst, :] = b_f
        lab[dst, :] = a_b
        lbb[dst, :] = b_b
        return 0

    lax.fori_loop(1, nblk, fill, 0, unroll=2)

    zero_state = jnp.zeros((1, lanes), F32)
    hf, pf, hb, pb = _strided_scan((caf, cbf, cab, cbb), (cpf, chf, cpb, chb), ctx_n, ctx_stride)
    ctx_in_f, ctx_state_f = _chunk_carries(hf, pf, zero_state, reverse=False)
    ctx_in_b, ctx_state_b = _chunk_carries(hb, pb, zero_state, reverse=True)
    hf, pf, hb, pb = _strided_scan((laf, lbf, lab, lbb), (lpf, lhf, lpb, lhb), rb, lat_stride)
    lat_in_f, _ = _chunk_carries(hf, pf, ctx_state_f, reverse=False)
    lat_in_b, _ = _chunk_carries(hb, pb, ctx_state_b, reverse=True)
    for s in range(nsub):
        carry_ref[pl.ds(s, 1), :] = lat_in_f[s]
        carry_ref[pl.ds(nsub + s, 1), :] = lat_in_b[s]

    if need_ctx:
        for c in range(nsub):
            src = pl.ds(c * ctx_stride, ctx_n)
            h = chf[src, :] + cpf[src, :] * ctx_in_f[c] + chb[src, :] + cpb[src, :] * ctx_in_b[c]
            rows = pl.ds(c * ctx_n, ctx_n)
            yc_ref[rows, :] = (h * jax.nn.gelu(gc_ref[rows, :])).astype(yc_ref.dtype)

    def emit(blk, _):
        src = pl.ds(pl.multiple_of((blk - 1) * lat_stride, nsub), rb)
        cf = carry_ref[pl.ds(blk - 1, 1), :]
        cbw = carry_ref[pl.ds(nsub + blk - 1, 1), :]
        h = lhf[src, :] + lpf[src, :] * cf + lhb[src, :] + lpb[src, :] * cbw
        rows = pl.ds(pl.multiple_of((blk - 1) * rb, rb), rb)
        y_ref[rows, :] = (h * jax.nn.gelu(gl_ref[rows, :])).astype(y_ref.dtype)
        return 0

    lax.fori_loop(1, nblk, emit, 0, unroll=2)


def _mixer_out(t, w, n_batch, ctx_len, seq_len, lanes, need_ctx):
    specs = [pl.BlockSpec((seq_len, lanes), lambda b, h: (b, h))]
    shapes = [jax.ShapeDtypeStruct((n_batch * seq_len, w), BF16)]
    if need_ctx:
        specs.append(pl.BlockSpec((ctx_len, lanes), lambda b, h: (b, h)))
        shapes.append(jax.ShapeDtypeStruct((n_batch * ctx_len, w), BF16))
    return specs, shapes


def _rg_call(p, conv_w, conv_b, w_gates, b_gates, lam, *, n_batch, ctx_len, seq_len, need_ctx):
    t, w2 = p.shape
    w = w2 // 2
    lanes = V7X_LANES
    nh = w // lanes
    nsub = V7X_SUBLANES
    lb = ctx_len + seq_len
    lat_block0 = n_batch * ctx_len // seq_len
    lat_rows = nsub * (seq_len // nsub + nsub)
    ctx_rows = nsub * (ctx_len // nsub + nsub)
    est = (4 * lb * lanes * 4 + 2 * lb * lanes * 2 + (lb + 2 * nsub) * lanes * 4 + 8 * (lat_rows + ctx_rows) * lanes * 4
           + 48 * ctx_len * lanes * 4 + (4 << 20))
    kern = functools.partial(_rg_kernel, ctx_len=ctx_len, seq_len=seq_len, need_ctx=need_ctx)
    out_specs, out_shapes = _mixer_out(t, w, n_batch, ctx_len, seq_len, lanes, need_ctx)
    outs = pl.pallas_call(
        kern,
        grid=(n_batch, nh),
        in_specs=[pl.BlockSpec((ctx_len, lanes), lambda b, h: (b, h)),
                  pl.BlockSpec((seq_len, lanes), lambda b, h: (lat_block0 + b, h)),
                  pl.BlockSpec((ctx_len, lanes), lambda b, h: (b, nh + h)),
                  pl.BlockSpec((seq_len, lanes), lambda b, h: (lat_block0 + b, nh + h)),
                  pl.BlockSpec((CONV_TAPS, lanes), lambda b, h: (0, h)),
                  pl.BlockSpec((1, lanes), lambda b, h: (0, h)),
                  pl.BlockSpec((None, lanes, 4 * lanes), lambda b, h: (h, 0, 0)),
                  pl.BlockSpec((None, 1, 4 * lanes), lambda b, h: (h, 0, 0)),
                  pl.BlockSpec((2, lanes), lambda b, h: (0, h))],
        out_specs=out_specs,
        out_shape=out_shapes,
        scratch_shapes=[pltpu.VMEM((lb + 2 * nsub, lanes), F32)]
        + [pltpu.VMEM((lat_rows, lanes), F32)] * 8
        + [pltpu.VMEM((ctx_rows, lanes), F32)] * 8
        + [pltpu.VMEM((2 * nsub, lanes), F32)],
        compiler_params=_params(("parallel", "parallel"), est),
        name="rglru_mixer",
    )(p, p, p, p, conv_w, conv_b, w_gates, b_gates, lam)
    return outs if need_ctx else (outs[0], None)


def _group_cumsum(x, group, reverse):
    n = x.shape[0]
    pos = lax.broadcasted_iota(jnp.int32, (n, 1), 0) % group
    step = 1
    while step < group:
        if reverse:
            shifted = pltpu.roll(x, n - step, axis=0)
            keep = pos < group - step
        else:
            shifted = pltpu.roll(x, step, axis=0)
            keep = pos >= step
        x = x + jnp.where(keep, shifted, 0.0)
        step *= 2
    return x


def _hg_kernel(*refs, ctx_len, seq_len, layer, dk, unroll, need_ctx):
    in_refs = refs[:10]
    lb_ref, nw_ref = refs[10:12]
    n_out = 2 if need_ctx else 1
    y_ref = refs[12]
    yc_ref = refs[13] if need_ctx else None
    (pitch, qs, fs, bs, vs, qd_f, qd_b, oacc, kv_f, kv_b, st_f, st_b,
     k_slow, cum_slow, o_slow, deepest_ref) = refs[12 + n_out:]
    lanes = qs.shape[1]
    ch = HG_CHUNK
    lb_rows = ctx_len + seq_len
    nchunk = lb_rows // ch
    ctx_chunks = ctx_len // ch
    sb = ctx_len
    nsb = lb_rows // sb
    cps = sb // ch
    ncol = seq_len // ch
    grid_pitch = ncol + V7X_SUBLANES
    qd_refs, kv_refs, st_refs = (qd_f, qd_b), (kv_f, kv_b), (st_f, st_b)

    def to_scan_order(ctx_src, lat_src, dst_ref):
        dst_ref[pl.ds(0, ctx_len), :] = ctx_src[...]

        def spread(r, _):
            src = pl.ds(pl.multiple_of(r * ncol, ncol), ncol)
            pitch[pl.ds(pl.multiple_of(r * grid_pitch, V7X_SUBLANES), ncol), :] = lat_src[src, :]
            return 0

        lax.fori_loop(0, ch, spread, 0, unroll=8)

        def column(n, _):
            dst = pl.ds(pl.multiple_of(ctx_len + n * ch, ch), ch)
            dst_ref[dst, :] = pitch[pl.ds(n, ch, stride=grid_pitch), :]
            return 0

        lax.fori_loop(0, ncol, column, 0, unroll=8)

    for k, dst_ref in enumerate((qs, fs, bs, vs)):
        to_scan_order(in_refs[2 * k], in_refs[2 * k + 1], dst_ref)
    gc_ref, gl_ref = in_refs[8:10]

    lbp = lb_ref[...]
    e = jnp.exp(lbp - jnp.max(lbp, axis=0, keepdims=True))
    sm = e / jnp.sum(e, axis=0, keepdims=True)
    lower = jnp.zeros(sm.shape[1:], F32)
    for l in range(1, layer + 1):
        lower = lower + sm[l]

    q_scale = dk ** -0.5
    j_idx = lax.broadcasted_iota(jnp.int32, (cps, ch, ch), 1)
    s_idx = lax.broadcasted_iota(jnp.int32, (cps, ch, ch), 2)
    causal = (j_idx >= s_idx, j_idx <= s_idx)
    aux = 2 * V7X_SUBLANES
    aux_row = lax.broadcasted_iota(jnp.int32, (1, aux, 1), 1)

    def bf16_pieces(x):
        hi = x.astype(BF16)
        rest = x - hi.astype(F32)
        mid = rest.astype(BF16)
        lo = (rest - mid.astype(F32)).astype(BF16)
        return hi, mid, lo

    def gate_feats(rows, d):
        lo = lower[d:d + 1]
        half_open = 0.5 * (1.0 - lo)
        th = jnp.tanh((bs if d else fs)[rows, :])
        log_f = jnp.log(lo + half_open * (1.0 + th))
        k = half_open * (1.0 - th)
        return k, _group_cumsum(log_f, ch, reverse=bool(d))

    def exact_intra(rows):
        q3 = (qs[rows, :] * q_scale).reshape(cps, ch, lanes)
        v3 = vs[rows, :].reshape(cps, ch, lanes)
        pos = lax.broadcasted_iota(jnp.int32, (1, ch, 1), 1)
        for d in range(2):
            k, cum = gate_feats(rows, d)
            k_slow[...] = k.reshape(cps, ch, lanes)
            cum_slow[...] = cum.reshape(cps, ch, lanes)

            def row_j(j, _):
                cum3 = cum_slow[...]
                w = jnp.exp(jnp.minimum(cum_slow[:, pl.ds(j, 1), :] - cum3, 0.0))
                valid = (pos >= j) if d else (pos <= j)
                q_j = jnp.where(pos == j, q3, 0.0).sum(axis=1, keepdims=True)
                col = jnp.sum(jnp.where(valid, q_j * k_slow[...] * w, 0.0), axis=-1, keepdims=True)
                o_j = jnp.sum(col * v3, axis=1, keepdims=True)
                if d:
                    o_slow[:, pl.ds(j, 1), :] += o_j
                else:
                    o_slow[:, pl.ds(j, 1), :] = o_j
                return 0

            lax.fori_loop(0, ch, row_j, 0)
        oacc[rows, :] = o_slow[...].reshape(sb, lanes)

    def local_part(i, _):
        rows = pl.ds(pl.multiple_of(i * sb, sb), sb)
        q = qs[rows, :] * q_scale
        v3 = vs[rows, :].reshape(cps, ch, lanes).astype(BF16)
        zeros_v = jnp.zeros((cps, ch, lanes), BF16)
        pad_rows = jnp.concatenate([jnp.zeros((cps, aux, lanes), BF16), jnp.ones((cps, aux, lanes), BF16)], axis=2)
        kv_rhs = jnp.concatenate([jnp.concatenate([v3, zeros_v], axis=2), pad_rows], axis=1)
        o_sum = None
        deepest = None
        for d in range(2):
            k, cum = gate_feats(rows, d)
            cum_l = cum * LOG2_E
            cum_l3 = cum_l.reshape(cps, ch, lanes)
            total_l3 = cum_l3[:, 0:1, :] if d else cum_l3[:, ch - 1:ch, :]
            qd = (q * jnp.exp2(cum_l)).astype(BF16)
            kd = (k * jnp.exp2(jnp.minimum(cum * (-LOG2_E), EXP_CLAMP * LOG2_E))).astype(BF16)
            kl = (k.reshape(cps, ch, lanes) * jnp.exp2(total_l3 - cum_l3)).astype(BF16)
            att = jnp.einsum("cjd,csd->cjs", qd.reshape(cps, ch, lanes), kd.reshape(cps, ch, lanes),
                             preferred_element_type=F32)
            att = jnp.where(causal[d], att, 0.0)
            o_intra = jnp.einsum("cjs,cse->cje", att.astype(BF16), v3, preferred_element_type=F32)
            o_sum = o_intra if o_sum is None else o_sum + o_intra
            qd_refs[d][rows, :] = qd
            d_hi, d_mid, d_lo = [p.astype(F32) for p in bf16_pieces(jnp.exp2(total_l3))]
            dec_rows = jnp.where(aux_row == 0, d_hi, jnp.where(aux_row == 1, d_mid, jnp.where(aux_row == 2, d_lo, 0.0)))
            kv_lhs = jnp.concatenate([kl, dec_rows.astype(BF16)], axis=1)
            for c in range(cps):
                kv_refs[d][i * cps + c] = lax.dot_general(kv_lhs[c], kv_rhs[c], (((0,), (0,)), ((), ())),
                                                          preferred_element_type=F32)
            low = jnp.min(total_l3, axis=0)
            deepest = low if deepest is None else jnp.minimum(deepest, low)
        oacc[rows, :] = o_sum.reshape(sb, lanes)
        deepest_ref[i] = jnp.min(deepest)
        return 0

    lax.fori_loop(0, nsb, local_part, 0, unroll=unroll)

    def redo_unsafe(i, _):
        @pl.when(deepest_ref[i] < -EXP_CLAMP * LOG2_E)
        def _():
            exact_intra(pl.ds(pl.multiple_of(i * sb, sb), sb))

        return 0

    lax.fori_loop(0, nsb, redo_unsafe, 0)

    def state_step(d, n, st):
        st_refs[d][n] = st.astype(BF16)
        kv = kv_refs[d][n]
        return kv[:, lanes:] * st + kv[:, :lanes]

    zero_state = jnp.zeros((lanes, lanes), F32)
    lax.fori_loop(0, nchunk, functools.partial(state_step, 0), zero_state, unroll=4)
    st = lax.fori_loop(0, ctx_chunks, lambda i, s: state_step(1, ctx_chunks - 1 - i, s), zero_state, unroll=4)
    lax.fori_loop(0, nchunk - ctx_chunks, lambda i, s: state_step(1, nchunk - 1 - i, s), st, unroll=4)

    nw = nw_ref[...]

    def inter_part(i, _):
        rows = pl.ds(pl.multiple_of(i * sb, sb), sb)
        o = oacc[rows, :]
        for d in range(2):
            qd3 = qd_refs[d][rows, :].reshape(cps, ch, lanes)
            st3 = st_refs[d][pl.ds(i * cps, cps)]
            o = o + jnp.einsum("cjd,cde->cje", qd3, st3, preferred_element_type=F32).reshape(sb, lanes)
        oacc[rows, :] = o * lax.rsqrt(jnp.mean(o * o, axis=-1, keepdims=True) + EPS) * nw
        return 0

    first_sb = 0 if need_ctx else ctx_len // sb
    lax.fori_loop(first_sb, nsb, inter_part, 0, unroll=True)

    def gated(o, g):
        return (o * (g * jax.nn.sigmoid(g))).astype(y_ref.dtype)

    if need_ctx:
        yc_ref[...] = gated(oacc[pl.ds(0, ctx_len), :], gc_ref[...])

    def column(n, _):
        src = pl.ds(pl.multiple_of(ctx_len + n * ch, ch), ch)
        pitch[pl.ds(n, ch, stride=grid_pitch), :] = oacc[src, :]
        return 0

    lax.fori_loop(0, ncol, column, 0, unroll=8)

    def gather_row(r, _):
        dst = pl.ds(pl.multiple_of(r * ncol, ncol), ncol)
        y_ref[dst, :] = gated(pitch[pl.ds(pl.multiple_of(r * grid_pitch, V7X_SUBLANES), ncol), :], gl_ref[dst, :])
        return 0

    lax.fori_loop(0, ch, gather_row, 0, unroll=8)


def _hg_call(p, hg_lb, norm_w, *, n_batch, ctx_len, seq_len, layer, need_ctx):
    t, w5 = p.shape
    w = w5 // 5
    lanes = V7X_LANES
    nh = w // lanes
    lb = ctx_len + seq_len
    lat_block0 = n_batch * ctx_len // seq_len
    nchunk = lb // HG_CHUNK
    depth = hg_lb.shape[0]
    blk_bytes = lb * lanes * 4
    cps = ctx_len // HG_CHUNK
    pitch_rows = HG_CHUNK * (seq_len // HG_CHUNK + V7X_SUBLANES)
    unroll = lb // ctx_len
    est = (10 * blk_bytes + 2 * lb * lanes * 2 + 8 * blk_bytes + pitch_rows * lanes * 4 + 2 * nchunk * lanes * lanes * 10
           + unroll * 40 * ctx_len * lanes * 4 + (4 << 20))
    kern = functools.partial(_hg_kernel, ctx_len=ctx_len, seq_len=seq_len, layer=layer, dk=lanes, unroll=unroll,
                             need_ctx=need_ctx)
    in_specs = []
    for k in range(5):
        in_specs.append(pl.BlockSpec((ctx_len, lanes), lambda b, h, k=k: (b, k * nh + h)))
        in_specs.append(pl.BlockSpec((seq_len, lanes), lambda b, h, k=k: (lat_block0 + b, k * nh + h)))
    out_specs, out_shapes = _mixer_out(t, w, n_batch, ctx_len, seq_len, lanes, need_ctx)
    outs = pl.pallas_call(
        kern,
        grid=(n_batch, nh),
        in_specs=in_specs + [pl.BlockSpec((depth, 2, lanes), lambda b, h: (0, 0, h)),
                             pl.BlockSpec((1, lanes), lambda b, h: (0, 0))],
        out_specs=out_specs,
        out_shape=out_shapes,
        scratch_shapes=[pltpu.VMEM((pitch_rows, lanes), F32)]
        + [pltpu.VMEM((lb, lanes), F32)] * 4
        + [pltpu.VMEM((lb, lanes), BF16)] * 2
        + [pltpu.VMEM((lb, lanes), F32)]
        + [pltpu.VMEM((nchunk, lanes, 2 * lanes), F32)] * 2
        + [pltpu.VMEM((nchunk, lanes, lanes), BF16)] * 2
        + [pltpu.VMEM((cps, HG_CHUNK, lanes), F32)] * 3
        + [pltpu.SMEM((lb // ctx_len,), F32)],
        compiler_params=_params(("parallel", "parallel"), est),
        name="hgrn2_mixer",
    )(*([p] * 10), hg_lb, norm_w)
    return outs if need_ctx else (outs[0], None)


def kernel(x, c, ctx, c_ctx, w_ada, b_ada, g_mix, g_ffn, g_final, w_ffn_in, w_ffn_out,
           rg_w_in, rg_conv_w, rg_conv_b, rg_w_a, rg_b_a, rg_w_i, rg_b_i, rg_lam, rg_w_out,
           hg_w_in, hg_lb, hg_norm, hg_w_out):
    n_batch, seq_len, d = x.shape
    ctx_len = ctx.shape[1]
    depth = w_ada.shape[0]
    lanes = V7X_LANES
    assert seq_len == V7X_SUBLANES * ctx_len, "row blocking assumes the latent is 8 context lengths long"
    assert seq_len // GRID_W == HG_CHUNK, "an HGRN2 chunk must be one latent grid column"
    assert (n_batch * ctx_len) % seq_len == 0, "latent blocks must stay block-aligned behind the context rows"
    assert d % lanes == 0 and rg_w_a.shape[-1] == lanes and hg_norm.shape[-1] == lanes
    dims = dict(n_batch=n_batch, ctx_len=ctx_len, seq_len=seq_len)
    rows = _Rows(n_batch, ctx_len, seq_len, ROW_TILE)
    nct = rows.n_ctx_tiles

    pad = (-(n_batch + 1)) % V7X_SUBLANES
    cvec = jnp.concatenate([c, c_ctx[None, :], jnp.zeros((pad, d), F32)], axis=0)
    mods = _ada_call(cvec, w_ada, b_ada).reshape(depth, cvec.shape[0], N_MOD, d)

    w_ffn_in_b, w_ffn_out_b = w_ffn_in.astype(BF16), w_ffn_out.astype(BF16)
    rg_w_in_b, rg_w_out_b = rg_w_in.astype(BF16), rg_w_out.astype(BF16)
    fifth = jnp.arange(hg_w_in.shape[-1]) // d
    hg_w_in_b = (hg_w_in * jnp.where((fifth == 1) | (fifth == 2), 0.5, 1.0)).astype(BF16)
    hg_w_out_b = hg_w_out.astype(BF16)

    x_parts = [(ctx.reshape(n_batch * ctx_len, d), 0), (x.reshape(n_batch * seq_len, d), 0)]
    out = None
    for i in range(depth):
        last = i == depth - 1
        sh1, sc1, ga1, sh2, sc2, ga2 = [mods[i, :, k][:, None, :] for k in range(N_MOD)]
        j = i // N_MIXERS
        if i % N_MIXERS == 0:
            p = _proj_call(x_parts, g_mix[i][None], sh1, sc1, rg_w_in_b, j, rows)
            w_gates = (0.5 * jnp.concatenate([rg_w_a[j, 0], rg_w_i[j, 0], rg_w_a[j, 1], rg_w_i[j, 1]], axis=-1)).astype(BF16)
            b_gates = 0.5 * jnp.concatenate([rg_b_a[j, 0], rg_b_i[j, 0], rg_b_a[j, 1], rg_b_i[j, 1]], axis=-1)[:, None, :]
            y, y_ctx = _rg_call(p, rg_conv_w[j], rg_conv_b[j][None], w_gates, b_gates, rg_lam[j], need_ctx=not last, **dims)
            w_out = rg_w_out_b
        else:
            p = _proj_call(x_parts, g_mix[i][None], sh1, sc1, hg_w_in_b, j, rows)
            y, y_ctx = _hg_call(p, hg_lb, hg_norm[j][None], layer=i, need_ctx=not last, **dims)
            w_out = hg_w_out_b
        if last:
            xs = _outproj_latent(y, w_out, j, x_parts, ga1, rows)
            out = _ffn_call(xs, 0, g_ffn[i][None], sh2, sc2, ga2, w_ffn_in_b, w_ffn_out_b, i, rows, nct,
                            final_g=g_final[None])
        else:
            xs = _outproj_call([(y_ctx, 0), (y, 0)], w_out, j, x_parts, ga1, rows, 0)
            xs = _ffn_call(xs, 0, g_ffn[i][None], sh2, sc2, ga2, w_ffn_in_b, w_ffn_out_b, i, rows, 0)
            x_parts = [(xs, 0)]
    return out.reshape(n_batch, seq_len, d)


def _outproj_latent(y, w_out, j, x_parts, gate, rows):
    nct = rows.n_ctx_tiles
    if len(x_parts) == 2:
        x_lat = [(x_parts[1][0], -nct)]
    else:
        x_lat = [(x_parts[0][0], 0)]
    return _outproj_call([(y, -nct)], w_out, j, x_lat, gate, rows, nct)
```

```python
import functools

import jax
import jax.numpy as jnp
from jax import lax
from jax.experimental import pallas as pl
from jax.experimental.pallas import tpu as pltpu

F32 = jnp.float32
BF16 = jnp.bfloat16

EPS = 1e-6
RG_C = 8.0
GRID_W = 64
N_MOD = 6
N_MIXERS = 2
CONV_TAPS = 4
CONV_LEFT = 2

V7X_LANES = 128
V7X_SUBLANES = 8
V7X_VMEM_LIMIT_CAP = 57 * 1024 * 1024

HG_CHUNK = 32
EXP_CLAMP = 80.0
LOG2_E = 1.4426950408889634
ROW_CHUNK = 128
ROW_TILE = 1024


def _col_tile(n, preferred=512):
    t = preferred
    while n % t:
        t -= V7X_LANES
    return t


def _params(sem, nbytes):
    return pltpu.CompilerParams(dimension_semantics=sem, vmem_limit_bytes=int(min(V7X_VMEM_LIMIT_CAP, nbytes)))


class _Rows:
    def __init__(self, n_batch, ctx_len, seq_len, tm):
        assert (n_batch * ctx_len) % tm == 0 and seq_len % tm == 0
        self.n_batch, self.ctx_len, self.seq_len, self.tm = n_batch, ctx_len, seq_len, tm
        self.n_ctx_tiles = n_batch * ctx_len // tm
        self.tiles_per_seq = seq_len // tm
        self.n_lat_tiles = n_batch * self.tiles_per_seq
        self.n_tiles = self.n_ctx_tiles + self.n_lat_tiles

    def mod_row(self, tile):
        return jnp.where(tile < self.n_ctx_tiles, self.n_batch, (tile - self.n_ctx_tiles) // self.tiles_per_seq)


def _mod_spec(rows, first_tile, ncols, col_of_j):
    return pl.BlockSpec((None, 1, ncols), lambda i, j: (rows.mod_row(i + first_tile), 0, col_of_j(j)))


def _part_specs(parts, rows, first_tile, ncols, col_of_j):
    if len(parts) == 1:
        (_, off), = parts
        return [pl.BlockSpec((rows.tm, ncols), lambda i, j: (i + first_tile + off, col_of_j(j)))]
    assert first_tile == 0
    (_, off_c), (_, off_l) = parts
    nct = rows.n_ctx_tiles
    return [pl.BlockSpec((rows.tm, ncols), lambda i, j: (jnp.minimum(i, nct - 1) + off_c, col_of_j(j))),
            pl.BlockSpec((rows.tm, ncols), lambda i, j: (jnp.maximum(i - nct, 0) + off_l, col_of_j(j)))]


def _with_part(refs, n_ctx_tiles, fn):
    if len(refs) == 1:
        fn(refs[0])
        return
    is_ctx = pl.program_id(0) < n_ctx_tiles
    pl.when(is_ctx)(lambda: fn(refs[0]))
    pl.when(jnp.logical_not(is_ctx))(lambda: fn(refs[1]))


def _for_row_chunks(tm, fn):
    def body(r, _):
        fn(pl.ds(pl.multiple_of(r * ROW_CHUNK, ROW_CHUNK), ROW_CHUNK))
        return 0

    lax.fori_loop(0, tm // ROW_CHUNK, body, 0, unroll=4)


def _rms_modulate_into(h_ref, inv_ref, x_ref, g_ref, sh_ref, sc_ref, tm):
    def stats(rows):
        x = x_ref[rows, :]
        inv_ref[rows, :] = lax.rsqrt(jnp.mean(x * x, axis=-1, keepdims=True) + EPS)

    _for_row_chunks(tm, stats)
    gain = g_ref[...] * (1.0 + sc_ref[...])
    shift = sh_ref[...]

    def affine(rows):
        h_ref[rows, :] = (x_ref[rows, :] * inv_ref[rows, :] * gain + shift).astype(h_ref.dtype)

    _for_row_chunks(tm, affine)


def _ada_kernel(c_ref, w_ref, b_ref, o_ref):
    c = c_ref[...]
    s = (c * jax.nn.sigmoid(c)).astype(BF16)
    o_ref[...] = jnp.dot(s, w_ref[...].astype(BF16), preferred_element_type=F32) + b_ref[...]


def _ada_call(cvec, w_ada, b_ada):
    depth, d, n = w_ada.shape
    r = cvec.shape[0]
    tn = _col_tile(n)
    est = 2 * (d * tn * 4) + d * tn * 2 + 4 * r * (d + 2 * tn) * 4 + (4 << 20)
    return pl.pallas_call(
        _ada_kernel,
        grid=(depth, n // tn),
        in_specs=[
            pl.BlockSpec((r, d), lambda l, j: (0, 0)),
            pl.BlockSpec((None, d, tn), lambda l, j: (l, 0, j)),
            pl.BlockSpec((None, 1, tn), lambda l, j: (l, 0, j)),
        ],
        out_specs=pl.BlockSpec((None, r, tn), lambda l, j: (l, 0, j)),
        out_shape=jax.ShapeDtypeStruct((depth, r, n), F32),
        compiler_params=_params(("parallel", "parallel"), est),
        name="ada_mod",
    )(cvec, w_ada, b_ada.reshape(depth, 1, n))


def _proj_kernel(*refs, n_x, tm, n_ctx_tiles):
    x_refs = refs[:n_x]
    g_ref, sh_ref, sc_ref, w_ref, o_ref, h_ref, inv_ref = refs[n_x:]

    @pl.when(pl.program_id(1) == 0)
    def _():
        _with_part(x_refs, n_ctx_tiles, lambda x_ref: _rms_modulate_into(h_ref, inv_ref, x_ref, g_ref, sh_ref, sc_ref, tm))

    o_ref[...] = jnp.dot(h_ref[...], w_ref[...], preferred_element_type=F32)


def _proj_call(x_parts, g, shift, scale, w, layer, rows):
    d = x_parts[0][0].shape[1]
    n = w.shape[2]
    nx = len(x_parts)
    tm, tn = rows.tm, _col_tile(n, 1024 if nx == 1 else 512)
    est = 2 * nx * tm * d * 4 + tm * d * 2 + 2 * d * tn * 2 + 3 * tm * tn * 4 + 8 * ROW_CHUNK * d * 4 + (3 << 20)
    kern = functools.partial(_proj_kernel, n_x=nx, tm=tm, n_ctx_tiles=rows.n_ctx_tiles)
    full = lambda j: 0
    return pl.pallas_call(
        kern,
        grid=(rows.n_tiles, n // tn),
        in_specs=[*_part_specs(x_parts, rows, 0, d, full),
                  pl.BlockSpec((1, d), lambda i, j: (0, 0)),
                  _mod_spec(rows, 0, d, full), _mod_spec(rows, 0, d, full),
                  pl.BlockSpec((None, d, tn), lambda i, j: (layer, 0, j))],
        out_specs=pl.BlockSpec((tm, tn), lambda i, j: (i, j)),
        out_shape=jax.ShapeDtypeStruct((rows.n_tiles * tm, n), F32),
        scratch_shapes=[pltpu.VMEM((tm, d), BF16), pltpu.VMEM((tm, 1), F32)],
        compiler_params=_params(("parallel", "arbitrary"), est),
        name="norm_mod_proj",
    )(*[a for a, _ in x_parts], g, shift, scale, w)


def _outproj_kernel(*refs, n_a, n_x, n_ctx_tiles):
    a_refs, x_refs = refs[:n_a], refs[n_a:n_a + n_x]
    w_ref, ga_ref, o_ref = refs[n_a + n_x:]

    def with_a(a_ref):
        acc = jnp.dot(a_ref[...], w_ref[...], preferred_element_type=F32) * ga_ref[...]

        def with_x(x_ref):
            o_ref[...] = x_ref[...] + acc

        _with_part(x_refs, n_ctx_tiles, with_x)

    _with_part(a_refs, n_ctx_tiles, with_a)


def _outproj_call(a_parts, w, layer, x_parts, gate, rows, first_tile):
    k, n = w.shape[1], w.shape[2]
    tm, tn = rows.tm, _col_tile(n, 1024)
    na, nx = len(a_parts), len(x_parts)
    n_tiles = rows.n_tiles - first_tile
    est = 2 * na * tm * k * 2 + 2 * k * tn * 2 + 2 * (nx + 1) * tm * tn * 4 + 3 * tm * tn * 4 + (3 << 20)
    kern = functools.partial(_outproj_kernel, n_a=na, n_x=nx, n_ctx_tiles=rows.n_ctx_tiles)
    return pl.pallas_call(
        kern,
        grid=(n_tiles, n // tn),
        in_specs=[*_part_specs(a_parts, rows, first_tile, k, lambda j: 0),
                  *_part_specs(x_parts, rows, first_tile, tn, lambda j: j),
                  pl.BlockSpec((None, k, tn), lambda i, j: (layer, 0, j)),
                  _mod_spec(rows, first_tile, tn, lambda j: j)],
        out_specs=pl.BlockSpec((tm, tn), lambda i, j: (i, j)),
        out_shape=jax.ShapeDtypeStruct((n_tiles * tm, n), F32),
        compiler_params=_params(("parallel", "parallel"), est),
        name="outproj_residual",
    )(*[a for a, _ in a_parts], *[a for a, _ in x_parts], w, gate)


def _ffn_kernel(*refs, tm, final_norm):
    if final_norm:
        x_ref, g_ref, sh_ref, sc_ref, ga_ref, wg_ref, wu_ref, wo_ref, gf_ref, o_ref, h_ref, inv_ref = refs
    else:
        x_ref, g_ref, sh_ref, sc_ref, ga_ref, wg_ref, wu_ref, wo_ref, o_ref, h_ref, inv_ref = refs
    f = pl.program_id(1)

    @pl.when(f == 0)
    def _():
        _rms_modulate_into(h_ref, inv_ref, x_ref, g_ref, sh_ref, sc_ref, tm)
        o_ref[...] = jnp.zeros_like(o_ref)

    tn = _col_tile(o_ref.shape[1])
    half = tm // 2
    for r0 in range(0, tm, half):
        h = h_ref[r0:r0 + half, :]
        gate = jnp.dot(h, wg_ref[...], preferred_element_type=F32)
        up = jnp.dot(h, wu_ref[...], preferred_element_type=F32)
        act = (gate * jax.nn.sigmoid(gate) * up).astype(BF16)
        for c0 in range(0, o_ref.shape[1], tn):
            o_ref[r0:r0 + half, c0:c0 + tn] += jnp.dot(act, wo_ref[:, c0:c0 + tn], preferred_element_type=F32)

    @pl.when(f == pl.num_programs(1) - 1)
    def _():
        res_gate = ga_ref[...]

        def chunk(rows):
            y = x_ref[rows, :] + res_gate * o_ref[rows, :]
            if final_norm:
                y = y * lax.rsqrt(jnp.mean(y * y, axis=-1, keepdims=True) + EPS) * gf_ref[...]
            o_ref[rows, :] = y

        _for_row_chunks(tm, chunk)


def _ffn_call(xs, x_first_tile, g, shift, scale, res_gate, w_in, w_out, layer, rows, first_tile, final_g=None):
    d = xs.shape[1]
    ff = w_out.shape[1]
    tm, tf = rows.tm, _col_tile(ff)
    nf = ff // tf
    n_tiles = rows.n_tiles - first_tile
    est = (4 * tm * d * 4 + tm * d * 2 + 2 * (2 * d * tf * 2 + tf * d * 2) + 2 * tm * tf * 4 + tm * tf * 2
           + 8 * ROW_CHUNK * d * 4 + (3 << 20))
    kern = functools.partial(_ffn_kernel, tm=tm, final_norm=final_g is not None)
    full = lambda f: 0
    extra_specs = [] if final_g is None else [pl.BlockSpec((1, d), lambda i, f: (0, 0))]
    extra_args = [] if final_g is None else [final_g]
    return pl.pallas_call(
        kern,
        grid=(n_tiles, nf),
        in_specs=[pl.BlockSpec((tm, d), lambda i, f: (i + x_first_tile, 0)),
                  pl.BlockSpec((1, d), lambda i, f: (0, 0)),
                  _mod_spec(rows, first_tile, d, full), _mod_spec(rows, first_tile, d, full),
                  _mod_spec(rows, first_tile, d, full),
                  pl.BlockSpec((None, d, tf), lambda i, f: (layer, 0, f)),
                  pl.BlockSpec((None, d, tf), lambda i, f: (layer, 0, nf + f)),
                  pl.BlockSpec((None, tf, d), lambda i, f: (layer, f, 0)),
                  *extra_specs],
        out_specs=pl.BlockSpec((tm, d), lambda i, f: (i, 0)),
        out_shape=jax.ShapeDtypeStruct((n_tiles * tm, d), F32),
        scratch_shapes=[pltpu.VMEM((tm, d), BF16), pltpu.VMEM((tm, 1), F32)],
        compiler_params=_params(("parallel", "arbitrary"), est),
        name="swiglu_ffn",
    )(xs, g, shift, scale, res_gate, w_in, w_in, w_out, *extra_args)


def _softplus(x):
    return jnp.maximum(x, 0.0) + jnp.log1p(jnp.exp(-jnp.abs(x)))


def _chunk_carries(h_loc, p_loc, h_init, reverse):
    n = h_loc.shape[0]
    carries = [None] * n
    c = h_init
    for s in (reversed(range(n)) if reverse else range(n)):
        carries[s] = c
        c = p_loc[s:s + 1] * c + h_loc[s:s + 1]
    return carries, c


def _strided_scan(coef, res, n, stride):
    nsub = V7X_SUBLANES
    af_ref, bf_ref, ab_ref, bb_ref = coef
    pf_ref, hf_ref, pb_ref, hb_ref = res
    lanes = af_ref.shape[1]

    def body(i, carry):
        hf, pf, hb, pb = carry
        rows_f = pl.ds(i, nsub, stride=stride)
        rows_b = pl.ds(n - 1 - i, nsub, stride=stride)
        a = af_ref[rows_f, :]
        hf = a * hf + bf_ref[rows_f, :]
        pf = a * pf
        hf_ref[rows_f, :] = hf
        pf_ref[rows_f, :] = pf
        a = ab_ref[rows_b, :]
        hb = a * hb + bb_ref[rows_b, :]
        pb = a * pb
        hb_ref[rows_b, :] = hb
        pb_ref[rows_b, :] = pb
        return hf, pf, hb, pb

    zero = jnp.zeros((nsub, lanes), F32)
    one = jnp.ones((nsub, lanes), F32)
    return lax.fori_loop(0, n, body, (zero, one, zero, one), unroll=8)


def _rg_kernel(*refs, ctx_len, seq_len, need_ctx):
    xc_ref, xl_ref, gc_ref, gl_ref, cw_ref, cb_ref, wg_ref, bg_ref, lam_ref = refs[:9]
    n_out = 2 if need_ctx else 1
    y_ref = refs[9]
    yc_ref = refs[10] if need_ctx else None
    (xp_ref, laf, lbf, lab, lbb, lpf, lhf, lpb, lhb, caf, cbf, cab, cbb, cpf, chf, cpb, chb, carry_ref) = refs[9 + n_out:]
    nsub = V7X_SUBLANES
    lanes = xc_ref.shape[1]
    rb = ctx_len
    nblk = 1 + seq_len // rb
    halo = nsub
    lat_stride = rb + nsub
    ctx_n = ctx_len // nsub
    ctx_stride = ctx_n + nsub

    xp_ref[pl.ds(0, halo), :] = jnp.zeros((halo, lanes), F32)
    xp_ref[pl.ds(halo + ctx_len + seq_len, halo), :] = jnp.zeros((halo, lanes), F32)
    xp_ref[pl.ds(halo, ctx_len), :] = xc_ref[...]
    xp_ref[pl.ds(halo + ctx_len, seq_len), :] = xl_ref[...]

    cw = cw_ref[...]
    cb = cb_ref[...]
    softplus_neg_lam = _softplus(-lam_ref[...])
    local_row = lax.broadcasted_iota(jnp.int32, (rb, 1), 0)

    def coeffs(blk):
        start = blk * rb if isinstance(blk, int) else pl.multiple_of(blk * rb, rb)
        win = xp_ref[pl.ds(start, rb + 2 * halo), :]
        seg_first = blk <= 1
        seg_last = jnp.logical_or(blk == 0, blk == nblk - 1)
        taps = []
        for k in range(CONV_TAPS):
            off = k - CONV_LEFT
            tap = win[halo + off:halo + off + rb]
            if off < 0:
                tap = jnp.where(jnp.logical_or(local_row >= -off, jnp.logical_not(seg_first)), tap, 0.0)
            elif off > 0:
                tap = jnp.where(jnp.logical_or(local_row < rb - off, jnp.logical_not(seg_last)), tap, 0.0)
            taps.append(tap)
        xc = cb
        for k in range(CONV_TAPS):
            xc = xc + cw[k:k + 1] * taps[k]
        half_z = jnp.dot(xc.astype(BF16), wg_ref[...], preferred_element_type=F32) + bg_ref[...]
        half_xc = 0.5 * xc
        out = []
        for d in range(2):
            th_r = jnp.tanh(half_z[:, (2 * d) * lanes:(2 * d + 1) * lanes])
            th_i = jnp.tanh(half_z[:, (2 * d + 1) * lanes:(2 * d + 2) * lanes])
            scale = (-0.5 * RG_C) * softplus_neg_lam[d:d + 1]
            log_a = scale * th_r + scale
            a = jnp.exp(log_a)
            t = jnp.tanh(log_a)
            u = (-2.0) * t / (1.0 - t)
            mult = jnp.where(u > 0.0, u * lax.rsqrt(u), 0.0)
            out.append((a, mult * half_xc * (1.0 + th_i)))
        return out

    (a_f, b_f), (a_b, b_b) = coeffs(0)
    for c in range(nsub):
        dst = pl.ds(c * ctx_stride, ctx_n)
        src = slice(c * ctx_n, (c + 1) * ctx_n)
        caf[dst, :] = a_f[src]
        cbf[dst, :] = b_f[src]
        cab[dst, :] = a_b[src]
        cbb[dst, :] = b_b[src]

    def fill(blk, _):
        (a_f, b_f), (a_b, b_b) = coeffs(blk)
        dst = pl.ds(pl.multiple_of((blk - 1) * lat_stride, nsub), rb)
        laf[dst, :] = a_f
        lbf[dst, :] = b_f
        lab[dst, :] = a_b
        lbb[dst, :] = b_b
        return 0

    lax.fori_loop(1, nblk, fill, 0, unroll=4)

    zero_state = jnp.zeros((1, lanes), F32)
    hf, pf, hb, pb = _strided_scan((caf, cbf, cab, cbb), (cpf, chf, cpb, chb), ctx_n, ctx_stride)
    ctx_in_f, ctx_state_f = _chunk_carries(hf, pf, zero_state, reverse=False)
    ctx_in_b, ctx_state_b = _chunk_carries(hb, pb, zero_state, reverse=True)
    hf, pf, hb, pb = _strided_scan((laf, lbf, lab, lbb), (lpf, lhf, lpb, lhb), rb, lat_stride)
    lat_in_f, _ = _chunk_carries(hf, pf, ctx_state_f, reverse=False)
    lat_in_b, _ = _chunk_carries(hb, pb, ctx_state_b, reverse=True)
    for s in range(nsub):
        carry_ref[pl.ds(s, 1), :] = lat_in_f[s]
        carry_ref[pl.ds(nsub + s, 1), :] = lat_in_b[s]

    if need_ctx:
        for c in range(nsub):
            src = pl.ds(c * ctx_stride, ctx_n)
            h = chf[src, :] + cpf[src, :] * ctx_in_f[c] + chb[src, :] + cpb[src, :] * ctx_in_b[c]
            rows = pl.ds(c * ctx_n, ctx_n)
            yc_ref[rows, :] = (h * jax.nn.gelu(gc_ref[rows, :])).astype(yc_ref.dtype)

    def emit(blk, _):
        src = pl.ds(pl.multiple_of((blk - 1) * lat_stride, nsub), rb)
        cf = carry_ref[pl.ds(blk - 1, 1), :]
        cbw = carry_ref[pl.ds(nsub + blk - 1, 1), :]
        h = lhf[src, :] + lpf[src, :] * cf + lhb[src, :] + lpb[src, :] * cbw
        rows = pl.ds(pl.multiple_of((blk - 1) * rb, rb), rb)
        y_ref[rows, :] = (h * jax.nn.gelu(gl_ref[rows, :])).astype(y_ref.dtype)
        return 0

    lax.fori_loop(1, nblk, emit, 0, unroll=4)


def _mixer_out(t, w, n_batch, ctx_len, seq_len, lanes, need_ctx):
    specs = [pl.BlockSpec((seq_len, lanes), lambda b, h: (b, h))]
    shapes = [jax.ShapeDtypeStruct((n_batch * seq_len, w), BF16)]
    if need_ctx:
        specs.append(pl.BlockSpec((ctx_len, lanes), lambda b, h: (b, h)))
        shapes.append(jax.ShapeDtypeStruct((n_batch * ctx_len, w), BF16))
    return specs, shapes


def _rg_call(p, conv_w, conv_b, w_gates, b_gates, lam, *, n_batch, ctx_len, seq_len, need_ctx):
    t, w2 = p.shape
    w = w2 // 2
    lanes = V7X_LANES
    nh = w // lanes
    nsub = V7X_SUBLANES
    lb = ctx_len + seq_len
    lat_block0 = n_batch * ctx_len // seq_len
    lat_rows = nsub * (seq_len // nsub + nsub)
    ctx_rows = nsub * (ctx_len // nsub + nsub)
    est = (4 * lb * lanes * 4 + 2 * lb * lanes * 2 + (lb + 2 * nsub) * lanes * 4 + 8 * (lat_rows + ctx_rows) * lanes * 4
           + 48 * ctx_len * lanes * 4 + (4 << 20))
    kern = functools.partial(_rg_kernel, ctx_len=ctx_len, seq_len=seq_len, need_ctx=need_ctx)
    out_specs, out_shapes = _mixer_out(t, w, n_batch, ctx_len, seq_len, lanes, need_ctx)
    outs = pl.pallas_call(
        kern,
        grid=(n_batch, nh),
        in_specs=[pl.BlockSpec((ctx_len, lanes), lambda b, h: (b, h)),
                  pl.BlockSpec((seq_len, lanes), lambda b, h: (lat_block0 + b, h)),
                  pl.BlockSpec((ctx_len, lanes), lambda b, h: (b, nh + h)),
                  pl.BlockSpec((seq_len, lanes), lambda b, h: (lat_block0 + b, nh + h)),
                  pl.BlockSpec((CONV_TAPS, lanes), lambda b, h: (0, h)),
                  pl.BlockSpec((1, lanes), lambda b, h: (0, h)),
                  pl.BlockSpec((None, lanes, 4 * lanes), lambda b, h: (h, 0, 0)),
                  pl.BlockSpec((None, 1, 4 * lanes), lambda b, h: (h, 0, 0)),
                  pl.BlockSpec((2, lanes), lambda b, h: (0, h))],
        out_specs=out_specs,
        out_shape=out_shapes,
        scratch_shapes=[pltpu.VMEM((lb + 2 * nsub, lanes), F32)]
        + [pltpu.VMEM((lat_rows, lanes), F32)] * 8
        + [pltpu.VMEM((ctx_rows, lanes), F32)] * 8
        + [pltpu.VMEM((2 * nsub, lanes), F32)],
        compiler_params=_params(("parallel", "parallel"), est),
        name="rglru_mixer",
    )(p, p, p, p, conv_w, conv_b, w_gates, b_gates, lam)
    return outs if need_ctx else (outs[0], None)


def _group_cumsum(x, group, reverse):
    n = x.shape[0]
    pos = lax.broadcasted_iota(jnp.int32, (n, 1), 0) % group
    step = 1
    while step < group:
        if reverse:
            shifted = pltpu.roll(x, n - step, axis=0)
            keep = pos < group - step
        else:
            shifted = pltpu.roll(x, step, axis=0)
            keep = pos >= step
        x = x + jnp.where(keep, shifted, 0.0)
        step *= 2
    return x


def _hg_kernel(*refs, ctx_len, seq_len, layer, dk, unroll, need_ctx):
    in_refs = refs[:10]
    lb_ref, nw_ref = refs[10:12]
    n_out = 2 if need_ctx else 1
    y_ref = refs[12]
    yc_ref = refs[13] if need_ctx else None
    (pitch, qs, fs, bs, vs, qd_f, qd_b, oacc, kv_f, kv_b, st_f, st_b,
     k_slow, cum_slow, o_slow, deepest_ref) = refs[12 + n_out:]
    lanes = qs.shape[1]
    ch = HG_CHUNK
    lb_rows = ctx_len + seq_len
    nchunk = lb_rows // ch
    ctx_chunks = ctx_len // ch
    sb = ctx_len
    nsb = lb_rows // sb
    cps = sb // ch
    ncol = seq_len // ch
    grid_pitch = ncol + V7X_SUBLANES
    qd_refs, kv_refs, st_refs = (qd_f, qd_b), (kv_f, kv_b), (st_f, st_b)

    def to_scan_order(ctx_src, lat_src, dst_ref):
        dst_ref[pl.ds(0, ctx_len), :] = ctx_src[...]

        def spread(r, _):
            src = pl.ds(pl.multiple_of(r * ncol, ncol), ncol)
            pitch[pl.ds(pl.multiple_of(r * grid_pitch, V7X_SUBLANES), ncol), :] = lat_src[src, :]
            return 0

        lax.fori_loop(0, ch, spread, 0, unroll=True)

        def column(n, _):
            dst = pl.ds(pl.multiple_of(ctx_len + n * ch, ch), ch)
            dst_ref[dst, :] = pitch[pl.ds(n, ch, stride=grid_pitch), :]
            return 0

        lax.fori_loop(0, ncol, column, 0, unroll=True)

    for k, dst_ref in enumerate((qs, fs, bs, vs)):
        to_scan_order(in_refs[2 * k], in_refs[2 * k + 1], dst_ref)
    gc_ref, gl_ref = in_refs[8:10]

    lbp = lb_ref[...]
    e = jnp.exp(lbp - jnp.max(lbp, axis=0, keepdims=True))
    sm = e / jnp.sum(e, axis=0, keepdims=True)
    lower = jnp.zeros(sm.shape[1:], F32)
    for l in range(1, layer + 1):
        lower = lower + sm[l]

    q_scale = dk ** -0.5
    j_idx = lax.broadcasted_iota(jnp.int32, (cps, ch, ch), 1)
    s_idx = lax.broadcasted_iota(jnp.int32, (cps, ch, ch), 2)
    causal = (j_idx >= s_idx, j_idx <= s_idx)
    aux = 2 * V7X_SUBLANES
    aux_row = lax.broadcasted_iota(jnp.int32, (1, aux, 1), 1)

    def bf16_pieces(x):
        hi = x.astype(BF16)
        rest = x - hi.astype(F32)
        mid = rest.astype(BF16)
        lo = (rest - mid.astype(F32)).astype(BF16)
        return hi, mid, lo

    def gate_feats(rows, d):
        lo = lower[d:d + 1]
        half_open = 0.5 * (1.0 - lo)
        th = jnp.tanh((bs if d else fs)[rows, :])
        log_f = jnp.log(lo + half_open * (1.0 + th))
        k = half_open * (1.0 - th)
        return k, _group_cumsum(log_f, ch, reverse=bool(d))

    def exact_intra(rows):
        q3 = (qs[rows, :] * q_scale).reshape(cps, ch, lanes)
        v3 = vs[rows, :].reshape(cps, ch, lanes)
        pos = lax.broadcasted_iota(jnp.int32, (1, ch, 1), 1)
        for d in range(2):
            k, cum = gate_feats(rows, d)
            k_slow[...] = k.reshape(cps, ch, lanes)
            cum_slow[...] = cum.reshape(cps, ch, lanes)

            def row_j(j, _):
                cum3 = cum_slow[...]
                w = jnp.exp(jnp.minimum(cum_slow[:, pl.ds(j, 1), :] - cum3, 0.0))
                valid = (pos >= j) if d else (pos <= j)
                q_j = jnp.where(pos == j, q3, 0.0).sum(axis=1, keepdims=True)
                col = jnp.sum(jnp.where(valid, q_j * k_slow[...] * w, 0.0), axis=-1, keepdims=True)
                o_j = jnp.sum(col * v3, axis=1, keepdims=True)
                if d:
                    o_slow[:, pl.ds(j, 1), :] += o_j
                else:
                    o_slow[:, pl.ds(j, 1), :] = o_j
                return 0

            lax.fori_loop(0, ch, row_j, 0)
        oacc[rows, :] = o_slow[...].reshape(sb, lanes)

    def local_part(i, _):
        rows = pl.ds(pl.multiple_of(i * sb, sb), sb)
        q = qs[rows, :] * q_scale
        v3 = vs[rows, :].reshape(cps, ch, lanes).astype(BF16)
        zeros_v = jnp.zeros((cps, ch, lanes), BF16)
        pad_rows = jnp.concatenate([jnp.zeros((cps, aux, lanes), BF16), jnp.ones((cps, aux, lanes), BF16)], axis=2)
        kv_rhs = jnp.concatenate([jnp.concatenate([v3, zeros_v], axis=2), pad_rows], axis=1)
        o_sum = None
        deepest = None
        for d in range(2):
            k, cum = gate_feats(rows, d)
            cum_l = cum * LOG2_E
            cum_l3 = cum_l.reshape(cps, ch, lanes)
            total_l3 = cum_l3[:, 0:1, :] if d else cum_l3[:, ch - 1:ch, :]
            qd = (q * jnp.exp2(cum_l)).astype(BF16)
            kd = (k * jnp.exp2(jnp.minimum(cum * (-LOG2_E), EXP_CLAMP * LOG2_E))).astype(BF16)
            kl = (k.reshape(cps, ch, lanes) * jnp.exp2(total_l3 - cum_l3)).astype(BF16)
            att = jnp.einsum("cjd,csd->cjs", qd.reshape(cps, ch, lanes), kd.reshape(cps, ch, lanes),
                             preferred_element_type=F32)
            att = jnp.where(causal[d], att, 0.0)
            o_intra = jnp.einsum("cjs,cse->cje", att.astype(BF16), v3, preferred_element_type=F32)
            o_sum = o_intra if o_sum is None else o_sum + o_intra
            qd_refs[d][rows, :] = qd
            d_hi, d_mid, d_lo = [p.astype(F32) for p in bf16_pieces(jnp.exp2(total_l3))]
            dec_rows = jnp.where(aux_row == 0, d_hi, jnp.where(aux_row == 1, d_mid, jnp.where(aux_row == 2, d_lo, 0.0)))
            kv_lhs = jnp.concatenate([kl, dec_rows.astype(BF16)], axis=1)
            for c in range(cps):
                kv_refs[d][i * cps + c] = lax.dot_general(kv_lhs[c], kv_rhs[c], (((0,), (0,)), ((), ())),
                                                          preferred_element_type=F32)
            low = jnp.min(total_l3, axis=0)
            deepest = low if deepest is None else jnp.minimum(deepest, low)
        oacc[rows, :] = o_sum.reshape(sb, lanes)
        deepest_ref[i] = jnp.min(deepest)
        return 0

    lax.fori_loop(0, nsb, local_part, 0, unroll=unroll)

    def redo_unsafe(i, _):
        @pl.when(deepest_ref[i] < -EXP_CLAMP * LOG2_E)
        def _():
            exact_intra(pl.ds(pl.multiple_of(i * sb, sb), sb))

        return 0

    lax.fori_loop(0, nsb, redo_unsafe, 0)

    def state_step(d, n, st):
        st_refs[d][n] = st.astype(BF16)
        kv = kv_refs[d][n]
        return kv[:, lanes:] * st + kv[:, :lanes]

    zero_state = jnp.zeros((lanes, lanes), F32)
    lax.fori_loop(0, nchunk, functools.partial(state_step, 0), zero_state, unroll=4)
    st = lax.fori_loop(0, ctx_chunks, lambda i, s: state_step(1, ctx_chunks - 1 - i, s), zero_state, unroll=4)
    lax.fori_loop(0, nchunk - ctx_chunks, lambda i, s: state_step(1, nchunk - 1 - i, s), st, unroll=4)

    nw = nw_ref[...]

    def inter_part(i, _):
        rows = pl.ds(pl.multiple_of(i * sb, sb), sb)
        o = oacc[rows, :]
        for d in range(2):
            qd3 = qd_refs[d][rows, :].reshape(cps, ch, lanes)
            st3 = st_refs[d][pl.ds(i * cps, cps)]
            o = o + jnp.einsum("cjd,cde->cje", qd3, st3, preferred_element_type=F32).reshape(sb, lanes)
        oacc[rows, :] = o * lax.rsqrt(jnp.mean(o * o, axis=-1, keepdims=True) + EPS) * nw
        return 0

    first_sb = 0 if need_ctx else ctx_len // sb
    lax.fori_loop(first_sb, nsb, inter_part, 0, unroll=True)

    def gated(o, g):
        return (o * (g * jax.nn.sigmoid(g))).astype(y_ref.dtype)

    if need_ctx:
        yc_ref[...] = gated(oacc[pl.ds(0, ctx_len), :], gc_ref[...])

    def column(n, _):
        src = pl.ds(pl.multiple_of(ctx_len + n * ch, ch), ch)
        pitch[pl.ds(n, ch, stride=grid_pitch), :] = oacc[src, :]
        return 0

    lax.fori_loop(0, ncol, column, 0, unroll=True)

    def gather_row(r, _):
        dst = pl.ds(pl.multiple_of(r * ncol, ncol), ncol)
        y_ref[dst, :] = gated(pitch[pl.ds(pl.multiple_of(r * grid_pitch, V7X_SUBLANES), ncol), :], gl_ref[dst, :])
        return 0

    lax.fori_loop(0, ch, gather_row, 0, unroll=True)


def _hg_call(p, hg_lb, norm_w, *, n_batch, ctx_len, seq_len, layer, need_ctx):
    t, w5 = p.shape
    w = w5 // 5
    lanes = V7X_LANES
    nh = w // lanes
    lb = ctx_len + seq_len
    lat_block0 = n_batch * ctx_len // seq_len
    nchunk = lb // HG_CHUNK
    depth = hg_lb.shape[0]
    blk_bytes = lb * lanes * 4
    cps = ctx_len // HG_CHUNK
    pitch_rows = HG_CHUNK * (seq_len // HG_CHUNK + V7X_SUBLANES)
    unroll = lb // ctx_len
    est = (10 * blk_bytes + 2 * lb * lanes * 2 + 8 * blk_bytes + pitch_rows * lanes * 4 + 2 * nchunk * lanes * lanes * 10
           + unroll * 40 * ctx_len * lanes * 4 + (4 << 20))
    kern = functools.partial(_hg_kernel, ctx_len=ctx_len, seq_len=seq_len, layer=layer, dk=lanes, unroll=unroll,
                             need_ctx=need_ctx)
    in_specs = []
    for k in range(5):
        in_specs.append(pl.BlockSpec((ctx_len, lanes), lambda b, h, k=k: (b, k * nh + h)))
        in_specs.append(pl.BlockSpec((seq_len, lanes), lambda b, h, k=k: (lat_block0 + b, k * nh + h)))
    out_specs, out_shapes = _mixer_out(t, w, n_batch, ctx_len, seq_len, lanes, need_ctx)
    outs = pl.pallas_call(
        kern,
        grid=(n_batch, nh),
        in_specs=in_specs + [pl.BlockSpec((depth, 2, lanes), lambda b, h: (0, 0, h)),
                             pl.BlockSpec((1, lanes), lambda b, h: (0, 0))],
        out_specs=out_specs,
        out_shape=out_shapes,
        scratch_shapes=[pltpu.VMEM((pitch_rows, lanes), F32)]
        + [pltpu.VMEM((lb, lanes), F32)] * 4
        + [pltpu.VMEM((lb, lanes), BF16)] * 2
        + [pltpu.VMEM((lb, lanes), F32)]
        + [pltpu.VMEM((nchunk, lanes, 2 * lanes), F32)] * 2
        + [pltpu.VMEM((nchunk, lanes, lanes), BF16)] * 2
        + [pltpu.VMEM((cps, HG_CHUNK, lanes), F32)] * 3
        + [pltpu.SMEM((lb // ctx_len,), F32)],
        compiler_params=_params(("parallel", "parallel"), est),
        name="hgrn2_mixer",
    )(*([p] * 10), hg_lb, norm_w)
    return outs if need_ctx else (outs[0], None)


def kernel(x, c, ctx, c_ctx, w_ada, b_ada, g_mix, g_ffn, g_final, w_ffn_in, w_ffn_out,
           rg_w_in, rg_conv_w, rg_conv_b, rg_w_a, rg_b_a, rg_w_i, rg_b_i, rg_lam, rg_w_out,
           hg_w_in, hg_lb, hg_norm, hg_w_out):
    n_batch, seq_len, d = x.shape
    ctx_len = ctx.shape[1]
    depth = w_ada.shape[0]
    lanes = V7X_LANES
    assert seq_len == V7X_SUBLANES * ctx_len, "row blocking assumes the latent is 8 context lengths long"
    assert seq_len // GRID_W == HG_CHUNK, "an HGRN2 chunk must be one latent grid column"
    assert (n_batch * ctx_len) % seq_len == 0, "latent blocks must stay block-aligned behind the context rows"
    assert d % lanes == 0 and rg_w_a.shape[-1] == lanes and hg_norm.shape[-1] == lanes
    dims = dict(n_batch=n_batch, ctx_len=ctx_len, seq_len=seq_len)
    rows = _Rows(n_batch, ctx_len, seq_len, ROW_TILE)
    nct = rows.n_ctx_tiles

    pad = (-(n_batch + 1)) % V7X_SUBLANES
    cvec = jnp.concatenate([c, c_ctx[None, :], jnp.zeros((pad, d), F32)], axis=0)
    mods = _ada_call(cvec, w_ada, b_ada).reshape(depth, cvec.shape[0], N_MOD, d)

    w_ffn_in_b, w_ffn_out_b = w_ffn_in.astype(BF16), w_ffn_out.astype(BF16)
    rg_w_in_b, rg_w_out_b = rg_w_in.astype(BF16), rg_w_out.astype(BF16)
    fifth = jnp.arange(hg_w_in.shape[-1]) // d
    hg_w_in_b = (hg_w_in * jnp.where((fifth == 1) | (fifth == 2), 0.5, 1.0)).astype(BF16)
    hg_w_out_b = hg_w_out.astype(BF16)

    x_parts = [(ctx.reshape(n_batch * ctx_len, d), 0), (x.reshape(n_batch * seq_len, d), 0)]
    out = None
    for i in range(depth):
        last = i == depth - 1
        sh1, sc1, ga1, sh2, sc2, ga2 = [mods[i, :, k][:, None, :] for k in range(N_MOD)]
        j = i // N_MIXERS
        if i % N_MIXERS == 0:
            p = _proj_call(x_parts, g_mix[i][None], sh1, sc1, rg_w_in_b, j, rows)
            w_gates = (0.5 * jnp.concatenate([rg_w_a[j, 0], rg_w_i[j, 0], rg_w_a[j, 1], rg_w_i[j, 1]], axis=-1)).astype(BF16)
            b_gates = 0.5 * jnp.concatenate([rg_b_a[j, 0], rg_b_i[j, 0], rg_b_a[j, 1], rg_b_i[j, 1]], axis=-1)[:, None, :]
            y, y_ctx = _rg_call(p, rg_conv_w[j], rg_conv_b[j][None], w_gates, b_gates, rg_lam[j], need_ctx=not last, **dims)
            w_out = rg_w_out_b
        else:
            p = _proj_call(x_parts, g_mix[i][None], sh1, sc1, hg_w_in_b, j, rows)
            y, y_ctx = _hg_call(p, hg_lb, hg_norm[j][None], layer=i, need_ctx=not last, **dims)
            w_out = hg_w_out_b
        if last:
            xs = _outproj_latent(y, w_out, j, x_parts, ga1, rows)
            out = _ffn_call(xs, 0, g_ffn[i][None], sh2, sc2, ga2, w_ffn_in_b, w_ffn_out_b, i, rows, nct,
                            final_g=g_final[None])
        else:
            xs = _outproj_call([(y_ctx, 0), (y, 0)], w_out, j, x_parts, ga1, rows, 0)
            xs = _ffn_call(xs, 0, g_ffn[i][None], sh2, sc2, ga2, w_ffn_in_b, w_ffn_out_b, i, rows, 0)
            x_parts = [(xs, 0)]
    return out.reshape(n_batch, seq_len, d)


def _outproj_latent(y, w_out, j, x_parts, gate, rows):
    nct = rows.n_ctx_tiles
    if len(x_parts) == 2:
        x_lat = [(x_parts[1][0], -nct)]
    else:
        x_lat = [(x_parts[0][0], 0)]
    return _outproj_call([(y, -nct)], w_out, j, x_lat, gate, rows, nct)
```

```python
import functools

import jax
import jax.numpy as jnp
from jax import lax
from jax.experimental import pallas as pl
from jax.experimental.pallas import tpu as pltpu

F32 = jnp.float32
BF16 = jnp.bfloat16

EPS = 1e-6
RG_C = 8.0
GRID_W = 64
N_MOD = 6
N_MIXERS = 2
CONV_TAPS = 4
CONV_LEFT = 2

V7X_LANES = 128
V7X_SUBLANES = 8
V7X_VMEM_LIMIT_CAP = 57 * 1024 * 1024

HG_CHUNK = 32
EXP_CLAMP = 80.0
LOG2_E = 1.4426950408889634
ROW_CHUNK = 128
ROW_TILE = 1024


def _col_tile(n, preferred=512):
    t = preferred
    while n % t:
        t -= V7X_LANES
    return t


def _params(sem, nbytes):
    return pltpu.CompilerParams(dimension_semantics=sem, vmem_limit_bytes=int(min(V7X_VMEM_LIMIT_CAP, nbytes)))


class _Rows:
    def __init__(self, n_batch, ctx_len, seq_len, tm):
        assert (n_batch * ctx_len) % tm == 0 and seq_len % tm == 0
        self.n_batch, self.ctx_len, self.seq_len, self.tm = n_batch, ctx_len, seq_len, tm
        self.n_ctx_tiles = n_batch * ctx_len // tm
        self.tiles_per_seq = seq_len // tm
        self.n_lat_tiles = n_batch * self.tiles_per_seq
        self.n_tiles = self.n_ctx_tiles + self.n_lat_tiles

    def mod_row(self, tile):
        return jnp.where(tile < self.n_ctx_tiles, self.n_batch, (tile - self.n_ctx_tiles) // self.tiles_per_seq)


def _mod_spec(rows, first_tile, ncols, col_of_j):
    return pl.BlockSpec((None, 1, ncols), lambda i, j: (rows.mod_row(i + first_tile), 0, col_of_j(j)))


def _part_specs(parts, rows, first_tile, ncols, col_of_j, single_buffer_ctx=False):
    if len(parts) == 1:
        (_, off), = parts
        return [pl.BlockSpec((rows.tm, ncols), lambda i, j: (i + first_tile + off, col_of_j(j)))]
    assert first_tile == 0
    (_, off_c), (_, off_l) = parts
    nct = rows.n_ctx_tiles
    ctx_mode = dict(pipeline_mode=pl.Buffered(1)) if single_buffer_ctx else {}
    return [pl.BlockSpec((rows.tm, ncols), lambda i, j: (jnp.minimum(i, nct - 1) + off_c, col_of_j(j)), **ctx_mode),
            pl.BlockSpec((rows.tm, ncols), lambda i, j: (jnp.maximum(i - nct, 0) + off_l, col_of_j(j)))]


def _with_part(refs, n_ctx_tiles, fn):
    if len(refs) == 1:
        fn(refs[0])
        return
    is_ctx = pl.program_id(0) < n_ctx_tiles
    pl.when(is_ctx)(lambda: fn(refs[0]))
    pl.when(jnp.logical_not(is_ctx))(lambda: fn(refs[1]))


def _for_row_chunks(tm, fn):
    def body(r, _):
        fn(pl.ds(pl.multiple_of(r * ROW_CHUNK, ROW_CHUNK), ROW_CHUNK))
        return 0

    lax.fori_loop(0, tm // ROW_CHUNK, body, 0, unroll=4)


def _rms_modulate_into(h_ref, inv_ref, x_ref, g_ref, sh_ref, sc_ref, tm):
    def stats(rows):
        x = x_ref[rows, :]
        inv_ref[rows, :] = lax.rsqrt(jnp.mean(x * x, axis=-1, keepdims=True) + EPS)

    _for_row_chunks(tm, stats)
    gain = g_ref[...] * (1.0 + sc_ref[...])
    shift = sh_ref[...]

    def affine(rows):
        h_ref[rows, :] = (x_ref[rows, :] * inv_ref[rows, :] * gain + shift).astype(h_ref.dtype)

    _for_row_chunks(tm, affine)


def _ada_kernel(c_ref, w_ref, b_ref, o_ref):
    c = c_ref[...]
    s = (c * jax.nn.sigmoid(c)).astype(BF16)
    o_ref[...] = jnp.dot(s, w_ref[...].astype(BF16), preferred_element_type=F32) + b_ref[...]


def _ada_call(cvec, w_ada, b_ada):
    depth, d, n = w_ada.shape
    r = cvec.shape[0]
    tn = _col_tile(n)
    est = 2 * (d * tn * 4) + d * tn * 2 + 4 * r * (d + 2 * tn) * 4 + (4 << 20)
    return pl.pallas_call(
        _ada_kernel,
        grid=(depth, n // tn),
        in_specs=[
            pl.BlockSpec((r, d), lambda l, j: (0, 0)),
            pl.BlockSpec((None, d, tn), lambda l, j: (l, 0, j)),
            pl.BlockSpec((None, 1, tn), lambda l, j: (l, 0, j)),
        ],
        out_specs=pl.BlockSpec((None, r, tn), lambda l, j: (l, 0, j)),
        out_shape=jax.ShapeDtypeStruct((depth, r, n), F32),
        compiler_params=_params(("parallel", "parallel"), est),
        name="ada_mod",
    )(cvec, w_ada, b_ada.reshape(depth, 1, n))


def _proj_kernel(*refs, n_x, tm, n_ctx_tiles):
    x_refs = refs[:n_x]
    g_ref, sh_ref, sc_ref, w_ref, o_ref, h_ref, inv_ref = refs[n_x:]

    @pl.when(pl.program_id(1) == 0)
    def _():
        _with_part(x_refs, n_ctx_tiles, lambda x_ref: _rms_modulate_into(h_ref, inv_ref, x_ref, g_ref, sh_ref, sc_ref, tm))

    o_ref[...] = jnp.dot(h_ref[...], w_ref[...], preferred_element_type=F32)


def _proj_call(x_parts, g, shift, scale, w, layer, rows):
    d = x_parts[0][0].shape[1]
    n = w.shape[2]
    nx = len(x_parts)
    tm, tn = rows.tm, _col_tile(n, 1024)
    x_bufs = 2 if nx == 1 else 3
    est = x_bufs * tm * d * 4 + tm * d * 2 + 2 * d * tn * 2 + 3 * tm * tn * 4 + 8 * ROW_CHUNK * d * 4 + (3 << 20)
    kern = functools.partial(_proj_kernel, n_x=nx, tm=tm, n_ctx_tiles=rows.n_ctx_tiles)
    full = lambda j: 0
    return pl.pallas_call(
        kern,
        grid=(rows.n_tiles, n // tn),
        in_specs=[*_part_specs(x_parts, rows, 0, d, full, single_buffer_ctx=True),
                  pl.BlockSpec((1, d), lambda i, j: (0, 0)),
                  _mod_spec(rows, 0, d, full), _mod_spec(rows, 0, d, full),
                  pl.BlockSpec((None, d, tn), lambda i, j: (layer, 0, j))],
        out_specs=pl.BlockSpec((tm, tn), lambda i, j: (i, j)),
        out_shape=jax.ShapeDtypeStruct((rows.n_tiles * tm, n), F32),
        scratch_shapes=[pltpu.VMEM((tm, d), BF16), pltpu.VMEM((tm, 1), F32)],
        compiler_params=_params(("parallel", "arbitrary"), est),
        name="norm_mod_proj",
    )(*[a for a, _ in x_parts], g, shift, scale, w)


def _outproj_kernel(*refs, n_a, n_x, n_ctx_tiles):
    a_refs, x_refs = refs[:n_a], refs[n_a:n_a + n_x]
    w_ref, ga_ref, o_ref = refs[n_a + n_x:]

    def with_a(a_ref):
        acc = jnp.dot(a_ref[...], w_ref[...], preferred_element_type=F32) * ga_ref[...]

        def with_x(x_ref):
            o_ref[...] = x_ref[...] + acc

        _with_part(x_refs, n_ctx_tiles, with_x)

    _with_part(a_refs, n_ctx_tiles, with_a)


def _outproj_call(a_parts, w, layer, x_parts, gate, rows, first_tile):
    k, n = w.shape[1], w.shape[2]
    tm, tn = rows.tm, _col_tile(n, 2048)
    na, nx = len(a_parts), len(x_parts)
    n_tiles = rows.n_tiles - first_tile
    est = 2 * na * tm * k * 2 + 2 * k * tn * 2 + 2 * (nx + 1) * tm * tn * 4 + 3 * tm * tn * 4 + (3 << 20)
    kern = functools.partial(_outproj_kernel, n_a=na, n_x=nx, n_ctx_tiles=rows.n_ctx_tiles)
    return pl.pallas_call(
        kern,
        grid=(n_tiles, n // tn),
        in_specs=[*_part_specs(a_parts, rows, first_tile, k, lambda j: 0),
                  *_part_specs(x_parts, rows, first_tile, tn, lambda j: j),
                  pl.BlockSpec((None, k, tn), lambda i, j: (layer, 0, j)),
                  _mod_spec(rows, first_tile, tn, lambda j: j)],
        out_specs=pl.BlockSpec((tm, tn), lambda i, j: (i, j)),
        out_shape=jax.ShapeDtypeStruct((n_tiles * tm, n), F32),
        compiler_params=_params(("parallel", "parallel"), est),
        name="outproj_residual",
    )(*[a for a, _ in a_parts], *[a for a, _ in x_parts], w, gate)


def _ffn_kernel(*refs, tm, final_norm):
    if final_norm:
        x_ref, g_ref, sh_ref, sc_ref, ga_ref, wg_ref, wu_ref, wo_ref, gf_ref, o_ref, h_ref, inv_ref = refs
    else:
        x_ref, g_ref, sh_ref, sc_ref, ga_ref, wg_ref, wu_ref, wo_ref, o_ref, h_ref, inv_ref = refs
    f = pl.program_id(1)

    @pl.when(f == 0)
    def _():
        _rms_modulate_into(h_ref, inv_ref, x_ref, g_ref, sh_ref, sc_ref, tm)
        o_ref[...] = jnp.zeros_like(o_ref)

    tn = _col_tile(o_ref.shape[1])
    half = tm // 2
    for r0 in range(0, tm, half):
        h = h_ref[r0:r0 + half, :]
        gate = jnp.dot(h, wg_ref[...], preferred_element_type=F32)
        up = jnp.dot(h, wu_ref[...], preferred_element_type=F32)
        act = (gate * jax.nn.sigmoid(gate) * up).astype(BF16)
        for c0 in range(0, o_ref.shape[1], tn):
            o_ref[r0:r0 + half, c0:c0 + tn] += jnp.dot(act, wo_ref[:, c0:c0 + tn], preferred_element_type=F32)

    @pl.when(f == pl.num_programs(1) - 1)
    def _():
        res_gate = ga_ref[...]

        def chunk(rows):
            y = x_ref[rows, :] + res_gate * o_ref[rows, :]
            if final_norm:
                y = y * lax.rsqrt(jnp.mean(y * y, axis=-1, keepdims=True) + EPS) * gf_ref[...]
            o_ref[rows, :] = y

        _for_row_chunks(tm, chunk)


def _ffn_call(xs, x_first_tile, g, shift, scale, res_gate, w_in, w_out, layer, rows, first_tile, final_g=None):
    d = xs.shape[1]
    ff = w_out.shape[1]
    tm, tf = rows.tm, _col_tile(ff)
    nf = ff // tf
    n_tiles = rows.n_tiles - first_tile
    est = (4 * tm * d * 4 + tm * d * 2 + 2 * (2 * d * tf * 2 + tf * d * 2) + 2 * tm * tf * 4 + tm * tf * 2
           + 8 * ROW_CHUNK * d * 4 + (3 << 20))
    kern = functools.partial(_ffn_kernel, tm=tm, final_norm=final_g is not None)
    full = lambda f: 0
    extra_specs = [] if final_g is None else [pl.BlockSpec((1, d), lambda i, f: (0, 0))]
    extra_args = [] if final_g is None else [final_g]
    return pl.pallas_call(
        kern,
        grid=(n_tiles, nf),
        in_specs=[pl.BlockSpec((tm, d), lambda i, f: (i + x_first_tile, 0)),
                  pl.BlockSpec((1, d), lambda i, f: (0, 0)),
                  _mod_spec(rows, first_tile, d, full), _mod_spec(rows, first_tile, d, full),
                  _mod_spec(rows, first_tile, d, full),
                  pl.BlockSpec((None, d, tf), lambda i, f: (layer, 0, f)),
                  pl.BlockSpec((None, d, tf), lambda i, f: (layer, 0, nf + f)),
                  pl.BlockSpec((None, tf, d), lambda i, f: (layer, f, 0)),
                  *extra_specs],
        out_specs=pl.BlockSpec((tm, d), lambda i, f: (i, 0)),
        out_shape=jax.ShapeDtypeStruct((n_tiles * tm, d), F32),
        scratch_shapes=[pltpu.VMEM((tm, d), BF16), pltpu.VMEM((tm, 1), F32)],
        compiler_params=_params(("parallel", "arbitrary"), est),
        name="swiglu_ffn",
    )(xs, g, shift, scale, res_gate, w_in, w_in, w_out, *extra_args)


def _softplus(x):
    return jnp.maximum(x, 0.0) + jnp.log1p(jnp.exp(-jnp.abs(x)))


def _chunk_carries(h_loc, p_loc, h_init, reverse):
    n = h_loc.shape[0]
    carries = [None] * n
    c = h_init
    for s in (reversed(range(n)) if reverse else range(n)):
        carries[s] = c
        c = p_loc[s:s + 1] * c + h_loc[s:s + 1]
    return carries, c


def _strided_scan(coef, res, n, stride):
    nsub = V7X_SUBLANES
    af_ref, bf_ref, ab_ref, bb_ref = coef
    pf_ref, hf_ref, pb_ref, hb_ref = res
    lanes = af_ref.shape[1]

    def body(i, carry):
        hf, pf, hb, pb = carry
        rows_f = pl.ds(i, nsub, stride=stride)
        rows_b = pl.ds(n - 1 - i, nsub, stride=stride)
        a = af_ref[rows_f, :]
        hf = a * hf + bf_ref[rows_f, :]
        pf = a * pf
        hf_ref[rows_f, :] = hf
        pf_ref[rows_f, :] = pf
        a = ab_ref[rows_b, :]
        hb = a * hb + bb_ref[rows_b, :]
        pb = a * pb
        hb_ref[rows_b, :] = hb
        pb_ref[rows_b, :] = pb
        return hf, pf, hb, pb

    zero = jnp.zeros((nsub, lanes), F32)
    one = jnp.ones((nsub, lanes), F32)
    return lax.fori_loop(0, n, body, (zero, one, zero, one), unroll=8)


def _rg_kernel(*refs, ctx_len, seq_len, need_ctx):
    xc_ref, xl_ref, gc_ref, gl_ref, cw_ref, cb_ref, wg_ref, bg_ref, lam_ref = refs[:9]
    n_out = 2 if need_ctx else 1
    y_ref = refs[9]
    yc_ref = refs[10] if need_ctx else None
    (xp_ref, laf, lbf, lab, lbb, lpf, lhf, lpb, lhb, caf, cbf, cab, cbb, cpf, chf, cpb, chb, carry_ref) = refs[9 + n_out:]
    nsub = V7X_SUBLANES
    lanes = xc_ref.shape[1]
    rb = ctx_len
    nblk = 1 + seq_len // rb
    halo = nsub
    lat_stride = rb + nsub
    ctx_n = ctx_len // nsub
    ctx_stride = ctx_n + nsub

    xp_ref[pl.ds(0, halo), :] = jnp.zeros((halo, lanes), F32)
    xp_ref[pl.ds(halo + ctx_len + seq_len, halo), :] = jnp.zeros((halo, lanes), F32)
    xp_ref[pl.ds(halo, ctx_len), :] = xc_ref[...]
    xp_ref[pl.ds(halo + ctx_len, seq_len), :] = xl_ref[...]

    cw = cw_ref[...]
    cb = cb_ref[...]
    softplus_neg_lam = _softplus(-lam_ref[...])
    local_row = lax.broadcasted_iota(jnp.int32, (rb, 1), 0)

    def coeffs(blk):
        start = blk * rb if isinstance(blk, int) else pl.multiple_of(blk * rb, rb)
        win = xp_ref[pl.ds(start, rb + 2 * halo), :]
        seg_first = blk <= 1
        seg_last = jnp.logical_or(blk == 0, blk == nblk - 1)
        taps = []
        for k in range(CONV_TAPS):
            off = k - CONV_LEFT
            tap = win[halo + off:halo + off + rb]
            if off < 0:
                tap = jnp.where(jnp.logical_or(local_row >= -off, jnp.logical_not(seg_first)), tap, 0.0)
            elif off > 0:
                tap = jnp.where(jnp.logical_or(local_row < rb - off, jnp.logical_not(seg_last)), tap, 0.0)
            taps.append(tap)
        xc = cb
        for k in range(CONV_TAPS):
            xc = xc + cw[k:k + 1] * taps[k]
        half_z = jnp.dot(xc.astype(BF16), wg_ref[...], preferred_element_type=F32) + bg_ref[...]
        half_xc = 0.5 * xc
        out = []
        for d in range(2):
            th_r = jnp.tanh(half_z[:, (2 * d) * lanes:(2 * d + 1) * lanes])
            th_i = jnp.tanh(half_z[:, (2 * d + 1) * lanes:(2 * d + 2) * lanes])
            scale = (-0.5 * RG_C) * softplus_neg_lam[d:d + 1]
            log_a = scale * th_r + scale
            a = jnp.exp(log_a)
            t = jnp.tanh(log_a)
            u = (-2.0) * t / (1.0 - t)
            mult = jnp.where(u > 0.0, u * lax.rsqrt(u), 0.0)
            out.append((a, mult * half_xc * (1.0 + th_i)))
        return out

    (a_f, b_f), (a_b, b_b) = coeffs(0)
    for c in range(nsub):
        dst = pl.ds(c * ctx_stride, ctx_n)
        src = slice(c * ctx_n, (c + 1) * ctx_n)
        caf[dst, :] = a_f[src]
        cbf[dst, :] = b_f[src]
        cab[dst, :] = a_b[src]
        cbb[dst, :] = b_b[src]

    def fill(blk, _):
        (a_f, b_f), (a_b, b_b) = coeffs(blk)
        dst = pl.ds(pl.multiple_of((blk - 1) * lat_stride, nsub), rb)
        laf[dst, :] = a_f
        lbf[dst, :] = b_f
        lab[dst, :] = a_b
        lbb[dst, :] = b_b
        return 0

    lax.fori_loop(1, nblk, fill, 0, unroll=4)

    zero_state = jnp.zeros((1, lanes), F32)
    hf, pf, hb, pb = _strided_scan((caf, cbf, cab, cbb), (cpf, chf, cpb, chb), ctx_n, ctx_stride)
    ctx_in_f, ctx_state_f = _chunk_carries(hf, pf, zero_state, reverse=False)
    ctx_in_b, ctx_state_b = _chunk_carries(hb, pb, zero_state, reverse=True)
    hf, pf, hb, pb = _strided_scan((laf, lbf, lab, lbb), (lpf, lhf, lpb, lhb), rb, lat_stride)
    lat_in_f, _ = _chunk_carries(hf, pf, ctx_state_f, reverse=False)
    lat_in_b, _ = _chunk_carries(hb, pb, ctx_state_b, reverse=True)
    for s in range(nsub):
        carry_ref[pl.ds(s, 1), :] = lat_in_f[s]
        carry_ref[pl.ds(nsub + s, 1), :] = lat_in_b[s]

    if need_ctx:
        for c in range(nsub):
            src = pl.ds(c * ctx_stride, ctx_n)
            h = chf[src, :] + cpf[src, :] * ctx_in_f[c] + chb[src, :] + cpb[src, :] * ctx_in_b[c]
            rows = pl.ds(c * ctx_n, ctx_n)
            yc_ref[rows, :] = (h * jax.nn.gelu(gc_ref[rows, :])).astype(yc_ref.dtype)

    def emit(blk, _):
        src = pl.ds(pl.multiple_of((blk - 1) * lat_stride, nsub), rb)
        cf = carry_ref[pl.ds(blk - 1, 1), :]
        cbw = carry_ref[pl.ds(nsub + blk - 1, 1), :]
        h = lhf[src, :] + lpf[src, :] * cf + lhb[src, :] + lpb[src, :] * cbw
        rows = pl.ds(pl.multiple_of((blk - 1) * rb, rb), rb)
        y_ref[rows, :] = (h * jax.nn.gelu(gl_ref[rows, :])).astype(y_ref.dtype)
        return 0

    lax.fori_loop(1, nblk, emit, 0, unroll=4)


def _mixer_out(t, w, n_batch, ctx_len, seq_len, lanes, need_ctx):
    specs = [pl.BlockSpec((seq_len, lanes), lambda b, h: (b, h))]
    shapes = [jax.ShapeDtypeStruct((n_batch * seq_len, w), BF16)]
    if need_ctx:
        specs.append(pl.BlockSpec((ctx_len, lanes), lambda b, h: (b, h)))
        shapes.append(jax.ShapeDtypeStruct((n_batch * ctx_len, w), BF16))
    return specs, shapes


def _rg_call(p, conv_w, conv_b, w_gates, b_gates, lam, *, n_batch, ctx_len, seq_len, need_ctx):
    t, w2 = p.shape
    w = w2 // 2
    lanes = V7X_LANES
    nh = w // lanes
    nsub = V7X_SUBLANES
    lb = ctx_len + seq_len
    lat_block0 = n_batch * ctx_len // seq_len
    lat_rows = nsub * (seq_len // nsub + nsub)
    ctx_rows = nsub * (ctx_len // nsub + nsub)
    est = (4 * lb * lanes * 4 + 2 * lb * lanes * 2 + (lb + 2 * nsub) * lanes * 4 + 8 * (lat_rows + ctx_rows) * lanes * 4
           + 48 * ctx_len * lanes * 4 + (4 << 20))
    kern = functools.partial(_rg_kernel, ctx_len=ctx_len, seq_len=seq_len, need_ctx=need_ctx)
    out_specs, out_shapes = _mixer_out(t, w, n_batch, ctx_len, seq_len, lanes, need_ctx)
    outs = pl.pallas_call(
        kern,
        grid=(n_batch, nh),
        in_specs=[pl.BlockSpec((ctx_len, lanes), lambda b, h: (b, h)),
                  pl.BlockSpec((seq_len, lanes), lambda b, h: (lat_block0 + b, h)),
                  pl.BlockSpec((ctx_len, lanes), lambda b, h: (b, nh + h)),
                  pl.BlockSpec((seq_len, lanes), lambda b, h: (lat_block0 + b, nh + h)),
                  pl.BlockSpec((CONV_TAPS, lanes), lambda b, h: (0, h)),
                  pl.BlockSpec((1, lanes), lambda b, h: (0, h)),
                  pl.BlockSpec((None, lanes, 4 * lanes), lambda b, h: (h, 0, 0)),
                  pl.BlockSpec((None, 1, 4 * lanes), lambda b, h: (h, 0, 0)),
                  pl.BlockSpec((2, lanes), lambda b, h: (0, h))],
        out_specs=out_specs,
        out_shape=out_shapes,
        scratch_shapes=[pltpu.VMEM((lb + 2 * nsub, lanes), F32)]
        + [pltpu.VMEM((lat_rows, lanes), F32)] * 8
        + [pltpu.VMEM((ctx_rows, lanes), F32)] * 8
        + [pltpu.VMEM((2 * nsub, lanes), F32)],
        compiler_params=_params(("parallel", "parallel"), est),
        name="rglru_mixer",
    )(p, p, p, p, conv_w, conv_b, w_gates, b_gates, lam)
    return outs if need_ctx else (outs[0], None)


def _group_cumsum(x, group, reverse):
    n = x.shape[0]
    pos = lax.broadcasted_iota(jnp.int32, (n, 1), 0) % group
    step = 1
    while step < group:
        if reverse:
            shifted = pltpu.roll(x, n - step, axis=0)
            keep = pos < group - step
        else:
            shifted = pltpu.roll(x, step, axis=0)
            keep = pos >= step
        x = x + jnp.where(keep, shifted, 0.0)
        step *= 2
    return x


def _hg_kernel(*refs, ctx_len, seq_len, layer, dk, unroll, need_ctx):
    in_refs = refs[:10]
    lb_ref, nw_ref = refs[10:12]
    n_out = 2 if need_ctx else 1
    y_ref = refs[12]
    yc_ref = refs[13] if need_ctx else None
    (pitch, qs, fs, bs, vs, qd_f, qd_b, oacc, kv_f, kv_b, st_f, st_b,
     k_slow, cum_slow, o_slow, deepest_ref) = refs[12 + n_out:]
    lanes = qs.shape[1]
    ch = HG_CHUNK
    lb_rows = ctx_len + seq_len
    nchunk = lb_rows // ch
    ctx_chunks = ctx_len // ch
    sb = ctx_len
    nsb = lb_rows // sb
    cps = sb // ch
    ncol = seq_len // ch
    grid_pitch = ncol + V7X_SUBLANES
    qd_refs, kv_refs, st_refs = (qd_f, qd_b), (kv_f, kv_b), (st_f, st_b)

    def to_scan_order(ctx_src, lat_src, dst_ref):
        dst_ref[pl.ds(0, ctx_len), :] = ctx_src[...]

        def spread(r, _):
            src = pl.ds(pl.multiple_of(r * ncol, ncol), ncol)
            pitch[pl.ds(pl.multiple_of(r * grid_pitch, V7X_SUBLANES), ncol), :] = lat_src[src, :]
            return 0

        lax.fori_loop(0, ch, spread, 0, unroll=True)

        def column(n, _):
            dst = pl.ds(pl.multiple_of(ctx_len + n * ch, ch), ch)
            dst_ref[dst, :] = pitch[pl.ds(n, ch, stride=grid_pitch), :]
            return 0

        lax.fori_loop(0, ncol, column, 0, unroll=True)

    for k, dst_ref in enumerate((qs, fs, bs, vs)):
        to_scan_order(in_refs[2 * k], in_refs[2 * k + 1], dst_ref)
    gc_ref, gl_ref = in_refs[8:10]

    lbp = lb_ref[...]
    e = jnp.exp(lbp - jnp.max(lbp, axis=0, keepdims=True))
    sm = e / jnp.sum(e, axis=0, keepdims=True)
    lower = jnp.zeros(sm.shape[1:], F32)
    for l in range(1, layer + 1):
        lower = lower + sm[l]

    q_scale = dk ** -0.5
    j_idx = lax.broadcasted_iota(jnp.int32, (cps, ch, ch), 1)
    s_idx = lax.broadcasted_iota(jnp.int32, (cps, ch, ch), 2)
    causal = (j_idx >= s_idx, j_idx <= s_idx)
    aux = 2 * V7X_SUBLANES
    aux_row = lax.broadcasted_iota(jnp.int32, (1, aux, 1), 1)

    def bf16_pieces(x):
        hi = x.astype(BF16)
        rest = x - hi.astype(F32)
        mid = rest.astype(BF16)
        lo = (rest - mid.astype(F32)).astype(BF16)
        return hi, mid, lo

    def gate_feats(rows, d):
        lo = lower[d:d + 1]
        half_open = 0.5 * (1.0 - lo)
        th = jnp.tanh((bs if d else fs)[rows, :])
        log_f = jnp.log(lo + half_open * (1.0 + th))
        k = half_open * (1.0 - th)
        return k, _group_cumsum(log_f, ch, reverse=bool(d))

    def exact_intra(rows):
        q3 = (qs[rows, :] * q_scale).reshape(cps, ch, lanes)
        v3 = vs[rows, :].reshape(cps, ch, lanes)
        pos = lax.broadcasted_iota(jnp.int32, (1, ch, 1), 1)
        for d in range(2):
            k, cum = gate_feats(rows, d)
            k_slow[...] = k.reshape(cps, ch, lanes)
            cum_slow[...] = cum.reshape(cps, ch, lanes)

            def row_j(j, _):
                cum3 = cum_slow[...]
                w = jnp.exp(jnp.minimum(cum_slow[:, pl.ds(j, 1), :] - cum3, 0.0))
                valid = (pos >= j) if d else (pos <= j)
                q_j = jnp.where(pos == j, q3, 0.0).sum(axis=1, keepdims=True)
                col = jnp.sum(jnp.where(valid, q_j * k_slow[...] * w, 0.0), axis=-1, keepdims=True)
                o_j = jnp.sum(col * v3, axis=1, keepdims=True)
                if d:
                    o_slow[:, pl.ds(j, 1), :] += o_j
                else:
                    o_slow[:, pl.ds(j, 1), :] = o_j
                return 0

            lax.fori_loop(0, ch, row_j, 0)
        oacc[rows, :] = o_slow[...].reshape(sb, lanes)

    def local_part(i, _):
        rows = pl.ds(pl.multiple_of(i * sb, sb), sb)
        q = qs[rows, :] * q_scale
        v3 = vs[rows, :].reshape(cps, ch, lanes).astype(BF16)
        zeros_v = jnp.zeros((cps, ch, lanes), BF16)
        pad_rows = jnp.concatenate([jnp.zeros((cps, aux, lanes), BF16), jnp.ones((cps, aux, lanes), BF16)], axis=2)
        kv_rhs = jnp.concatenate([jnp.concatenate([v3, zeros_v], axis=2), pad_rows], axis=1)
        o_sum = None
        deepest = None
        for d in range(2):
            k, cum = gate_feats(rows, d)
            cum_l = cum * LOG2_E
            cum_l3 = cum_l.reshape(cps, ch, lanes)
            total_l3 = cum_l3[:, 0:1, :] if d else cum_l3[:, ch - 1:ch, :]
            qd = (q * jnp.exp2(cum_l)).astype(BF16)
            kd = (k * jnp.exp2(jnp.minimum(cum * (-LOG2_E), EXP_CLAMP * LOG2_E))).astype(BF16)
            kl = (k.reshape(cps, ch, lanes) * jnp.exp2(total_l3 - cum_l3)).astype(BF16)
            att = jnp.einsum("cjd,csd->cjs", qd.reshape(cps, ch, lanes), kd.reshape(cps, ch, lanes),
                             preferred_element_type=F32)
            att = jnp.where(causal[d], att, 0.0)
            o_intra = jnp.einsum("cjs,cse->cje", att.astype(BF16), v3, preferred_element_type=F32)
            o_sum = o_intra if o_sum is None else o_sum + o_intra
            qd_refs[d][rows, :] = qd
            d_hi, d_mid, d_lo = [p.astype(F32) for p in bf16_pieces(jnp.exp2(total_l3))]
            dec_rows = jnp.where(aux_row == 0, d_hi, jnp.where(aux_row == 1, d_mid, jnp.where(aux_row == 2, d_lo, 0.0)))
            kv_lhs = jnp.concatenate([kl, dec_rows.astype(BF16)], axis=1)
            for c in range(cps):
                kv_refs[d][i * cps + c] = lax.dot_general(kv_lhs[c], kv_rhs[c], (((0,), (0,)), ((), ())),
                                                          preferred_element_type=F32)
            low = jnp.min(total_l3, axis=0)
            deepest = low if deepest is None else jnp.minimum(deepest, low)
        oacc[rows, :] = o_sum.reshape(sb, lanes)
        deepest_ref[i] = jnp.min(deepest)
        return 0

    lax.fori_loop(0, nsb, local_part, 0, unroll=unroll)

    def redo_unsafe(i, _):
        @pl.when(deepest_ref[i] < -EXP_CLAMP * LOG2_E)
        def _():
            exact_intra(pl.ds(pl.multiple_of(i * sb, sb), sb))

        return 0

    lax.fori_loop(0, nsb, redo_unsafe, 0)

    def state_step(d, n, st):
        st_refs[d][n] = st.astype(BF16)
        kv = kv_refs[d][n]
        return kv[:, lanes:] * st + kv[:, :lanes]

    zero_state = jnp.zeros((lanes, lanes), F32)
    lax.fori_loop(0, nchunk, functools.partial(state_step, 0), zero_state, unroll=8)
    st = lax.fori_loop(0, ctx_chunks, lambda i, s: state_step(1, ctx_chunks - 1 - i, s), zero_state, unroll=8)
    lax.fori_loop(0, nchunk - ctx_chunks, lambda i, s: state_step(1, nchunk - 1 - i, s), st, unroll=8)

    nw = nw_ref[...]

    def inter_part(i, _):
        rows = pl.ds(pl.multiple_of(i * sb, sb), sb)
        o = oacc[rows, :]
        for d in range(2):
            qd3 = qd_refs[d][rows, :].reshape(cps, ch, lanes)
            st3 = st_refs[d][pl.ds(i * cps, cps)]
            o = o + jnp.einsum("cjd,cde->cje", qd3, st3, preferred_element_type=F32).reshape(sb, lanes)
        oacc[rows, :] = o * lax.rsqrt(jnp.mean(o * o, axis=-1, keepdims=True) + EPS) * nw
        return 0

    first_sb = 0 if need_ctx else ctx_len // sb
    lax.fori_loop(first_sb, nsb, inter_part, 0, unroll=True)

    def gated(o, g):
        return (o * (g * jax.nn.sigmoid(g))).astype(y_ref.dtype)

    if need_ctx:
        yc_ref[...] = gated(oacc[pl.ds(0, ctx_len), :], gc_ref[...])

    def column(n, _):
        src = pl.ds(pl.multiple_of(ctx_len + n * ch, ch), ch)
        pitch[pl.ds(n, ch, stride=grid_pitch), :] = oacc[src, :]
        return 0

    lax.fori_loop(0, ncol, column, 0, unroll=True)

    def gather_row(r, _):
        dst = pl.ds(pl.multiple_of(r * ncol, ncol), ncol)
        y_ref[dst, :] = gated(pitch[pl.ds(pl.multiple_of(r * grid_pitch, V7X_SUBLANES), ncol), :], gl_ref[dst, :])
        return 0

    lax.fori_loop(0, ch, gather_row, 0, unroll=True)


def _hg_call(p, hg_lb, norm_w, *, n_batch, ctx_len, seq_len, layer, need_ctx):
    t, w5 = p.shape
    w = w5 // 5
    lanes = V7X_LANES
    nh = w // lanes
    lb = ctx_len + seq_len
    lat_block0 = n_batch * ctx_len // seq_len
    nchunk = lb // HG_CHUNK
    depth = hg_lb.shape[0]
    blk_bytes = lb * lanes * 4
    cps = ctx_len // HG_CHUNK
    pitch_rows = HG_CHUNK * (seq_len // HG_CHUNK + V7X_SUBLANES)
    unroll = lb // ctx_len
    est = (10 * blk_bytes + 2 * lb * lanes * 2 + 8 * blk_bytes + pitch_rows * lanes * 4 + 2 * nchunk * lanes * lanes * 10
           + unroll * 40 * ctx_len * lanes * 4 + (4 << 20))
    kern = functools.partial(_hg_kernel, ctx_len=ctx_len, seq_len=seq_len, layer=layer, dk=lanes, unroll=unroll,
                             need_ctx=need_ctx)
    in_specs = []
    for k in range(5):
        in_specs.append(pl.BlockSpec((ctx_len, lanes), lambda b, h, k=k: (b, k * nh + h)))
        in_specs.append(pl.BlockSpec((seq_len, lanes), lambda b, h, k=k: (lat_block0 + b, k * nh + h)))
    out_specs, out_shapes = _mixer_out(t, w, n_batch, ctx_len, seq_len, lanes, need_ctx)
    outs = pl.pallas_call(
        kern,
        grid=(n_batch, nh),
        in_specs=in_specs + [pl.BlockSpec((depth, 2, lanes), lambda b, h: (0, 0, h)),
                             pl.BlockSpec((1, lanes), lambda b, h: (0, 0))],
        out_specs=out_specs,
        out_shape=out_shapes,
        scratch_shapes=[pltpu.VMEM((pitch_rows, lanes), F32)]
        + [pltpu.VMEM((lb, lanes), F32)] * 4
        + [pltpu.VMEM((lb, lanes), BF16)] * 2
        + [pltpu.VMEM((lb, lanes), F32)]
        + [pltpu.VMEM((nchunk, lanes, 2 * lanes), F32)] * 2
        + [pltpu.VMEM((nchunk, lanes, lanes), BF16)] * 2
        + [pltpu.VMEM((cps, HG_CHUNK, lanes), F32)] * 3
        + [pltpu.SMEM((lb // ctx_len,), F32)],
        compiler_params=_params(("parallel", "parallel"), est),
        name="hgrn2_mixer",
    )(*([p] * 10), hg_lb, norm_w)
    return outs if need_ctx else (outs[0], None)


def kernel(x, c, ctx, c_ctx, w_ada, b_ada, g_mix, g_ffn, g_final, w_ffn_in, w_ffn_out,
           rg_w_in, rg_conv_w, rg_conv_b, rg_w_a, rg_b_a, rg_w_i, rg_b_i, rg_lam, rg_w_out,
           hg_w_in, hg_lb, hg_norm, hg_w_out):
    n_batch, seq_len, d = x.shape
    ctx_len = ctx.shape[1]
    depth = w_ada.shape[0]
    lanes = V7X_LANES
    assert seq_len == V7X_SUBLANES * ctx_len, "row blocking assumes the latent is 8 context lengths long"
    assert seq_len // GRID_W == HG_CHUNK, "an HGRN2 chunk must be one latent grid column"
    assert (n_batch * ctx_len) % seq_len == 0, "latent blocks must stay block-aligned behind the context rows"
    assert d % lanes == 0 and rg_w_a.shape[-1] == lanes and hg_norm.shape[-1] == lanes
    dims = dict(n_batch=n_batch, ctx_len=ctx_len, seq_len=seq_len)
    rows = _Rows(n_batch, ctx_len, seq_len, ROW_TILE)
    rows_half = _Rows(n_batch, ctx_len, seq_len, ROW_TILE // 2)
    nct = rows.n_ctx_tiles

    pad = (-(n_batch + 1)) % V7X_SUBLANES
    cvec = jnp.concatenate([c, c_ctx[None, :], jnp.zeros((pad, d), F32)], axis=0)
    mods = _ada_call(cvec, w_ada, b_ada).reshape(depth, cvec.shape[0], N_MOD, d)

    w_ffn_in_b, w_ffn_out_b = w_ffn_in.astype(BF16), w_ffn_out.astype(BF16)
    rg_w_in_b, rg_w_out_b = rg_w_in.astype(BF16), rg_w_out.astype(BF16)
    fifth = jnp.arange(hg_w_in.shape[-1]) // d
    hg_w_in_b = (hg_w_in * jnp.where((fifth == 1) | (fifth == 2), 0.5, 1.0)).astype(BF16)
    hg_w_out_b = hg_w_out.astype(BF16)

    x_parts = [(ctx.reshape(n_batch * ctx_len, d), 0), (x.reshape(n_batch * seq_len, d), 0)]
    out = None
    for i in range(depth):
        last = i == depth - 1
        sh1, sc1, ga1, sh2, sc2, ga2 = [mods[i, :, k][:, None, :] for k in range(N_MOD)]
        j = i // N_MIXERS
        if i % N_MIXERS == 0:
            p = _proj_call(x_parts, g_mix[i][None], sh1, sc1, rg_w_in_b, j, rows)
            w_gates = (0.5 * jnp.concatenate([rg_w_a[j, 0], rg_w_i[j, 0], rg_w_a[j, 1], rg_w_i[j, 1]], axis=-1)).astype(BF16)
            b_gates = 0.5 * jnp.concatenate([rg_b_a[j, 0], rg_b_i[j, 0], rg_b_a[j, 1], rg_b_i[j, 1]], axis=-1)[:, None, :]
            y, y_ctx = _rg_call(p, rg_conv_w[j], rg_conv_b[j][None], w_gates, b_gates, rg_lam[j], need_ctx=not last, **dims)
            w_out = rg_w_out_b
        else:
            p = _proj_call(x_parts, g_mix[i][None], sh1, sc1, hg_w_in_b, j, rows)
            y, y_ctx = _hg_call(p, hg_lb, hg_norm[j][None], layer=i, need_ctx=not last, **dims)
            w_out = hg_w_out_b
        if last:
            xs = _outproj_latent(y, w_out, j, x_parts, ga1, rows_half)
            out = _ffn_call(xs, 0, g_ffn[i][None], sh2, sc2, ga2, w_ffn_in_b, w_ffn_out_b, i, rows, nct,
                            final_g=g_final[None])
        else:
            xs = _outproj_call([(y_ctx, 0), (y, 0)], w_out, j, x_parts, ga1, rows_half, 0)
            xs = _ffn_call(xs, 0, g_ffn[i][None], sh2, sc2, ga2, w_ffn_in_b, w_ffn_out_b, i, rows, 0)
            x_parts = [(xs, 0)]
    return out.reshape(n_batch, seq_len, d)


def _outproj_latent(y, w_out, j, x_parts, gate, rows):
    nct = rows.n_ctx_tiles
    if len(x_parts) == 2:
        x_lat = [(x_parts[1][0], -nct)]
    else:
        x_lat = [(x_parts[0][0], 0)]
    return _outproj_call([(y, -nct)], w_out, j, x_lat, gate, rows, nct)
```

```python
import functools

import jax
import jax.numpy as jnp
from jax import lax
from jax.experimental import pallas as pl
from jax.experimental.pallas import tpu as pltpu

F32 = jnp.float32
BF16 = jnp.bfloat16

EPS = 1e-6
RG_C = 8.0
GRID_W = 64
N_MOD = 6
N_MIXERS = 2
CONV_TAPS = 4
CONV_LEFT = 2

V7X_LANES = 128
V7X_SUBLANES = 8
V7X_VMEM_LIMIT_CAP = 57 * 1024 * 1024

HG_CHUNK = 32
EXP_CLAMP = 80.0
LOG2_E = 1.4426950408889634
ROW_CHUNK = 128
ROW_TILE = 1024


def _col_tile(n, preferred=512):
    t = preferred
    while n % t:
        t -= V7X_LANES
    return t


def _params(sem, nbytes):
    return pltpu.CompilerParams(dimension_semantics=sem, vmem_limit_bytes=int(min(V7X_VMEM_LIMIT_CAP, nbytes)))


class _Rows:
    def __init__(self, n_batch, ctx_len, seq_len, tm):
        assert (n_batch * ctx_len) % tm == 0 and seq_len % tm == 0
        self.n_batch, self.ctx_len, self.seq_len, self.tm = n_batch, ctx_len, seq_len, tm
        self.n_ctx_tiles = n_batch * ctx_len // tm
        self.tiles_per_seq = seq_len // tm
        self.n_lat_tiles = n_batch * self.tiles_per_seq
        self.n_tiles = self.n_ctx_tiles + self.n_lat_tiles

    def mod_row(self, tile):
        return jnp.where(tile < self.n_ctx_tiles, self.n_batch, (tile - self.n_ctx_tiles) // self.tiles_per_seq)


def _mod_spec(rows, first_tile, ncols, col_of_j):
    return pl.BlockSpec((None, 1, ncols), lambda i, j: (rows.mod_row(i + first_tile), 0, col_of_j(j)))


def _part_specs(parts, rows, first_tile, ncols, col_of_j, single_buffer_ctx=False):
    if len(parts) == 1:
        (_, off), = parts
        return [pl.BlockSpec((rows.tm, ncols), lambda i, j: (i + first_tile + off, col_of_j(j)))]
    assert first_tile == 0
    (_, off_c), (_, off_l) = parts
    nct = rows.n_ctx_tiles
    ctx_mode = dict(pipeline_mode=pl.Buffered(1)) if single_buffer_ctx else {}
    return [pl.BlockSpec((rows.tm, ncols), lambda i, j: (jnp.minimum(i, nct - 1) + off_c, col_of_j(j)), **ctx_mode),
            pl.BlockSpec((rows.tm, ncols), lambda i, j: (jnp.maximum(i - nct, 0) + off_l, col_of_j(j)))]


def _with_part(refs, n_ctx_tiles, fn):
    if len(refs) == 1:
        fn(refs[0])
        return
    is_ctx = pl.program_id(0) < n_ctx_tiles
    pl.when(is_ctx)(lambda: fn(refs[0]))
    pl.when(jnp.logical_not(is_ctx))(lambda: fn(refs[1]))


def _for_row_chunks(tm, fn):
    def body(r, _):
        fn(pl.ds(pl.multiple_of(r * ROW_CHUNK, ROW_CHUNK), ROW_CHUNK))
        return 0

    lax.fori_loop(0, tm // ROW_CHUNK, body, 0, unroll=4)


def _rms_modulate_into(h_ref, inv_ref, x_ref, g_ref, sh_ref, sc_ref, tm):
    def stats(rows):
        x = x_ref[rows, :]
        inv_ref[rows, :] = lax.rsqrt(jnp.mean(x * x, axis=-1, keepdims=True) + EPS)

    _for_row_chunks(tm, stats)
    gain = g_ref[...] * (1.0 + sc_ref[...])
    shift = sh_ref[...]

    def affine(rows):
        h_ref[rows, :] = (x_ref[rows, :] * inv_ref[rows, :] * gain + shift).astype(h_ref.dtype)

    _for_row_chunks(tm, affine)


def _ada_kernel(c_ref, w_ref, b_ref, o_ref):
    c = c_ref[...]
    s = (c * jax.nn.sigmoid(c)).astype(BF16)
    o_ref[...] = jnp.dot(s, w_ref[...].astype(BF16), preferred_element_type=F32) + b_ref[...]


def _ada_call(cvec, w_ada, b_ada):
    depth, d, n = w_ada.shape
    r = cvec.shape[0]
    tn = _col_tile(n)
    est = 2 * (d * tn * 4) + d * tn * 2 + 4 * r * (d + 2 * tn) * 4 + (4 << 20)
    return pl.pallas_call(
        _ada_kernel,
        grid=(depth, n // tn),
        in_specs=[
            pl.BlockSpec((r, d), lambda l, j: (0, 0)),
            pl.BlockSpec((None, d, tn), lambda l, j: (l, 0, j)),
            pl.BlockSpec((None, 1, tn), lambda l, j: (l, 0, j)),
        ],
        out_specs=pl.BlockSpec((None, r, tn), lambda l, j: (l, 0, j)),
        out_shape=jax.ShapeDtypeStruct((depth, r, n), F32),
        compiler_params=_params(("parallel", "parallel"), est),
        name="ada_mod",
    )(cvec, w_ada, b_ada.reshape(depth, 1, n))


def _proj_kernel(*refs, n_x, tm, n_ctx_tiles):
    x_refs = refs[:n_x]
    g_ref, sh_ref, sc_ref, w_ref, o_ref, h_ref, inv_ref = refs[n_x:]

    @pl.when(pl.program_id(1) == 0)
    def _():
        _with_part(x_refs, n_ctx_tiles, lambda x_ref: _rms_modulate_into(h_ref, inv_ref, x_ref, g_ref, sh_ref, sc_ref, tm))

    acc = jnp.dot(h_ref[...], w_ref[...], preferred_element_type=F32)
    for s in range(o_ref.shape[0]):
        o_ref[s] = acc[:, s * V7X_LANES:(s + 1) * V7X_LANES]


def _proj_call(x_parts, g, shift, scale, w, layer, rows):
    d = x_parts[0][0].shape[1]
    n = w.shape[2]
    nx = len(x_parts)
    tm, tn = rows.tm, _col_tile(n, 1024)
    x_bufs = 2 if nx == 1 else 3
    est = x_bufs * tm * d * 4 + tm * d * 2 + 2 * d * tn * 2 + 3 * tm * tn * 4 + 8 * ROW_CHUNK * d * 4 + (3 << 20)
    kern = functools.partial(_proj_kernel, n_x=nx, tm=tm, n_ctx_tiles=rows.n_ctx_tiles)
    full = lambda j: 0
    return pl.pallas_call(
        kern,
        grid=(rows.n_tiles, n // tn),
        in_specs=[*_part_specs(x_parts, rows, 0, d, full, single_buffer_ctx=True),
                  pl.BlockSpec((1, d), lambda i, j: (0, 0)),
                  _mod_spec(rows, 0, d, full), _mod_spec(rows, 0, d, full),
                  pl.BlockSpec((None, d, tn), lambda i, j: (layer, 0, j))],
        out_specs=pl.BlockSpec((tn // V7X_LANES, tm, V7X_LANES), lambda i, j: (j, i, 0)),
        out_shape=jax.ShapeDtypeStruct((n // V7X_LANES, rows.n_tiles * tm, V7X_LANES), F32),
        scratch_shapes=[pltpu.VMEM((tm, d), BF16), pltpu.VMEM((tm, 1), F32)],
        compiler_params=_params(("parallel", "arbitrary"), est),
        name="norm_mod_proj",
    )(*[a for a, _ in x_parts], g, shift, scale, w)


def _outproj_kernel(*refs, n_a, n_x, n_ctx_tiles):
    a_refs, x_refs = refs[:n_a], refs[n_a:n_a + n_x]
    w_ref, ga_ref, o_ref = refs[n_a + n_x:]

    def with_a(a_ref):
        acc = jnp.dot(a_ref[...], w_ref[...], preferred_element_type=F32) * ga_ref[...]

        def with_x(x_ref):
            o_ref[...] = x_ref[...] + acc

        _with_part(x_refs, n_ctx_tiles, with_x)

    _with_part(a_refs, n_ctx_tiles, with_a)


def _outproj_call(a_parts, w, layer, x_parts, gate, rows, first_tile):
    k, n = w.shape[1], w.shape[2]
    tm, tn = rows.tm, _col_tile(n, 2048)
    na, nx = len(a_parts), len(x_parts)
    n_tiles = rows.n_tiles - first_tile
    est = 2 * na * tm * k * 2 + 2 * k * tn * 2 + 2 * (nx + 1) * tm * tn * 4 + 3 * tm * tn * 4 + (3 << 20)
    kern = functools.partial(_outproj_kernel, n_a=na, n_x=nx, n_ctx_tiles=rows.n_ctx_tiles)
    return pl.pallas_call(
        kern,
        grid=(n_tiles, n // tn),
        in_specs=[*_part_specs(a_parts, rows, first_tile, k, lambda j: 0),
                  *_part_specs(x_parts, rows, first_tile, tn, lambda j: j),
                  pl.BlockSpec((None, k, tn), lambda i, j: (layer, 0, j)),
                  _mod_spec(rows, first_tile, tn, lambda j: j)],
        out_specs=pl.BlockSpec((tm, tn), lambda i, j: (i, j)),
        out_shape=jax.ShapeDtypeStruct((n_tiles * tm, n), F32),
        compiler_params=_params(("parallel", "parallel"), est),
        name="outproj_residual",
    )(*[a for a, _ in a_parts], *[a for a, _ in x_parts], w, gate)


def _ffn_kernel(*refs, tm, final_norm):
    if final_norm:
        x_ref, g_ref, sh_ref, sc_ref, ga_ref, wg_ref, wu_ref, wo_ref, gf_ref, o_ref, h_ref, inv_ref = refs
    else:
        x_ref, g_ref, sh_ref, sc_ref, ga_ref, wg_ref, wu_ref, wo_ref, o_ref, h_ref, inv_ref = refs
    f = pl.program_id(1)

    @pl.when(f == 0)
    def _():
        _rms_modulate_into(h_ref, inv_ref, x_ref, g_ref, sh_ref, sc_ref, tm)
        o_ref[...] = jnp.zeros_like(o_ref)

    tn = _col_tile(o_ref.shape[1])
    half = tm // 2
    for r0 in range(0, tm, half):
        h = h_ref[r0:r0 + half, :]
        gate = jnp.dot(h, wg_ref[...], preferred_element_type=F32)
        up = jnp.dot(h, wu_ref[...], preferred_element_type=F32)
        act = (gate * jax.nn.sigmoid(gate) * up).astype(BF16)
        for c0 in range(0, o_ref.shape[1], tn):
            o_ref[r0:r0 + half, c0:c0 + tn] += jnp.dot(act, wo_ref[:, c0:c0 + tn], preferred_element_type=F32)

    @pl.when(f == pl.num_programs(1) - 1)
    def _():
        res_gate = ga_ref[...]

        def chunk(rows):
            y = x_ref[rows, :] + res_gate * o_ref[rows, :]
            if final_norm:
                y = y * lax.rsqrt(jnp.mean(y * y, axis=-1, keepdims=True) + EPS) * gf_ref[...]
            o_ref[rows, :] = y

        _for_row_chunks(tm, chunk)


def _ffn_call(xs, x_first_tile, g, shift, scale, res_gate, w_in, w_out, layer, rows, first_tile, final_g=None):
    d = xs.shape[1]
    ff = w_out.shape[1]
    tm, tf = rows.tm, _col_tile(ff)
    nf = ff // tf
    n_tiles = rows.n_tiles - first_tile
    est = (4 * tm * d * 4 + tm * d * 2 + 2 * (2 * d * tf * 2 + tf * d * 2) + 2 * tm * tf * 4 + tm * tf * 2
           + 8 * ROW_CHUNK * d * 4 + (3 << 20))
    kern = functools.partial(_ffn_kernel, tm=tm, final_norm=final_g is not None)
    full = lambda f: 0
    extra_specs = [] if final_g is None else [pl.BlockSpec((1, d), lambda i, f: (0, 0))]
    extra_args = [] if final_g is None else [final_g]
    return pl.pallas_call(
        kern,
        grid=(n_tiles, nf),
        in_specs=[pl.BlockSpec((tm, d), lambda i, f: (i + x_first_tile, 0)),
                  pl.BlockSpec((1, d), lambda i, f: (0, 0)),
                  _mod_spec(rows, first_tile, d, full), _mod_spec(rows, first_tile, d, full),
                  _mod_spec(rows, first_tile, d, full),
                  pl.BlockSpec((None, d, tf), lambda i, f: (layer, 0, f)),
                  pl.BlockSpec((None, d, tf), lambda i, f: (layer, 0, nf + f)),
                  pl.BlockSpec((None, tf, d), lambda i, f: (layer, f, 0)),
                  *extra_specs],
        out_specs=pl.BlockSpec((tm, d), lambda i, f: (i, 0)),
        out_shape=jax.ShapeDtypeStruct((n_tiles * tm, d), F32),
        scratch_shapes=[pltpu.VMEM((tm, d), BF16), pltpu.VMEM((tm, 1), F32)],
        compiler_params=_params(("parallel", "arbitrary"), est),
        name="swiglu_ffn",
    )(xs, g, shift, scale, res_gate, w_in, w_in, w_out, *extra_args)


def _softplus(x):
    return jnp.maximum(x, 0.0) + jnp.log1p(jnp.exp(-jnp.abs(x)))


def _chunk_carries(h_loc, p_loc, h_init, reverse):
    n = h_loc.shape[0]
    carries = [None] * n
    c = h_init
    for s in (reversed(range(n)) if reverse else range(n)):
        carries[s] = c
        c = p_loc[s:s + 1] * c + h_loc[s:s + 1]
    return carries, c


def _strided_scan(coef, res, n, stride):
    nsub = V7X_SUBLANES
    af_ref, bf_ref, ab_ref, bb_ref = coef
    pf_ref, hf_ref, pb_ref, hb_ref = res
    lanes = af_ref.shape[1]

    def body(i, carry):
        hf, pf, hb, pb = carry
        rows_f = pl.ds(i, nsub, stride=stride)
        rows_b = pl.ds(n - 1 - i, nsub, stride=stride)
        a = af_ref[rows_f, :]
        hf = a * hf + bf_ref[rows_f, :]
        pf = a * pf
        hf_ref[rows_f, :] = hf
        pf_ref[rows_f, :] = pf
        a = ab_ref[rows_b, :]
        hb = a * hb + bb_ref[rows_b, :]
        pb = a * pb
        hb_ref[rows_b, :] = hb
        pb_ref[rows_b, :] = pb
        return hf, pf, hb, pb

    zero = jnp.zeros((nsub, lanes), F32)
    one = jnp.ones((nsub, lanes), F32)
    return lax.fori_loop(0, n, body, (zero, one, zero, one), unroll=8)


def _rg_kernel(*refs, ctx_len, seq_len, need_ctx):
    xc_ref, xl_ref, gc_ref, gl_ref, cw_ref, cb_ref, wg_ref, bg_ref, lam_ref = refs[:9]
    n_out = 2 if need_ctx else 1
    y_ref = refs[9]
    yc_ref = refs[10] if need_ctx else None
    (xp_ref, laf, lbf, lab, lbb, lpf, lhf, lpb, lhb, caf, cbf, cab, cbb, cpf, chf, cpb, chb, carry_ref) = refs[9 + n_out:]
    nsub = V7X_SUBLANES
    lanes = xc_ref.shape[1]
    rb = ctx_len
    nblk = 1 + seq_len // rb
    halo = nsub
    lat_stride = rb + nsub
    ctx_n = ctx_len // nsub
    ctx_stride = ctx_n + nsub

    xp_ref[pl.ds(0, halo), :] = jnp.zeros((halo, lanes), F32)
    xp_ref[pl.ds(halo + ctx_len + seq_len, halo), :] = jnp.zeros((halo, lanes), F32)
    xp_ref[pl.ds(halo, ctx_len), :] = xc_ref[...]
    xp_ref[pl.ds(halo + ctx_len, seq_len), :] = xl_ref[...]

    cw = cw_ref[...]
    cb = cb_ref[...]
    softplus_neg_lam = _softplus(-lam_ref[...])
    local_row = lax.broadcasted_iota(jnp.int32, (rb, 1), 0)

    def coeffs(blk):
        start = blk * rb if isinstance(blk, int) else pl.multiple_of(blk * rb, rb)
        seg_first = blk <= 1
        seg_last = jnp.logical_or(blk == 0, blk == nblk - 1)
        taps = []
        for k in range(CONV_TAPS):
            off = k - CONV_LEFT
            tap = xp_ref[pl.ds(start + halo + off, rb), :]
            if off < 0:
                tap = jnp.where(jnp.logical_or(local_row >= -off, jnp.logical_not(seg_first)), tap, 0.0)
            elif off > 0:
                tap = jnp.where(jnp.logical_or(local_row < rb - off, jnp.logical_not(seg_last)), tap, 0.0)
            taps.append(tap)
        xc = cb
        for k in range(CONV_TAPS):
            xc = xc + cw[k:k + 1] * taps[k]
        half_z = jnp.dot(xc.astype(BF16), wg_ref[...], preferred_element_type=F32) + bg_ref[...]
        half_xc = 0.5 * xc
        out = []
        for d in range(2):
            th_r = jnp.tanh(half_z[:, (2 * d) * lanes:(2 * d + 1) * lanes])
            th_i = jnp.tanh(half_z[:, (2 * d + 1) * lanes:(2 * d + 2) * lanes])
            scale = (-0.5 * RG_C) * softplus_neg_lam[d:d + 1]
            log_a = scale * th_r + scale
            a = jnp.exp(log_a)
            t = jnp.tanh(log_a)
            u = (-2.0) * t / (1.0 - t)
            mult = jnp.where(u > 0.0, u * lax.rsqrt(u), 0.0)
            out.append((a, mult * half_xc * (1.0 + th_i)))
        return out

    (a_f, b_f), (a_b, b_b) = coeffs(0)
    for c in range(nsub):
        dst = pl.ds(c * ctx_stride, ctx_n)
        src = slice(c * ctx_n, (c + 1) * ctx_n)
        caf[dst, :] = a_f[src]
        cbf[dst, :] = b_f[src]
        cab[dst, :] = a_b[src]
        cbb[dst, :] = b_b[src]

    def fill(blk, _):
        (a_f, b_f), (a_b, b_b) = coeffs(blk)
        dst = pl.ds(pl.multiple_of((blk - 1) * lat_stride, nsub), rb)
        laf[dst, :] = a_f
        lbf[dst, :] = b_f
        lab[dst, :] = a_b
        lbb[dst, :] = b_b
        return 0

    lax.fori_loop(1, nblk, fill, 0, unroll=4)

    zero_state = jnp.zeros((1, lanes), F32)
    hf, pf, hb, pb = _strided_scan((caf, cbf, cab, cbb), (cpf, chf, cpb, chb), ctx_n, ctx_stride)
    ctx_in_f, ctx_state_f = _chunk_carries(hf, pf, zero_state, reverse=False)
    ctx_in_b, ctx_state_b = _chunk_carries(hb, pb, zero_state, reverse=True)
    hf, pf, hb, pb = _strided_scan((laf, lbf, lab, lbb), (lpf, lhf, lpb, lhb), rb, lat_stride)
    lat_in_f, _ = _chunk_carries(hf, pf, ctx_state_f, reverse=False)
    lat_in_b, _ = _chunk_carries(hb, pb, ctx_state_b, reverse=True)
    for s in range(nsub):
        carry_ref[pl.ds(s, 1), :] = lat_in_f[s]
        carry_ref[pl.ds(nsub + s, 1), :] = lat_in_b[s]

    if need_ctx:
        for c in range(nsub):
            src = pl.ds(c * ctx_stride, ctx_n)
            h = chf[src, :] + cpf[src, :] * ctx_in_f[c] + chb[src, :] + cpb[src, :] * ctx_in_b[c]
            rows = pl.ds(c * ctx_n, ctx_n)
            yc_ref[rows, :] = (h * jax.nn.gelu(gc_ref[rows, :])).astype(yc_ref.dtype)

    def emit(blk, _):
        src = pl.ds(pl.multiple_of((blk - 1) * lat_stride, nsub), rb)
        cf = carry_ref[pl.ds(blk - 1, 1), :]
        cbw = carry_ref[pl.ds(nsub + blk - 1, 1), :]
        h = lhf[src, :] + lpf[src, :] * cf + lhb[src, :] + lpb[src, :] * cbw
        rows = pl.ds(pl.multiple_of((blk - 1) * rb, rb), rb)
        y_ref[rows, :] = (h * jax.nn.gelu(gl_ref[rows, :])).astype(y_ref.dtype)
        return 0

    lax.fori_loop(1, nblk, emit, 0, unroll=4)


def _mixer_out(t, w, n_batch, ctx_len, seq_len, lanes, need_ctx):
    specs = [pl.BlockSpec((seq_len, lanes), lambda b, h: (b, h))]
    shapes = [jax.ShapeDtypeStruct((n_batch * seq_len, w), BF16)]
    if need_ctx:
        specs.append(pl.BlockSpec((ctx_len, lanes), lambda b, h: (b, h)))
        shapes.append(jax.ShapeDtypeStruct((n_batch * ctx_len, w), BF16))
    return specs, shapes


def _rg_call(p, conv_w, conv_b, w_gates, b_gates, lam, *, n_batch, ctx_len, seq_len, need_ctx):
    lanes = V7X_LANES
    t = p.shape[1]
    nh = p.shape[0] // 2
    w = nh * lanes
    nsub = V7X_SUBLANES
    lb = ctx_len + seq_len
    lat_block0 = n_batch * ctx_len // seq_len
    lat_rows = nsub * (seq_len // nsub + nsub)
    ctx_rows = nsub * (ctx_len // nsub + nsub)
    est = (4 * lb * lanes * 4 + 2 * lb * lanes * 2 + (lb + 2 * nsub) * lanes * 4 + 8 * (lat_rows + ctx_rows) * lanes * 4
           + 48 * ctx_len * lanes * 4 + (4 << 20))
    kern = functools.partial(_rg_kernel, ctx_len=ctx_len, seq_len=seq_len, need_ctx=need_ctx)
    out_specs, out_shapes = _mixer_out(t, w, n_batch, ctx_len, seq_len, lanes, need_ctx)
    outs = pl.pallas_call(
        kern,
        grid=(n_batch, nh),
        in_specs=[pl.BlockSpec((None, ctx_len, lanes), lambda b, h: (h, b, 0)),
                  pl.BlockSpec((None, seq_len, lanes), lambda b, h: (h, lat_block0 + b, 0)),
                  pl.BlockSpec((None, ctx_len, lanes), lambda b, h: (nh + h, b, 0)),
                  pl.BlockSpec((None, seq_len, lanes), lambda b, h: (nh + h, lat_block0 + b, 0)),
                  pl.BlockSpec((CONV_TAPS, lanes), lambda b, h: (0, h)),
                  pl.BlockSpec((1, lanes), lambda b, h: (0, h)),
                  pl.BlockSpec((None, lanes, 4 * lanes), lambda b, h: (h, 0, 0)),
                  pl.BlockSpec((None, 1, 4 * lanes), lambda b, h: (h, 0, 0)),
                  pl.BlockSpec((2, lanes), lambda b, h: (0, h))],
        out_specs=out_specs,
        out_shape=out_shapes,
        scratch_shapes=[pltpu.VMEM((lb + 2 * nsub, lanes), F32)]
        + [pltpu.VMEM((lat_rows, lanes), F32)] * 8
        + [pltpu.VMEM((ctx_rows, lanes), F32)] * 8
        + [pltpu.VMEM((2 * nsub, lanes), F32)],
        compiler_params=_params(("parallel", "parallel"), est),
        name="rglru_mixer",
    )(p, p, p, p, conv_w, conv_b, w_gates, b_gates, lam)
    return outs if need_ctx else (outs[0], None)


def _group_cumsum(x, group, reverse):
    n = x.shape[0]
    pos = lax.broadcasted_iota(jnp.int32, (n, 1), 0) % group
    step = 1
    while step < group:
        if reverse:
            shifted = pltpu.roll(x, n - step, axis=0)
            keep = pos < group - step
        else:
            shifted = pltpu.roll(x, step, axis=0)
            keep = pos >= step
        x = x + jnp.where(keep, shifted, 0.0)
        step *= 2
    return x


def _hg_kernel(*refs, ctx_len, seq_len, layer, dk, unroll, need_ctx):
    in_refs = refs[:10]
    lb_ref, nw_ref = refs[10:12]
    n_out = 2 if need_ctx else 1
    y_ref = refs[12]
    yc_ref = refs[13] if need_ctx else None
    (pitch, qs, fs, bs, vs, qd_f, qd_b, oacc, kv_f, kv_b, st_f, st_b,
     k_slow, cum_slow, o_slow, deepest_ref) = refs[12 + n_out:]
    lanes = qs.shape[1]
    ch = HG_CHUNK
    lb_rows = ctx_len + seq_len
    nchunk = lb_rows // ch
    ctx_chunks = ctx_len // ch
    sb = ctx_len
    nsb = lb_rows // sb
    cps = sb // ch
    ncol = seq_len // ch
    grid_pitch = ncol + V7X_SUBLANES
    qd_refs, kv_refs, st_refs = (qd_f, qd_b), (kv_f, kv_b), (st_f, st_b)

    def to_scan_order(ctx_src, lat_src, dst_ref):
        dst_ref[pl.ds(0, ctx_len), :] = ctx_src[...]

        def spread(r, _):
            src = pl.ds(pl.multiple_of(r * ncol, ncol), ncol)
            pitch[pl.ds(pl.multiple_of(r * grid_pitch, V7X_SUBLANES), ncol), :] = lat_src[src, :]
            return 0

        lax.fori_loop(0, ch, spread, 0, unroll=True)

        def column(n, _):
            dst = pl.ds(pl.multiple_of(ctx_len + n * ch, ch), ch)
            dst_ref[dst, :] = pitch[pl.ds(n, ch, stride=grid_pitch), :]
            return 0

        lax.fori_loop(0, ncol, column, 0, unroll=True)

    for k, dst_ref in enumerate((qs, fs, bs, vs)):
        to_scan_order(in_refs[2 * k], in_refs[2 * k + 1], dst_ref)
    gc_ref, gl_ref = in_refs[8:10]

    lbp = lb_ref[...]
    e = jnp.exp(lbp - jnp.max(lbp, axis=0, keepdims=True))
    sm = e / jnp.sum(e, axis=0, keepdims=True)
    lower = jnp.zeros(sm.shape[1:], F32)
    for l in range(1, layer + 1):
        lower = lower + sm[l]

    q_scale = dk ** -0.5
    j_idx = lax.broadcasted_iota(jnp.int32, (cps, ch, ch), 1)
    s_idx = lax.broadcasted_iota(jnp.int32, (cps, ch, ch), 2)
    causal = (j_idx >= s_idx, j_idx <= s_idx)
    aux = 2 * V7X_SUBLANES
    aux_row = lax.broadcasted_iota(jnp.int32, (1, aux, 1), 1)

    def bf16_pieces(x):
        hi = x.astype(BF16)
        rest = x - hi.astype(F32)
        mid = rest.astype(BF16)
        lo = (rest - mid.astype(F32)).astype(BF16)
        return hi, mid, lo

    def gate_feats(rows, d):
        lo = lower[d:d + 1]
        half_open = 0.5 * (1.0 - lo)
        th = jnp.tanh((bs if d else fs)[rows, :])
        log_f = jnp.log(lo + half_open * (1.0 + th))
        k = half_open * (1.0 - th)
        return k, _group_cumsum(log_f, ch, reverse=bool(d))

    def exact_intra(rows):
        q3 = (qs[rows, :] * q_scale).reshape(cps, ch, lanes)
        v3 = vs[rows, :].reshape(cps, ch, lanes)
        pos = lax.broadcasted_iota(jnp.int32, (1, ch, 1), 1)
        for d in range(2):
            k, cum = gate_feats(rows, d)
            k_slow[...] = k.reshape(cps, ch, lanes)
            cum_slow[...] = cum.reshape(cps, ch, lanes)

            def row_j(j, _):
                cum3 = cum_slow[...]
                w = jnp.exp(jnp.minimum(cum_slow[:, pl.ds(j, 1), :] - cum3, 0.0))
                valid = (pos >= j) if d else (pos <= j)
                q_j = jnp.where(pos == j, q3, 0.0).sum(axis=1, keepdims=True)
                col = jnp.sum(jnp.where(valid, q_j * k_slow[...] * w, 0.0), axis=-1, keepdims=True)
                o_j = jnp.sum(col * v3, axis=1, keepdims=True)
                if d:
                    o_slow[:, pl.ds(j, 1), :] += o_j
                else:
                    o_slow[:, pl.ds(j, 1), :] = o_j
                return 0

            lax.fori_loop(0, ch, row_j, 0)
        oacc[rows, :] = o_slow[...].reshape(sb, lanes)

    def local_part(i, _):
        rows = pl.ds(pl.multiple_of(i * sb, sb), sb)
        q = qs[rows, :] * q_scale
        v3 = vs[rows, :].reshape(cps, ch, lanes).astype(BF16)
        zeros_v = jnp.zeros((cps, ch, lanes), BF16)
        pad_rows = jnp.concatenate([jnp.zeros((cps, aux, lanes), BF16), jnp.ones((cps, aux, lanes), BF16)], axis=2)
        kv_rhs = jnp.concatenate([jnp.concatenate([v3, zeros_v], axis=2), pad_rows], axis=1)
        o_sum = None
        deepest = None
        for d in range(2):
            k, cum = gate_feats(rows, d)
            cum_l = cum * LOG2_E
            cum_l3 = cum_l.reshape(cps, ch, lanes)
            total_l3 = cum_l3[:, 0:1, :] if d else cum_l3[:, ch - 1:ch, :]
            qd = (q * jnp.exp2(cum_l)).astype(BF16)
            kd = (k * jnp.exp2(jnp.minimum(cum * (-LOG2_E), EXP_CLAMP * LOG2_E))).astype(BF16)
            kl = (k.reshape(cps, ch, lanes) * jnp.exp2(total_l3 - cum_l3)).astype(BF16)
            att = jnp.einsum("cjd,csd->cjs", qd.reshape(cps, ch, lanes), kd.reshape(cps, ch, lanes),
                             preferred_element_type=F32)
            att = jnp.where(causal[d], att, 0.0)
            o_intra = jnp.einsum("cjs,cse->cje", att.astype(BF16), v3, preferred_element_type=F32)
            o_sum = o_intra if o_sum is None else o_sum + o_intra
            qd_refs[d][rows, :] = qd
            d_hi, d_mid, d_lo = [p.astype(F32) for p in bf16_pieces(jnp.exp2(total_l3))]
            dec_rows = jnp.where(aux_row == 0, d_hi, jnp.where(aux_row == 1, d_mid, jnp.where(aux_row == 2, d_lo, 0.0)))
            kv_lhs = jnp.concatenate([kl, dec_rows.astype(BF16)], axis=1)
            for c in range(cps):
                kv_refs[d][i * cps + c] = lax.dot_general(kv_lhs[c], kv_rhs[c], (((0,), (0,)), ((), ())),
                                                          preferred_element_type=F32)
            low = jnp.min(total_l3, axis=0)
            deepest = low if deepest is None else jnp.minimum(deepest, low)
        oacc[rows, :] = o_sum.reshape(sb, lanes)
        deepest_ref[i] = jnp.min(deepest)
        return 0

    lax.fori_loop(0, nsb, local_part, 0, unroll=unroll)

    def redo_unsafe(i, _):
        @pl.when(deepest_ref[i] < -EXP_CLAMP * LOG2_E)
        def _():
            exact_intra(pl.ds(pl.multiple_of(i * sb, sb), sb))

        return 0

    lax.fori_loop(0, nsb, redo_unsafe, 0)

    def state_step(d, n, st):
        st_refs[d][n] = st.astype(BF16)
        kv = kv_refs[d][n]
        return kv[:, lanes:] * st + kv[:, :lanes]

    zero_state = jnp.zeros((lanes, lanes), F32)
    lax.fori_loop(0, nchunk, functools.partial(state_step, 0), zero_state, unroll=8)
    st = lax.fori_loop(0, ctx_chunks, lambda i, s: state_step(1, ctx_chunks - 1 - i, s), zero_state, unroll=8)
    lax.fori_loop(0, nchunk - ctx_chunks, lambda i, s: state_step(1, nchunk - 1 - i, s), st, unroll=8)

    nw = nw_ref[...]

    def inter_part(i, _):
        rows = pl.ds(pl.multiple_of(i * sb, sb), sb)
        o = oacc[rows, :]
        for d in range(2):
            qd3 = qd_refs[d][rows, :].reshape(cps, ch, lanes)
            st3 = st_refs[d][pl.ds(i * cps, cps)]
            o = o + jnp.einsum("cjd,cde->cje", qd3, st3, preferred_element_type=F32).reshape(sb, lanes)
        oacc[rows, :] = o * lax.rsqrt(jnp.mean(o * o, axis=-1, keepdims=True) + EPS) * nw
        return 0

    first_sb = 0 if need_ctx else ctx_len // sb
    lax.fori_loop(first_sb, nsb, inter_part, 0, unroll=True)

    def gated(o, g):
        return (o * (g * jax.nn.sigmoid(g))).astype(y_ref.dtype)

    if need_ctx:
        yc_ref[...] = gated(oacc[pl.ds(0, ctx_len), :], gc_ref[...])

    def column(n, _):
        src = pl.ds(pl.multiple_of(ctx_len + n * ch, ch), ch)
        pitch[pl.ds(n, ch, stride=grid_pitch), :] = oacc[src, :]
        return 0

    lax.fori_loop(0, ncol, column, 0, unroll=True)

    def gather_row(r, _):
        dst = pl.ds(pl.multiple_of(r * ncol, ncol), ncol)
        y_ref[dst, :] = gated(pitch[pl.ds(pl.multiple_of(r * grid_pitch, V7X_SUBLANES), ncol), :], gl_ref[dst, :])
        return 0

    lax.fori_loop(0, ch, gather_row, 0, unroll=True)


def _hg_call(p, hg_lb, norm_w, *, n_batch, ctx_len, seq_len, layer, need_ctx):
    lanes = V7X_LANES
    t = p.shape[1]
    nh = p.shape[0] // 5
    w = nh * lanes
    lb = ctx_len + seq_len
    lat_block0 = n_batch * ctx_len // seq_len
    nchunk = lb // HG_CHUNK
    depth = hg_lb.shape[0]
    blk_bytes = lb * lanes * 4
    cps = ctx_len // HG_CHUNK
    pitch_rows = HG_CHUNK * (seq_len // HG_CHUNK + V7X_SUBLANES)
    unroll = lb // ctx_len
    est = (10 * blk_bytes + 2 * lb * lanes * 2 + 8 * blk_bytes + pitch_rows * lanes * 4 + 2 * nchunk * lanes * lanes * 10
           + unroll * 40 * ctx_len * lanes * 4 + (4 << 20))
    kern = functools.partial(_hg_kernel, ctx_len=ctx_len, seq_len=seq_len, layer=layer, dk=lanes, unroll=unroll,
                             need_ctx=need_ctx)
    in_specs = []
    for k in range(5):
        in_specs.append(pl.BlockSpec((None, ctx_len, lanes), lambda b, h, k=k: (k * nh + h, b, 0)))
        in_specs.append(pl.BlockSpec((None, seq_len, lanes), lambda b, h, k=k: (k * nh + h, lat_block0 + b, 0)))
    out_specs, out_shapes = _mixer_out(t, w, n_batch, ctx_len, seq_len, lanes, need_ctx)
    outs = pl.pallas_call(
        kern,
        grid=(n_batch, nh),
        in_specs=in_specs + [pl.BlockSpec((depth, 2, lanes), lambda b, h: (0, 0, h)),
                             pl.BlockSpec((1, lanes), lambda b, h: (0, 0))],
        out_specs=out_specs,
        out_shape=out_shapes,
        scratch_shapes=[pltpu.VMEM((pitch_rows, lanes), F32)]
        + [pltpu.VMEM((lb, lanes), F32)] * 4
        + [pltpu.VMEM((lb, lanes), BF16)] * 2
        + [pltpu.VMEM((lb, lanes), F32)]
        + [pltpu.VMEM((nchunk, lanes, 2 * lanes), F32)] * 2
        + [pltpu.VMEM((nchunk, lanes, lanes), BF16)] * 2
        + [pltpu.VMEM((cps, HG_CHUNK, lanes), F32)] * 3
        + [pltpu.SMEM((lb // ctx_len,), F32)],
        compiler_params=_params(("parallel", "parallel"), est),
        name="hgrn2_mixer",
    )(*([p] * 10), hg_lb, norm_w)
    return outs if need_ctx else (outs[0], None)


def kernel(x, c, ctx, c_ctx, w_ada, b_ada, g_mix, g_ffn, g_final, w_ffn_in, w_ffn_out,
           rg_w_in, rg_conv_w, rg_conv_b, rg_w_a, rg_b_a, rg_w_i, rg_b_i, rg_lam, rg_w_out,
           hg_w_in, hg_lb, hg_norm, hg_w_out):
    n_batch, seq_len, d = x.shape
    ctx_len = ctx.shape[1]
    depth = w_ada.shape[0]
    lanes = V7X_LANES
    assert seq_len == V7X_SUBLANES * ctx_len, "row blocking assumes the latent is 8 context lengths long"
    assert seq_len // GRID_W == HG_CHUNK, "an HGRN2 chunk must be one latent grid column"
    assert (n_batch * ctx_len) % seq_len == 0, "latent blocks must stay block-aligned behind the context rows"
    assert d % lanes == 0 and rg_w_a.shape[-1] == lanes and hg_norm.shape[-1] == lanes
    dims = dict(n_batch=n_batch, ctx_len=ctx_len, seq_len=seq_len)
    rows = _Rows(n_batch, ctx_len, seq_len, ROW_TILE)
    rows_half = _Rows(n_batch, ctx_len, seq_len, ROW_TILE // 2)
    nct = rows.n_ctx_tiles

    pad = (-(n_batch + 1)) % V7X_SUBLANES
    cvec = jnp.concatenate([c, c_ctx[None, :], jnp.zeros((pad, d), F32)], axis=0)
    mods = _ada_call(cvec, w_ada, b_ada).reshape(depth, cvec.shape[0], N_MOD, d)

    w_ffn_in_b, w_ffn_out_b = w_ffn_in.astype(BF16), w_ffn_out.astype(BF16)
    rg_w_in_b, rg_w_out_b = rg_w_in.astype(BF16), rg_w_out.astype(BF16)
    fifth = jnp.arange(hg_w_in.shape[-1]) // d
    hg_w_in_b = (hg_w_in * jnp.where((fifth == 1) | (fifth == 2), 0.5, 1.0)).astype(BF16)
    hg_w_out_b = hg_w_out.astype(BF16)

    x_parts = [(ctx.reshape(n_batch * ctx_len, d), 0), (x.reshape(n_batch * seq_len, d), 0)]
    out = None
    for i in range(depth):
        last = i == depth - 1
        sh1, sc1, ga1, sh2, sc2, ga2 = [mods[i, :, k][:, None, :] for k in range(N_MOD)]
        j = i // N_MIXERS
        if i % N_MIXERS == 0:
            p = _proj_call(x_parts, g_mix[i][None], sh1, sc1, rg_w_in_b, j, rows)
            w_gates = (0.5 * jnp.concatenate([rg_w_a[j, 0], rg_w_i[j, 0], rg_w_a[j, 1], rg_w_i[j, 1]], axis=-1)).astype(BF16)
            b_gates = 0.5 * jnp.concatenate([rg_b_a[j, 0], rg_b_i[j, 0], rg_b_a[j, 1], rg_b_i[j, 1]], axis=-1)[:, None, :]
            y, y_ctx = _rg_call(p, rg_conv_w[j], rg_conv_b[j][None], w_gates, b_gates, rg_lam[j], need_ctx=not last, **dims)
            w_out = rg_w_out_b
        else:
            p = _proj_call(x_parts, g_mix[i][None], sh1, sc1, hg_w_in_b, j, rows)
            y, y_ctx = _hg_call(p, hg_lb, hg_norm[j][None], layer=i, need_ctx=not last, **dims)
            w_out = hg_w_out_b
        if last:
            xs = _outproj_latent(y, w_out, j, x_parts, ga1, rows_half)
            out = _ffn_call(xs, 0, g_ffn[i][None], sh2, sc2, ga2, w_ffn_in_b, w_ffn_out_b, i, rows, nct,
                            final_g=g_final[None])
        else:
            xs = _outproj_call([(y_ctx, 0), (y, 0)], w_out, j, x_parts, ga1, rows_half, 0)
            xs = _ffn_call(xs, 0, g_ffn[i][None], sh2, sc2, ga2, w_ffn_in_b, w_ffn_out_b, i, rows, 0)
            x_parts = [(xs, 0)]
    return out.reshape(n_batch, seq_len, d)


def _outproj_latent(y, w_out, j, x_parts, gate, rows):
    nct = rows.n_ctx_tiles
    if len(x_parts) == 2:
        x_lat = [(x_parts[1][0], -nct)]
    else:
        x_lat = [(x_parts[0][0], 0)]
    return _outproj_call([(y, -nct)], w_out, j, x_lat, gate, rows, nct)
```

```python
import functools

import jax
import jax.numpy as jnp
from jax import lax
from jax.experimental import pallas as pl
from jax.experimental.pallas import tpu as pltpu

F32 = jnp.float32
BF16 = jnp.bfloat16

EPS = 1e-6
RG_C = 8.0
GRID_W = 64
N_MOD = 6
N_MIXERS = 2
CONV_TAPS = 4
CONV_LEFT = 2

V7X_LANES = 128
V7X_SUBLANES = 8
V7X_VMEM_LIMIT_CAP = 57 * 1024 * 1024

HG_CHUNK = 32
EXP_CLAMP = 80.0
LOG2_E = 1.4426950408889634
ROW_CHUNK = 128
ROW_TILE = 1024


def _col_tile(n, preferred=512):
    t = preferred
    while n % t:
        t -= V7X_LANES
    return t


def _params(sem, nbytes):
    return pltpu.CompilerParams(dimension_semantics=sem, vmem_limit_bytes=int(min(V7X_VMEM_LIMIT_CAP, nbytes)))


class _Rows:
    def __init__(self, n_batch, ctx_len, seq_len, tm):
        assert (n_batch * ctx_len) % tm == 0 and seq_len % tm == 0
        self.n_batch, self.ctx_len, self.seq_len, self.tm = n_batch, ctx_len, seq_len, tm
        self.n_ctx_tiles = n_batch * ctx_len // tm
        self.tiles_per_seq = seq_len // tm
        self.n_lat_tiles = n_batch * self.tiles_per_seq
        self.n_tiles = self.n_ctx_tiles + self.n_lat_tiles

    def mod_row(self, tile):
        return jnp.where(tile < self.n_ctx_tiles, self.n_batch, (tile - self.n_ctx_tiles) // self.tiles_per_seq)


def _mod_spec(rows, first_tile, ncols, col_of_j):
    return pl.BlockSpec((None, 1, ncols), lambda i, j: (rows.mod_row(i + first_tile), 0, col_of_j(j)))


def _part_specs(parts, rows, first_tile, ncols, col_of_j, single_buffer_ctx=False):
    if len(parts) == 1:
        (_, off), = parts
        return [pl.BlockSpec((rows.tm, ncols), lambda i, j: (i + first_tile + off, col_of_j(j)))]
    assert first_tile == 0
    (_, off_c), (_, off_l) = parts
    nct = rows.n_ctx_tiles
    ctx_mode = dict(pipeline_mode=pl.Buffered(1)) if single_buffer_ctx else {}
    return [pl.BlockSpec((rows.tm, ncols), lambda i, j: (jnp.minimum(i, nct - 1) + off_c, col_of_j(j)), **ctx_mode),
            pl.BlockSpec((rows.tm, ncols), lambda i, j: (jnp.maximum(i - nct, 0) + off_l, col_of_j(j)))]


def _with_part(refs, n_ctx_tiles, fn):
    if len(refs) == 1:
        fn(refs[0])
        return
    is_ctx = pl.program_id(0) < n_ctx_tiles
    pl.when(is_ctx)(lambda: fn(refs[0]))
    pl.when(jnp.logical_not(is_ctx))(lambda: fn(refs[1]))


def _for_row_chunks(tm, fn):
    def body(r, _):
        fn(pl.ds(pl.multiple_of(r * ROW_CHUNK, ROW_CHUNK), ROW_CHUNK))
        return 0

    lax.fori_loop(0, tm // ROW_CHUNK, body, 0, unroll=4)


def _rms_modulate_into(h_ref, inv_ref, x_ref, g_ref, sh_ref, sc_ref, tm):
    def stats(rows):
        x = x_ref[rows, :]
        inv_ref[rows, :] = lax.rsqrt(jnp.mean(x * x, axis=-1, keepdims=True) + EPS)

    _for_row_chunks(tm, stats)
    gain = g_ref[...] * (1.0 + sc_ref[...])
    shift = sh_ref[...]

    def affine(rows):
        h_ref[rows, :] = (x_ref[rows, :] * inv_ref[rows, :] * gain + shift).astype(h_ref.dtype)

    _for_row_chunks(tm, affine)


def _ada_kernel(c_ref, w_ref, b_ref, o_ref):
    c = c_ref[...]
    s = (c * jax.nn.sigmoid(c)).astype(BF16)
    o_ref[...] = jnp.dot(s, w_ref[...].astype(BF16), preferred_element_type=F32) + b_ref[...]


def _ada_call(cvec, w_ada, b_ada):
    depth, d, n = w_ada.shape
    r = cvec.shape[0]
    tn = _col_tile(n)
    est = 2 * (d * tn * 4) + d * tn * 2 + 4 * r * (d + 2 * tn) * 4 + (4 << 20)
    return pl.pallas_call(
        _ada_kernel,
        grid=(depth, n // tn),
        in_specs=[
            pl.BlockSpec((r, d), lambda l, j: (0, 0)),
            pl.BlockSpec((None, d, tn), lambda l, j: (l, 0, j)),
            pl.BlockSpec((None, 1, tn), lambda l, j: (l, 0, j)),
        ],
        out_specs=pl.BlockSpec((None, r, tn), lambda l, j: (l, 0, j)),
        out_shape=jax.ShapeDtypeStruct((depth, r, n), F32),
        compiler_params=_params(("parallel", "parallel"), est),
        name="ada_mod",
    )(cvec, w_ada, b_ada.reshape(depth, 1, n))


def _proj_kernel(*refs, n_x, tm, n_ctx_tiles):
    x_refs = refs[:n_x]
    g_ref, sh_ref, sc_ref, w_ref, o_ref, h_ref, inv_ref = refs[n_x:]

    @pl.when(pl.program_id(1) == 0)
    def _():
        _with_part(x_refs, n_ctx_tiles, lambda x_ref: _rms_modulate_into(h_ref, inv_ref, x_ref, g_ref, sh_ref, sc_ref, tm))

    acc = jnp.dot(h_ref[...], w_ref[...], preferred_element_type=F32)
    for s in range(o_ref.shape[0]):
        o_ref[s] = acc[:, s * V7X_LANES:(s + 1) * V7X_LANES]


def _proj_call(x_parts, g, shift, scale, w, layer, rows):
    d = x_parts[0][0].shape[1]
    n = w.shape[2]
    nx = len(x_parts)
    tm, tn = rows.tm, _col_tile(n, 1024)
    x_bufs = 2 if nx == 1 else 3
    est = x_bufs * tm * d * 4 + tm * d * 2 + 2 * d * tn * 2 + 3 * tm * tn * 4 + 8 * ROW_CHUNK * d * 4 + (3 << 20)
    kern = functools.partial(_proj_kernel, n_x=nx, tm=tm, n_ctx_tiles=rows.n_ctx_tiles)
    full = lambda j: 0
    return pl.pallas_call(
        kern,
        grid=(rows.n_tiles, n // tn),
        in_specs=[*_part_specs(x_parts, rows, 0, d, full, single_buffer_ctx=True),
                  pl.BlockSpec((1, d), lambda i, j: (0, 0)),
                  _mod_spec(rows, 0, d, full), _mod_spec(rows, 0, d, full),
                  pl.BlockSpec((None, d, tn), lambda i, j: (layer, 0, j))],
        out_specs=pl.BlockSpec((tn // V7X_LANES, tm, V7X_LANES), lambda i, j: (j, i, 0)),
        out_shape=jax.ShapeDtypeStruct((n // V7X_LANES, rows.n_tiles * tm, V7X_LANES), F32),
        scratch_shapes=[pltpu.VMEM((tm, d), BF16), pltpu.VMEM((tm, 1), F32)],
        compiler_params=_params(("parallel", "arbitrary"), est),
        name="norm_mod_proj",
    )(*[a for a, _ in x_parts], g, shift, scale, w)


def _outproj_kernel(*refs, n_a, n_x, n_ctx_tiles):
    a_refs, x_refs = refs[:n_a], refs[n_a:n_a + n_x]
    w_ref, ga_ref, o_ref = refs[n_a + n_x:]

    def with_a(a_ref):
        acc = jnp.dot(a_ref[...], w_ref[...], preferred_element_type=F32) * ga_ref[...]

        def with_x(x_ref):
            o_ref[...] = x_ref[...] + acc

        _with_part(x_refs, n_ctx_tiles, with_x)

    _with_part(a_refs, n_ctx_tiles, with_a)


def _outproj_call(a_parts, w, layer, x_parts, gate, rows, first_tile):
    k, n = w.shape[1], w.shape[2]
    tm, tn = rows.tm, _col_tile(n, 2048)
    na, nx = len(a_parts), len(x_parts)
    n_tiles = rows.n_tiles - first_tile
    est = 2 * na * tm * k * 2 + 2 * k * tn * 2 + 2 * (nx + 1) * tm * tn * 4 + 3 * tm * tn * 4 + (3 << 20)
    kern = functools.partial(_outproj_kernel, n_a=na, n_x=nx, n_ctx_tiles=rows.n_ctx_tiles)
    return pl.pallas_call(
        kern,
        grid=(n_tiles, n // tn),
        in_specs=[*_part_specs(a_parts, rows, first_tile, k, lambda j: 0),
                  *_part_specs(x_parts, rows, first_tile, tn, lambda j: j),
                  pl.BlockSpec((None, k, tn), lambda i, j: (layer, 0, j)),
                  _mod_spec(rows, first_tile, tn, lambda j: j)],
        out_specs=pl.BlockSpec((tm, tn), lambda i, j: (i, j)),
        out_shape=jax.ShapeDtypeStruct((n_tiles * tm, n), F32),
        compiler_params=_params(("parallel", "parallel"), est),
        name="outproj_residual",
    )(*[a for a, _ in a_parts], *[a for a, _ in x_parts], w, gate)


def _ffn_kernel(*refs, tm, final_norm):
    if final_norm:
        x_ref, g_ref, sh_ref, sc_ref, ga_ref, wg_ref, wu_ref, wo_ref, gf_ref, o_ref, h_ref, inv_ref = refs
    else:
        x_ref, g_ref, sh_ref, sc_ref, ga_ref, wg_ref, wu_ref, wo_ref, o_ref, h_ref, inv_ref = refs
    f = pl.program_id(1)

    @pl.when(f == 0)
    def _():
        _rms_modulate_into(h_ref, inv_ref, x_ref, g_ref, sh_ref, sc_ref, tm)
        o_ref[...] = jnp.zeros_like(o_ref)

    tn = _col_tile(o_ref.shape[1])
    half = tm // 2
    for r0 in range(0, tm, half):
        h = h_ref[r0:r0 + half, :]
        gate = jnp.dot(h, wg_ref[...], preferred_element_type=F32)
        up = jnp.dot(h, wu_ref[...], preferred_element_type=F32)
        act = (gate * jax.nn.sigmoid(gate) * up).astype(BF16)
        for c0 in range(0, o_ref.shape[1], tn):
            o_ref[r0:r0 + half, c0:c0 + tn] += jnp.dot(act, wo_ref[:, c0:c0 + tn], preferred_element_type=F32)

    @pl.when(f == pl.num_programs(1) - 1)
    def _():
        res_gate = ga_ref[...]

        def chunk(rows):
            y = x_ref[rows, :] + res_gate * o_ref[rows, :]
            if final_norm:
                y = y * lax.rsqrt(jnp.mean(y * y, axis=-1, keepdims=True) + EPS) * gf_ref[...]
            o_ref[rows, :] = y

        _for_row_chunks(tm, chunk)


def _ffn_call(xs, x_first_tile, g, shift, scale, res_gate, w_in, w_out, layer, rows, first_tile, final_g=None):
    d = xs.shape[1]
    ff = w_out.shape[1]
    tm, tf = rows.tm, _col_tile(ff)
    nf = ff // tf
    n_tiles = rows.n_tiles - first_tile
    est = (4 * tm * d * 4 + tm * d * 2 + 2 * (2 * d * tf * 2 + tf * d * 2) + 2 * tm * tf * 4 + tm * tf * 2
           + 8 * ROW_CHUNK * d * 4 + (3 << 20))
    kern = functools.partial(_ffn_kernel, tm=tm, final_norm=final_g is not None)
    full = lambda f: 0
    extra_specs = [] if final_g is None else [pl.BlockSpec((1, d), lambda i, f: (0, 0))]
    extra_args = [] if final_g is None else [final_g]
    return pl.pallas_call(
        kern,
        grid=(n_tiles, nf),
        in_specs=[pl.BlockSpec((tm, d), lambda i, f: (i + x_first_tile, 0)),
                  pl.BlockSpec((1, d), lambda i, f: (0, 0)),
                  _mod_spec(rows, first_tile, d, full), _mod_spec(rows, first_tile, d, full),
                  _mod_spec(rows, first_tile, d, full),
                  pl.BlockSpec((None, d, tf), lambda i, f: (layer, 0, f)),
                  pl.BlockSpec((None, d, tf), lambda i, f: (layer, 0, nf + f)),
                  pl.BlockSpec((None, tf, d), lambda i, f: (layer, f, 0)),
                  *extra_specs],
        out_specs=pl.BlockSpec((tm, d), lambda i, f: (i, 0)),
        out_shape=jax.ShapeDtypeStruct((n_tiles * tm, d), F32),
        scratch_shapes=[pltpu.VMEM((tm, d), BF16), pltpu.VMEM((tm, 1), F32)],
        compiler_params=_params(("parallel", "arbitrary"), est),
        name="swiglu_ffn",
    )(xs, g, shift, scale, res_gate, w_in, w_in, w_out, *extra_args)


def _softplus(x):
    return jnp.maximum(x, 0.0) + jnp.log1p(jnp.exp(-jnp.abs(x)))


def _chunk_carries(h_loc, p_loc, h_init, reverse):
    n = h_loc.shape[0]
    carries = [None] * n
    c = h_init
    for s in (reversed(range(n)) if reverse else range(n)):
        carries[s] = c
        c = p_loc[s:s + 1] * c + h_loc[s:s + 1]
    return carries, c


def _strided_scan(coef, res, n, stride):
    nsub = V7X_SUBLANES
    af_ref, bf_ref, ab_ref, bb_ref = coef
    pf_ref, hf_ref, pb_ref, hb_ref = res
    lanes = af_ref.shape[1]

    def body(i, carry):
        hf, pf, hb, pb = carry
        rows_f = pl.ds(i, nsub, stride=stride)
        rows_b = pl.ds(n - 1 - i, nsub, stride=stride)
        a = af_ref[rows_f, :]
        hf = a * hf + bf_ref[rows_f, :]
        pf = a * pf
        hf_ref[rows_f, :] = hf
        pf_ref[rows_f, :] = pf
        a = ab_ref[rows_b, :]
        hb = a * hb + bb_ref[rows_b, :]
        pb = a * pb
        hb_ref[rows_b, :] = hb
        pb_ref[rows_b, :] = pb
        return hf, pf, hb, pb

    zero = jnp.zeros((nsub, lanes), F32)
    one = jnp.ones((nsub, lanes), F32)
    return lax.fori_loop(0, n, body, (zero, one, zero, one), unroll=8)


def _rg_kernel(*refs, ctx_len, seq_len, need_ctx):
    pc_ref, pl_ref, cw_ref, cb_ref, wg_ref, bg_ref, lam_ref = refs[:7]
    xc_ref, gc_ref = pc_ref.at[0], pc_ref.at[1]
    xl_ref, gl_ref = pl_ref.at[0], pl_ref.at[1]
    n_out = 2 if need_ctx else 1
    y_ref = refs[7]
    yc_ref = refs[8] if need_ctx else None
    (xp_ref, laf, lbf, lab, lbb, lpf, lhf, lpb, lhb, caf, cbf, cab, cbb, cpf, chf, cpb, chb, carry_ref) = refs[7 + n_out:]
    nsub = V7X_SUBLANES
    lanes = xc_ref.shape[1]
    rb = ctx_len
    nblk = 1 + seq_len // rb
    halo = nsub
    lat_stride = rb + nsub
    ctx_n = ctx_len // nsub
    ctx_stride = ctx_n + nsub

    xp_ref[pl.ds(0, halo), :] = jnp.zeros((halo, lanes), F32)
    xp_ref[pl.ds(halo + ctx_len + seq_len, halo), :] = jnp.zeros((halo, lanes), F32)
    xp_ref[pl.ds(halo, ctx_len), :] = xc_ref[...]
    xp_ref[pl.ds(halo + ctx_len, seq_len), :] = xl_ref[...]

    cw = cw_ref[...]
    cb = cb_ref[...]
    softplus_neg_lam = _softplus(-lam_ref[...])
    local_row = lax.broadcasted_iota(jnp.int32, (rb, 1), 0)

    def coeffs(blk):
        start = blk * rb if isinstance(blk, int) else pl.multiple_of(blk * rb, rb)
        seg_first = blk <= 1
        seg_last = jnp.logical_or(blk == 0, blk == nblk - 1)
        taps = []
        for k in range(CONV_TAPS):
            off = k - CONV_LEFT
            tap = xp_ref[pl.ds(start + halo + off, rb), :]
            if off < 0:
                tap = jnp.where(jnp.logical_or(local_row >= -off, jnp.logical_not(seg_first)), tap, 0.0)
            elif off > 0:
                tap = jnp.where(jnp.logical_or(local_row < rb - off, jnp.logical_not(seg_last)), tap, 0.0)
            taps.append(tap)
        xc = cb
        for k in range(CONV_TAPS):
            xc = xc + cw[k:k + 1] * taps[k]
        half_z = jnp.dot(xc.astype(BF16), wg_ref[...], preferred_element_type=F32) + bg_ref[...]
        half_xc = 0.5 * xc
        out = []
        for d in range(2):
            th_r = jnp.tanh(half_z[:, (2 * d) * lanes:(2 * d + 1) * lanes])
            th_i = jnp.tanh(half_z[:, (2 * d + 1) * lanes:(2 * d + 2) * lanes])
            scale = (-0.5 * RG_C) * softplus_neg_lam[d:d + 1]
            log_a = scale * th_r + scale
            a = jnp.exp(log_a)
            t = jnp.tanh(log_a)
            u = (-2.0) * t / (1.0 - t)
            mult = jnp.where(u > 0.0, u * lax.rsqrt(u), 0.0)
            out.append((a, mult * half_xc * (1.0 + th_i)))
        return out

    (a_f, b_f), (a_b, b_b) = coeffs(0)
    for c in range(nsub):
        dst = pl.ds(c * ctx_stride, ctx_n)
        src = slice(c * ctx_n, (c + 1) * ctx_n)
        caf[dst, :] = a_f[src]
        cbf[dst, :] = b_f[src]
        cab[dst, :] = a_b[src]
        cbb[dst, :] = b_b[src]

    def fill(blk, _):
        (a_f, b_f), (a_b, b_b) = coeffs(blk)
        dst = pl.ds(pl.multiple_of((blk - 1) * lat_stride, nsub), rb)
        laf[dst, :] = a_f
        lbf[dst, :] = b_f
        lab[dst, :] = a_b
        lbb[dst, :] = b_b
        return 0

    lax.fori_loop(1, nblk, fill, 0, unroll=4)

    zero_state = jnp.zeros((1, lanes), F32)
    hf, pf, hb, pb = _strided_scan((caf, cbf, cab, cbb), (cpf, chf, cpb, chb), ctx_n, ctx_stride)
    ctx_in_f, ctx_state_f = _chunk_carries(hf, pf, zero_state, reverse=False)
    ctx_in_b, ctx_state_b = _chunk_carries(hb, pb, zero_state, reverse=True)
    hf, pf, hb, pb = _strided_scan((laf, lbf, lab, lbb), (lpf, lhf, lpb, lhb), rb, lat_stride)
    lat_in_f, _ = _chunk_carries(hf, pf, ctx_state_f, reverse=False)
    lat_in_b, _ = _chunk_carries(hb, pb, ctx_state_b, reverse=True)
    for s in range(nsub):
        carry_ref[pl.ds(s, 1), :] = lat_in_f[s]
        carry_ref[pl.ds(nsub + s, 1), :] = lat_in_b[s]

    if need_ctx:
        for c in range(nsub):
            src = pl.ds(c * ctx_stride, ctx_n)
            h = chf[src, :] + cpf[src, :] * ctx_in_f[c] + chb[src, :] + cpb[src, :] * ctx_in_b[c]
            rows = pl.ds(c * ctx_n, ctx_n)
            yc_ref[rows, :] = (h * jax.nn.gelu(gc_ref[rows, :])).astype(yc_ref.dtype)

    def emit(blk, _):
        src = pl.ds(pl.multiple_of((blk - 1) * lat_stride, nsub), rb)
        cf = carry_ref[pl.ds(blk - 1, 1), :]
        cbw = carry_ref[pl.ds(nsub + blk - 1, 1), :]
        h = lhf[src, :] + lpf[src, :] * cf + lhb[src, :] + lpb[src, :] * cbw
        rows = pl.ds(pl.multiple_of((blk - 1) * rb, rb), rb)
        y_ref[rows, :] = (h * jax.nn.gelu(gl_ref[rows, :])).astype(y_ref.dtype)
        return 0

    lax.fori_loop(1, nblk, emit, 0, unroll=4)


def _mixer_out(t, w, n_batch, ctx_len, seq_len, lanes, need_ctx):
    specs = [pl.BlockSpec((seq_len, lanes), lambda b, h: (b, h))]
    shapes = [jax.ShapeDtypeStruct((n_batch * seq_len, w), BF16)]
    if need_ctx:
        specs.append(pl.BlockSpec((ctx_len, lanes), lambda b, h: (b, h)))
        shapes.append(jax.ShapeDtypeStruct((n_batch * ctx_len, w), BF16))
    return specs, shapes


def _rg_call(p, conv_w, conv_b, w_gates, b_gates, lam, *, n_batch, ctx_len, seq_len, need_ctx):
    lanes = V7X_LANES
    t = p.shape[1]
    nh = p.shape[0] // 2
    w = nh * lanes
    nsub = V7X_SUBLANES
    lb = ctx_len + seq_len
    lat_block0 = n_batch * ctx_len // seq_len
    lat_rows = nsub * (seq_len // nsub + nsub)
    ctx_rows = nsub * (ctx_len // nsub + nsub)
    est = (4 * lb * lanes * 4 + 2 * lb * lanes * 2 + (lb + 2 * nsub) * lanes * 4 + 8 * (lat_rows + ctx_rows) * lanes * 4
           + 48 * ctx_len * lanes * 4 + (4 << 20))
    kern = functools.partial(_rg_kernel, ctx_len=ctx_len, seq_len=seq_len, need_ctx=need_ctx)
    out_specs, out_shapes = _mixer_out(t, w, n_batch, ctx_len, seq_len, lanes, need_ctx)
    outs = pl.pallas_call(
        kern,
        grid=(n_batch, nh),
        in_specs=[pl.BlockSpec((2, ctx_len, lanes), lambda b, h: (h, b, 0)),
                  pl.BlockSpec((2, seq_len, lanes), lambda b, h: (h, lat_block0 + b, 0)),
                  pl.BlockSpec((CONV_TAPS, lanes), lambda b, h: (0, h)),
                  pl.BlockSpec((1, lanes), lambda b, h: (0, h)),
                  pl.BlockSpec((None, lanes, 4 * lanes), lambda b, h: (h, 0, 0)),
                  pl.BlockSpec((None, 1, 4 * lanes), lambda b, h: (h, 0, 0)),
                  pl.BlockSpec((2, lanes), lambda b, h: (0, h))],
        out_specs=out_specs,
        out_shape=out_shapes,
        scratch_shapes=[pltpu.VMEM((lb + 2 * nsub, lanes), F32)]
        + [pltpu.VMEM((lat_rows, lanes), F32)] * 8
        + [pltpu.VMEM((ctx_rows, lanes), F32)] * 8
        + [pltpu.VMEM((2 * nsub, lanes), F32)],
        compiler_params=_params(("parallel", "parallel"), est),
        name="rglru_mixer",
    )(p, p, conv_w, conv_b, w_gates, b_gates, lam)
    return outs if need_ctx else (outs[0], None)


def _group_cumsum(x, group, reverse):
    n = x.shape[0]
    pos = lax.broadcasted_iota(jnp.int32, (n, 1), 0) % group
    step = 1
    while step < group:
        if reverse:
            shifted = pltpu.roll(x, n - step, axis=0)
            keep = pos < group - step
        else:
            shifted = pltpu.roll(x, step, axis=0)
            keep = pos >= step
        x = x + jnp.where(keep, shifted, 0.0)
        step *= 2
    return x


def _hg_kernel(*refs, ctx_len, seq_len, layer, dk, unroll, need_ctx):
    pc_ref, pl_ref, lb_ref, nw_ref = refs[:4]
    n_out = 2 if need_ctx else 1
    y_ref = refs[4]
    yc_ref = refs[5] if need_ctx else None
    (pitch, qs, fs, bs, vs, qd_f, qd_b, oacc, kv_f, kv_b, st_f, st_b,
     k_slow, cum_slow, o_slow, deepest_ref) = refs[4 + n_out:]
    lanes = qs.shape[1]
    ch = HG_CHUNK
    lb_rows = ctx_len + seq_len
    nchunk = lb_rows // ch
    ctx_chunks = ctx_len // ch
    sb = ctx_len
    nsb = lb_rows // sb
    cps = sb // ch
    ncol = seq_len // ch
    grid_pitch = ncol + V7X_SUBLANES
    qd_refs, kv_refs, st_refs = (qd_f, qd_b), (kv_f, kv_b), (st_f, st_b)

    def to_scan_order(ctx_src, lat_src, dst_ref):
        dst_ref[pl.ds(0, ctx_len), :] = ctx_src[...]

        def spread(r, _):
            src = pl.ds(pl.multiple_of(r * ncol, ncol), ncol)
            pitch[pl.ds(pl.multiple_of(r * grid_pitch, V7X_SUBLANES), ncol), :] = lat_src[src, :]
            return 0

        lax.fori_loop(0, ch, spread, 0, unroll=True)

        def column(n, _):
            dst = pl.ds(pl.multiple_of(ctx_len + n * ch, ch), ch)
            dst_ref[dst, :] = pitch[pl.ds(n, ch, stride=grid_pitch), :]
            return 0

        lax.fori_loop(0, ncol, column, 0, unroll=True)

    for k, dst_ref in enumerate((qs, fs, bs, vs)):
        to_scan_order(pc_ref.at[k], pl_ref.at[k], dst_ref)
    gc_ref, gl_ref = pc_ref.at[4], pl_ref.at[4]

    lbp = lb_ref[...]
    e = jnp.exp(lbp - jnp.max(lbp, axis=0, keepdims=True))
    sm = e / jnp.sum(e, axis=0, keepdims=True)
    lower = jnp.zeros(sm.shape[1:], F32)
    for l in range(1, layer + 1):
        lower = lower + sm[l]

    q_scale = dk ** -0.5
    j_idx = lax.broadcasted_iota(jnp.int32, (cps, ch, ch), 1)
    s_idx = lax.broadcasted_iota(jnp.int32, (cps, ch, ch), 2)
    causal = (j_idx >= s_idx, j_idx <= s_idx)
    aux = 2 * V7X_SUBLANES
    aux_row = lax.broadcasted_iota(jnp.int32, (1, aux, 1), 1)

    def bf16_pieces(x):
        hi = x.astype(BF16)
        rest = x - hi.astype(F32)
        mid = rest.astype(BF16)
        lo = (rest - mid.astype(F32)).astype(BF16)
        return hi, mid, lo

    def gate_feats(rows, d):
        lo = lower[d:d + 1]
        half_open = 0.5 * (1.0 - lo)
        th = jnp.tanh((bs if d else fs)[rows, :])
        log_f = jnp.log(lo + half_open * (1.0 + th))
        k = half_open * (1.0 - th)
        return k, _group_cumsum(log_f, ch, reverse=bool(d))

    def exact_intra(rows):
        q3 = (qs[rows, :] * q_scale).reshape(cps, ch, lanes)
        v3 = vs[rows, :].reshape(cps, ch, lanes)
        pos = lax.broadcasted_iota(jnp.int32, (1, ch, 1), 1)
        for d in range(2):
            k, cum = gate_feats(rows, d)
            k_slow[...] = k.reshape(cps, ch, lanes)
            cum_slow[...] = cum.reshape(cps, ch, lanes)

            def row_j(j, _):
                cum3 = cum_slow[...]
                w = jnp.exp(jnp.minimum(cum_slow[:, pl.ds(j, 1), :] - cum3, 0.0))
                valid = (pos >= j) if d else (pos <= j)
                q_j = jnp.where(pos == j, q3, 0.0).sum(axis=1, keepdims=True)
                col = jnp.sum(jnp.where(valid, q_j * k_slow[...] * w, 0.0), axis=-1, keepdims=True)
                o_j = jnp.sum(col * v3, axis=1, keepdims=True)
                if d:
                    o_slow[:, pl.ds(j, 1), :] += o_j
                else:
                    o_slow[:, pl.ds(j, 1), :] = o_j
                return 0

            lax.fori_loop(0, ch, row_j, 0)
        oacc[rows, :] = o_slow[...].reshape(sb, lanes)

    def local_part(i, _):
        rows = pl.ds(pl.multiple_of(i * sb, sb), sb)
        q = qs[rows, :] * q_scale
        v3 = vs[rows, :].reshape(cps, ch, lanes).astype(BF16)
        zeros_v = jnp.zeros((cps, ch, lanes), BF16)
        pad_rows = jnp.concatenate([jnp.zeros((cps, aux, lanes), BF16), jnp.ones((cps, aux, lanes), BF16)], axis=2)
        kv_rhs = jnp.concatenate([jnp.concatenate([v3, zeros_v], axis=2), pad_rows], axis=1)
        o_sum = None
        deepest = None
        for d in range(2):
            k, cum = gate_feats(rows, d)
            cum_l = cum * LOG2_E
            cum_l3 = cum_l.reshape(cps, ch, lanes)
            total_l3 = cum_l3[:, 0:1, :] if d else cum_l3[:, ch - 1:ch, :]
            qd = (q * jnp.exp2(cum_l)).astype(BF16)
            kd = (k * jnp.exp2(jnp.minimum(cum * (-LOG2_E), EXP_CLAMP * LOG2_E))).astype(BF16)
            kl = (k.reshape(cps, ch, lanes) * jnp.exp2(total_l3 - cum_l3)).astype(BF16)
            att = jnp.einsum("cjd,csd->cjs", qd.reshape(cps, ch, lanes), kd.reshape(cps, ch, lanes),
                             preferred_element_type=F32)
            att = jnp.where(causal[d], att, 0.0)
            o_intra = jnp.einsum("cjs,cse->cje", att.astype(BF16), v3, preferred_element_type=F32)
            o_sum = o_intra if o_sum is None else o_sum + o_intra
            qd_refs[d][rows, :] = qd
            d_hi, d_mid, d_lo = [p.astype(F32) for p in bf16_pieces(jnp.exp2(total_l3))]
            dec_rows = jnp.where(aux_row == 0, d_hi, jnp.where(aux_row == 1, d_mid, jnp.where(aux_row == 2, d_lo, 0.0)))
            kv_lhs = jnp.concatenate([kl, dec_rows.astype(BF16)], axis=1)
            for c in range(cps):
                kv_refs[d][i * cps + c] = lax.dot_general(kv_lhs[c], kv_rhs[c], (((0,), (0,)), ((), ())),
                                                          preferred_element_type=F32)
            low = jnp.min(total_l3, axis=0)
            deepest = low if deepest is None else jnp.minimum(deepest, low)
        oacc[rows, :] = o_sum.reshape(sb, lanes)
        deepest_ref[i] = jnp.min(deepest)
        return 0

    lax.fori_loop(0, nsb, local_part, 0, unroll=unroll)

    def redo_unsafe(i, _):
        @pl.when(deepest_ref[i] < -EXP_CLAMP * LOG2_E)
        def _():
            exact_intra(pl.ds(pl.multiple_of(i * sb, sb), sb))

        return 0

    lax.fori_loop(0, nsb, redo_unsafe, 0)

    def state_step(d, n, st):
        st_refs[d][n] = st.astype(BF16)
        kv = kv_refs[d][n]
        return kv[:, lanes:] * st + kv[:, :lanes]

    zero_state = jnp.zeros((lanes, lanes), F32)
    lax.fori_loop(0, nchunk, functools.partial(state_step, 0), zero_state, unroll=8)
    st = lax.fori_loop(0, ctx_chunks, lambda i, s: state_step(1, ctx_chunks - 1 - i, s), zero_state, unroll=8)
    lax.fori_loop(0, nchunk - ctx_chunks, lambda i, s: state_step(1, nchunk - 1 - i, s), st, unroll=8)

    nw = nw_ref[...]

    def inter_part(i, _):
        rows = pl.ds(pl.multiple_of(i * sb, sb), sb)
        o = oacc[rows, :]
        for d in range(2):
            qd3 = qd_refs[d][rows, :].reshape(cps, ch, lanes)
            st3 = st_refs[d][pl.ds(i * cps, cps)]
            o = o + jnp.einsum("cjd,cde->cje", qd3, st3, preferred_element_type=F32).reshape(sb, lanes)
        oacc[rows, :] = o * lax.rsqrt(jnp.mean(o * o, axis=-1, keepdims=True) + EPS) * nw
        return 0

    first_sb = 0 if need_ctx else ctx_len // sb
    lax.fori_loop(first_sb, nsb, inter_part, 0, unroll=True)

    def gated(o, g):
        return (o * (g * jax.nn.sigmoid(g))).astype(y_ref.dtype)

    if need_ctx:
        yc_ref[...] = gated(oacc[pl.ds(0, ctx_len), :], gc_ref[...])

    def column(n, _):
        src = pl.ds(pl.multiple_of(ctx_len + n * ch, ch), ch)
        pitch[pl.ds(n, ch, stride=grid_pitch), :] = oacc[src, :]
        return 0

    lax.fori_loop(0, ncol, column, 0, unroll=True)

    def gather_row(r, _):
        dst = pl.ds(pl.multiple_of(r * ncol, ncol), ncol)
        y_ref[dst, :] = gated(pitch[pl.ds(pl.multiple_of(r * grid_pitch, V7X_SUBLANES), ncol), :], gl_ref[dst, :])
        return 0

    lax.fori_loop(0, ch, gather_row, 0, unroll=True)


def _hg_call(p, hg_lb, norm_w, *, n_batch, ctx_len, seq_len, layer, need_ctx):
    lanes = V7X_LANES
    t = p.shape[1]
    nh = p.shape[0] // 5
    w = nh * lanes
    lb = ctx_len + seq_len
    lat_block0 = n_batch * ctx_len // seq_len
    nchunk = lb // HG_CHUNK
    depth = hg_lb.shape[0]
    blk_bytes = lb * lanes * 4
    cps = ctx_len // HG_CHUNK
    pitch_rows = HG_CHUNK * (seq_len // HG_CHUNK + V7X_SUBLANES)
    unroll = lb // ctx_len
    est = (10 * blk_bytes + 2 * lb * lanes * 2 + 8 * blk_bytes + pitch_rows * lanes * 4 + 2 * nchunk * lanes * lanes * 10
           + unroll * 40 * ctx_len * lanes * 4 + (4 << 20))
    kern = functools.partial(_hg_kernel, ctx_len=ctx_len, seq_len=seq_len, layer=layer, dk=lanes, unroll=unroll,
                             need_ctx=need_ctx)
    in_specs = [pl.BlockSpec((5, ctx_len, lanes), lambda b, h: (h, b, 0)),
                pl.BlockSpec((5, seq_len, lanes), lambda b, h: (h, lat_block0 + b, 0))]
    out_specs, out_shapes = _mixer_out(t, w, n_batch, ctx_len, seq_len, lanes, need_ctx)
    outs = pl.pallas_call(
        kern,
        grid=(n_batch, nh),
        in_specs=in_specs + [pl.BlockSpec((depth, 2, lanes), lambda b, h: (0, 0, h)),
                             pl.BlockSpec((1, lanes), lambda b, h: (0, 0))],
        out_specs=out_specs,
        out_shape=out_shapes,
        scratch_shapes=[pltpu.VMEM((pitch_rows, lanes), F32)]
        + [pltpu.VMEM((lb, lanes), F32)] * 4
        + [pltpu.VMEM((lb, lanes), BF16)] * 2
        + [pltpu.VMEM((lb, lanes), F32)]
        + [pltpu.VMEM((nchunk, lanes, 2 * lanes), F32)] * 2
        + [pltpu.VMEM((nchunk, lanes, lanes), BF16)] * 2
        + [pltpu.VMEM((cps, HG_CHUNK, lanes), F32)] * 3
        + [pltpu.SMEM((lb // ctx_len,), F32)],
        compiler_params=_params(("parallel", "parallel"), est),
        name="hgrn2_mixer",
    )(p, p, hg_lb, norm_w)
    return outs if need_ctx else (outs[0], None)


def kernel(x, c, ctx, c_ctx, w_ada, b_ada, g_mix, g_ffn, g_final, w_ffn_in, w_ffn_out,
           rg_w_in, rg_conv_w, rg_conv_b, rg_w_a, rg_b_a, rg_w_i, rg_b_i, rg_lam, rg_w_out,
           hg_w_in, hg_lb, hg_norm, hg_w_out):
    n_batch, seq_len, d = x.shape
    ctx_len = ctx.shape[1]
    depth = w_ada.shape[0]
    lanes = V7X_LANES
    assert seq_len == V7X_SUBLANES * ctx_len, "row blocking assumes the latent is 8 context lengths long"
    assert seq_len // GRID_W == HG_CHUNK, "an HGRN2 chunk must be one latent grid column"
    assert (n_batch * ctx_len) % seq_len == 0, "latent blocks must stay block-aligned behind the context rows"
    assert d % lanes == 0 and rg_w_a.shape[-1] == lanes and hg_norm.shape[-1] == lanes
    dims = dict(n_batch=n_batch, ctx_len=ctx_len, seq_len=seq_len)
    rows = _Rows(n_batch, ctx_len, seq_len, ROW_TILE)
    rows_half = _Rows(n_batch, ctx_len, seq_len, ROW_TILE // 2)
    nct = rows.n_ctx_tiles

    pad = (-(n_batch + 1)) % V7X_SUBLANES
    cvec = jnp.concatenate([c, c_ctx[None, :], jnp.zeros((pad, d), F32)], axis=0)
    mods = _ada_call(cvec, w_ada, b_ada).reshape(depth, cvec.shape[0], N_MOD, d)

    w_ffn_in_b, w_ffn_out_b = w_ffn_in.astype(BF16), w_ffn_out.astype(BF16)
    def head_major(w_in, groups, scale=None):
        n_l, d_in, n = w_in.shape
        w_in = w_in.reshape(n_l, d_in, groups, n // (groups * lanes), lanes)
        if scale is not None:
            w_in = w_in * scale.reshape(groups, 1, 1)
        return w_in.transpose(0, 1, 3, 2, 4).reshape(n_l, d_in, n).astype(BF16)

    rg_w_in_b, rg_w_out_b = head_major(rg_w_in, 2), rg_w_out.astype(BF16)
    hg_w_in_b = head_major(hg_w_in, 5, jnp.array([1.0, 0.5, 0.5, 1.0, 1.0], F32))
    hg_w_out_b = hg_w_out.astype(BF16)

    x_parts = [(ctx.reshape(n_batch * ctx_len, d), 0), (x.reshape(n_batch * seq_len, d), 0)]
    out = None
    for i in range(depth):
        last = i == depth - 1
        sh1, sc1, ga1, sh2, sc2, ga2 = [mods[i, :, k][:, None, :] for k in range(N_MOD)]
        j = i // N_MIXERS
        if i % N_MIXERS == 0:
            p = _proj_call(x_parts, g_mix[i][None], sh1, sc1, rg_w_in_b, j, rows)
            w_gates = (0.5 * jnp.concatenate([rg_w_a[j, 0], rg_w_i[j, 0], rg_w_a[j, 1], rg_w_i[j, 1]], axis=-1)).astype(BF16)
            b_gates = 0.5 * jnp.concatenate([rg_b_a[j, 0], rg_b_i[j, 0], rg_b_a[j, 1], rg_b_i[j, 1]], axis=-1)[:, None, :]
            y, y_ctx = _rg_call(p, rg_conv_w[j], rg_conv_b[j][None], w_gates, b_gates, rg_lam[j], need_ctx=not last, **dims)
            w_out = rg_w_out_b
        else:
            p = _proj_call(x_parts, g_mix[i][None], sh1, sc1, hg_w_in_b, j, rows)
            y, y_ctx = _hg_call(p, hg_lb, hg_norm[j][None], layer=i, need_ctx=not last, **dims)
            w_out = hg_w_out_b
        if last:
            xs = _outproj_latent(y, w_out, j, x_parts, ga1, rows_half)
            out = _ffn_call(xs, 0, g_ffn[i][None], sh2, sc2, ga2, w_ffn_in_b, w_ffn_out_b, i, rows, nct,
                            final_g=g_final[None])
        else:
            xs = _outproj_call([(y_ctx, 0), (y, 0)], w_out, j, x_parts, ga1, rows_half, 0)
            xs = _ffn_call(xs, 0, g_ffn[i][None], sh2, sc2, ga2, w_ffn_in_b, w_ffn_out_b, i, rows, 0)
            x_parts = [(xs, 0)]
    return out.reshape(n_batch, seq_len, d)


def _outproj_latent(y, w_out, j, x_parts, gate, rows):
    nct = rows.n_ctx_tiles
    if len(x_parts) == 2:
        x_lat = [(x_parts[1][0], -nct)]
    else:
        x_lat = [(x_parts[0][0], 0)]
    return _outproj_call([(y, -nct)], w_out, j, x_lat, gate, rows, nct)
```

```python
import functools

import jax
import jax.numpy as jnp
from jax import lax
from jax.experimental import pallas as pl
from jax.experimental.pallas import tpu as pltpu

F32 = jnp.float32
BF16 = jnp.bfloat16

EPS = 1e-6
RG_C = 8.0
GRID_W = 64
N_MOD = 6
N_MIXERS = 2
CONV_TAPS = 4
CONV_LEFT = 2

V7X_LANES = 128
V7X_SUBLANES = 8
V7X_VMEM_LIMIT_CAP = 57 * 1024 * 1024

HG_CHUNK = 32
EXP_CLAMP = 80.0
LOG2_E = 1.4426950408889634
ROW_CHUNK = 128
ROW_TILE = 1024


def _col_tile(n, preferred=512):
    t = preferred
    while n % t:
        t -= V7X_LANES
    return t


def _params(sem, nbytes):
    return pltpu.CompilerParams(dimension_semantics=sem, vmem_limit_bytes=int(min(V7X_VMEM_LIMIT_CAP, nbytes)))


class _Rows:
    def __init__(self, n_batch, ctx_len, seq_len, tm):
        assert (n_batch * ctx_len) % tm == 0 and seq_len % tm == 0
        self.n_batch, self.ctx_len, self.seq_len, self.tm = n_batch, ctx_len, seq_len, tm
        self.n_ctx_tiles = n_batch * ctx_len // tm
        self.tiles_per_seq = seq_len // tm
        self.n_lat_tiles = n_batch * self.tiles_per_seq
        self.n_tiles = self.n_ctx_tiles + self.n_lat_tiles

    def mod_row(self, tile):
        return jnp.where(tile < self.n_ctx_tiles, self.n_batch, (tile - self.n_ctx_tiles) // self.tiles_per_seq)


def _mod_spec(rows, first_tile, ncols, col_of_j):
    return pl.BlockSpec((None, 1, ncols), lambda i, j: (rows.mod_row(i + first_tile), 0, col_of_j(j)))


def _part_specs(parts, rows, first_tile, ncols, col_of_j, single_buffer_ctx=False):
    if len(parts) == 1:
        (_, off), = parts
        return [pl.BlockSpec((rows.tm, ncols), lambda i, j: (i + first_tile + off, col_of_j(j)))]
    assert first_tile == 0
    (_, off_c), (_, off_l) = parts
    nct = rows.n_ctx_tiles
    ctx_mode = dict(pipeline_mode=pl.Buffered(1)) if single_buffer_ctx else {}
    return [pl.BlockSpec((rows.tm, ncols), lambda i, j: (jnp.minimum(i, nct - 1) + off_c, col_of_j(j)), **ctx_mode),
            pl.BlockSpec((rows.tm, ncols), lambda i, j: (jnp.maximum(i - nct, 0) + off_l, col_of_j(j)))]


def _with_part(refs, n_ctx_tiles, fn):
    if len(refs) == 1:
        fn(refs[0])
        return
    is_ctx = pl.program_id(0) < n_ctx_tiles
    pl.when(is_ctx)(lambda: fn(refs[0]))
    pl.when(jnp.logical_not(is_ctx))(lambda: fn(refs[1]))


def _for_row_chunks(tm, fn):
    def body(r, _):
        fn(pl.ds(pl.multiple_of(r * ROW_CHUNK, ROW_CHUNK), ROW_CHUNK))
        return 0

    lax.fori_loop(0, tm // ROW_CHUNK, body, 0, unroll=4)


def _rms_modulate_into(h_ref, inv_ref, x_ref, g_ref, sh_ref, sc_ref, tm):
    def stats(rows):
        x = x_ref[rows, :]
        inv_ref[rows, :] = lax.rsqrt(jnp.mean(x * x, axis=-1, keepdims=True) + EPS)

    _for_row_chunks(tm, stats)
    gain = g_ref[...] * (1.0 + sc_ref[...])
    shift = sh_ref[...]

    def affine(rows):
        h_ref[rows, :] = (x_ref[rows, :] * inv_ref[rows, :] * gain + shift).astype(h_ref.dtype)

    _for_row_chunks(tm, affine)


def _ada_kernel(c_ref, w_ref, b_ref, o_ref):
    c = c_ref[...]
    s = (c * jax.nn.sigmoid(c)).astype(BF16)
    o_ref[...] = jnp.dot(s, w_ref[...].astype(BF16), preferred_element_type=F32) + b_ref[...]


def _ada_call(cvec, w_ada, b_ada):
    depth, d, n = w_ada.shape
    r = cvec.shape[0]
    tn = _col_tile(n)
    est = 2 * (d * tn * 4) + d * tn * 2 + 4 * r * (d + 2 * tn) * 4 + (4 << 20)
    return pl.pallas_call(
        _ada_kernel,
        grid=(depth, n // tn),
        in_specs=[
            pl.BlockSpec((r, d), lambda l, j: (0, 0)),
            pl.BlockSpec((None, d, tn), lambda l, j: (l, 0, j)),
            pl.BlockSpec((None, 1, tn), lambda l, j: (l, 0, j)),
        ],
        out_specs=pl.BlockSpec((None, r, tn), lambda l, j: (l, 0, j)),
        out_shape=jax.ShapeDtypeStruct((depth, r, n), F32),
        compiler_params=_params(("parallel", "parallel"), est),
        name="ada_mod",
    )(cvec, w_ada, b_ada.reshape(depth, 1, n))


def _proj_kernel(*refs, n_x, tm, n_ctx_tiles):
    x_refs = refs[:n_x]
    g_ref, sh_ref, sc_ref, w_ref, o_ref, h_ref, inv_ref = refs[n_x:]

    @pl.when(pl.program_id(1) == 0)
    def _():
        _with_part(x_refs, n_ctx_tiles, lambda x_ref: _rms_modulate_into(h_ref, inv_ref, x_ref, g_ref, sh_ref, sc_ref, tm))

    acc = jnp.dot(h_ref[...], w_ref[...], preferred_element_type=F32)
    for s in range(o_ref.shape[0]):
        o_ref[s] = acc[:, s * V7X_LANES:(s + 1) * V7X_LANES]


def _proj_call(x_parts, g, shift, scale, w, layer, rows):
    d = x_parts[0][0].shape[1]
    n = w.shape[2]
    nx = len(x_parts)
    tm, tn = rows.tm, _col_tile(n, 1024)
    x_bufs = 2 if nx == 1 else 3
    est = x_bufs * tm * d * 4 + tm * d * 2 + 2 * d * tn * 2 + 3 * tm * tn * 4 + 8 * ROW_CHUNK * d * 4 + (3 << 20)
    kern = functools.partial(_proj_kernel, n_x=nx, tm=tm, n_ctx_tiles=rows.n_ctx_tiles)
    full = lambda j: 0
    return pl.pallas_call(
        kern,
        grid=(rows.n_tiles, n // tn),
        in_specs=[*_part_specs(x_parts, rows, 0, d, full, single_buffer_ctx=True),
                  pl.BlockSpec((1, d), lambda i, j: (0, 0)),
                  _mod_spec(rows, 0, d, full), _mod_spec(rows, 0, d, full),
                  pl.BlockSpec((None, d, tn), lambda i, j: (layer, 0, j))],
        out_specs=pl.BlockSpec((tn // V7X_LANES, tm, V7X_LANES), lambda i, j: (j, i, 0)),
        out_shape=jax.ShapeDtypeStruct((n // V7X_LANES, rows.n_tiles * tm, V7X_LANES), F32),
        scratch_shapes=[pltpu.VMEM((tm, d), BF16), pltpu.VMEM((tm, 1), F32)],
        compiler_params=_params(("parallel", "arbitrary"), est),
        name="norm_mod_proj",
    )(*[a for a, _ in x_parts], g, shift, scale, w)


def _outproj_kernel(*refs, n_a, n_x, n_ctx_tiles):
    a_refs, x_refs = refs[:n_a], refs[n_a:n_a + n_x]
    w_ref, ga_ref, o_ref = refs[n_a + n_x:]

    def with_a(a_ref):
        acc = jnp.dot(a_ref[...], w_ref[...], preferred_element_type=F32) * ga_ref[...]

        def with_x(x_ref):
            o_ref[...] = x_ref[...] + acc

        _with_part(x_refs, n_ctx_tiles, with_x)

    _with_part(a_refs, n_ctx_tiles, with_a)


def _outproj_call(a_parts, w, layer, x_parts, gate, rows, first_tile):
    k, n = w.shape[1], w.shape[2]
    tm, tn = rows.tm, _col_tile(n, 2048)
    na, nx = len(a_parts), len(x_parts)
    n_tiles = rows.n_tiles - first_tile
    est = 2 * na * tm * k * 2 + 2 * k * tn * 2 + 2 * (nx + 1) * tm * tn * 4 + 3 * tm * tn * 4 + (3 << 20)
    kern = functools.partial(_outproj_kernel, n_a=na, n_x=nx, n_ctx_tiles=rows.n_ctx_tiles)
    return pl.pallas_call(
        kern,
        grid=(n_tiles, n // tn),
        in_specs=[*_part_specs(a_parts, rows, first_tile, k, lambda j: 0),
                  *_part_specs(x_parts, rows, first_tile, tn, lambda j: j),
                  pl.BlockSpec((None, k, tn), lambda i, j: (layer, 0, j)),
                  _mod_spec(rows, first_tile, tn, lambda j: j)],
        out_specs=pl.BlockSpec((tm, tn), lambda i, j: (i, j)),
        out_shape=jax.ShapeDtypeStruct((n_tiles * tm, n), F32),
        compiler_params=_params(("parallel", "parallel"), est),
        name="outproj_residual",
    )(*[a for a, _ in a_parts], *[a for a, _ in x_parts], w, gate)


def _ffn_kernel(*refs, tm, final_norm):
    if final_norm:
        x_ref, g_ref, sh_ref, sc_ref, ga_ref, wg_ref, wu_ref, wo_ref, gf_ref, o_ref, h_ref, inv_ref = refs
    else:
        x_ref, g_ref, sh_ref, sc_ref, ga_ref, wg_ref, wu_ref, wo_ref, o_ref, h_ref, inv_ref = refs
    f = pl.program_id(1)

    @pl.when(f == 0)
    def _():
        _rms_modulate_into(h_ref, inv_ref, x_ref, g_ref, sh_ref, sc_ref, tm)
        o_ref[...] = jnp.zeros_like(o_ref)

    tn = _col_tile(o_ref.shape[1])
    half = tm // 2
    for r0 in range(0, tm, half):
        h = h_ref[r0:r0 + half, :]
        gate = jnp.dot(h, wg_ref[...], preferred_element_type=F32)
        up = jnp.dot(h, wu_ref[...], preferred_element_type=F32)
        act = (gate * jax.nn.sigmoid(gate) * up).astype(BF16)
        for c0 in range(0, o_ref.shape[1], tn):
            o_ref[r0:r0 + half, c0:c0 + tn] += jnp.dot(act, wo_ref[:, c0:c0 + tn], preferred_element_type=F32)

    @pl.when(f == pl.num_programs(1) - 1)
    def _():
        res_gate = ga_ref[...]

        def chunk(rows):
            y = x_ref[rows, :] + res_gate * o_ref[rows, :]
            if final_norm:
                y = y * lax.rsqrt(jnp.mean(y * y, axis=-1, keepdims=True) + EPS) * gf_ref[...]
            o_ref[rows, :] = y

        _for_row_chunks(tm, chunk)


def _ffn_call(xs, x_first_tile, g, shift, scale, res_gate, w_in, w_out, layer, rows, first_tile, final_g=None):
    d = xs.shape[1]
    ff = w_out.shape[1]
    tm, tf = rows.tm, _col_tile(ff)
    nf = ff // tf
    n_tiles = rows.n_tiles - first_tile
    est = (4 * tm * d * 4 + tm * d * 2 + 2 * (2 * d * tf * 2 + tf * d * 2) + 2 * tm * tf * 4 + tm * tf * 2
           + 8 * ROW_CHUNK * d * 4 + (3 << 20))
    kern = functools.partial(_ffn_kernel, tm=tm, final_norm=final_g is not None)
    full = lambda f: 0
    extra_specs = [] if final_g is None else [pl.BlockSpec((1, d), lambda i, f: (0, 0))]
    extra_args = [] if final_g is None else [final_g]
    return pl.pallas_call(
        kern,
        grid=(n_tiles, nf),
        in_specs=[pl.BlockSpec((tm, d), lambda i, f: (i + x_first_tile, 0)),
                  pl.BlockSpec((1, d), lambda i, f: (0, 0)),
                  _mod_spec(rows, first_tile, d, full), _mod_spec(rows, first_tile, d, full),
                  _mod_spec(rows, first_tile, d, full),
                  pl.BlockSpec((None, d, tf), lambda i, f: (layer, 0, f)),
                  pl.BlockSpec((None, d, tf), lambda i, f: (layer, 0, nf + f)),
                  pl.BlockSpec((None, tf, d), lambda i, f: (layer, f, 0)),
                  *extra_specs],
        out_specs=pl.BlockSpec((tm, d), lambda i, f: (i, 0)),
        out_shape=jax.ShapeDtypeStruct((n_tiles * tm, d), F32),
        scratch_shapes=[pltpu.VMEM((tm, d), BF16), pltpu.VMEM((tm, 1), F32)],
        compiler_params=_params(("parallel", "arbitrary"), est),
        name="swiglu_ffn",
    )(xs, g, shift, scale, res_gate, w_in, w_in, w_out, *extra_args)


def _softplus(x):
    return jnp.maximum(x, 0.0) + jnp.log1p(jnp.exp(-jnp.abs(x)))


def _chunk_carries(h_loc, p_loc, h_init, reverse):
    n = h_loc.shape[0]
    carries = [None] * n
    c = h_init
    for s in (reversed(range(n)) if reverse else range(n)):
        carries[s] = c
        c = p_loc[s:s + 1] * c + h_loc[s:s + 1]
    return carries, c


def _strided_scan(coef, res, n, stride):
    nsub = V7X_SUBLANES
    af_ref, bf_ref, ab_ref, bb_ref = coef
    pf_ref, hf_ref, pb_ref, hb_ref = res
    lanes = af_ref.shape[1]

    def body(i, carry):
        hf, pf, hb, pb = carry
        rows_f = pl.ds(i, nsub, stride=stride)
        rows_b = pl.ds(n - 1 - i, nsub, stride=stride)
        a = af_ref[rows_f, :]
        hf = a * hf + bf_ref[rows_f, :]
        pf = a * pf
        hf_ref[rows_f, :] = hf
        pf_ref[rows_f, :] = pf
        a = ab_ref[rows_b, :]
        hb = a * hb + bb_ref[rows_b, :]
        pb = a * pb
        hb_ref[rows_b, :] = hb
        pb_ref[rows_b, :] = pb
        return hf, pf, hb, pb

    zero = jnp.zeros((nsub, lanes), F32)
    one = jnp.ones((nsub, lanes), F32)
    return lax.fori_loop(0, n, body, (zero, one, zero, one), unroll=8)


def _rg_kernel(*refs, ctx_len, seq_len, need_ctx):
    pc_ref, pl_ref, cw_ref, cb_ref, wg_ref, bg_ref, lam_ref = refs[:7]
    xc_ref, gc_ref = pc_ref.at[0], pc_ref.at[1]
    xl_ref, gl_ref = pl_ref.at[0], pl_ref.at[1]
    n_out = 2 if need_ctx else 1
    y_ref = refs[7]
    yc_ref = refs[8] if need_ctx else None
    (xp_ref, laf, lbf, lab, lbb, lpf, lhf, lpb, lhb, caf, cbf, cab, cbb, cpf, chf, cpb, chb, carry_ref) = refs[7 + n_out:]
    nsub = V7X_SUBLANES
    lanes = xc_ref.shape[1]
    rb = ctx_len
    nblk = 1 + seq_len // rb
    halo = nsub
    lat_stride = rb + nsub
    ctx_n = ctx_len // nsub
    ctx_stride = ctx_n + nsub

    xp_ref[pl.ds(0, halo), :] = jnp.zeros((halo, lanes), F32)
    xp_ref[pl.ds(halo + ctx_len + seq_len, halo), :] = jnp.zeros((halo, lanes), F32)
    xp_ref[pl.ds(halo, ctx_len), :] = xc_ref[...]
    xp_ref[pl.ds(halo + ctx_len, seq_len), :] = xl_ref[...]

    cw = cw_ref[...]
    cb = cb_ref[...]
    softplus_neg_lam = _softplus(-lam_ref[...])
    local_row = lax.broadcasted_iota(jnp.int32, (rb, 1), 0)

    def coeffs(blk):
        start = blk * rb if isinstance(blk, int) else pl.multiple_of(blk * rb, rb)
        seg_first = blk <= 1
        seg_last = jnp.logical_or(blk == 0, blk == nblk - 1)
        taps = []
        for k in range(CONV_TAPS):
            off = k - CONV_LEFT
            tap = xp_ref[pl.ds(start + halo + off, rb), :]
            if off < 0:
                tap = jnp.where(jnp.logical_or(local_row >= -off, jnp.logical_not(seg_first)), tap, 0.0)
            elif off > 0:
                tap = jnp.where(jnp.logical_or(local_row < rb - off, jnp.logical_not(seg_last)), tap, 0.0)
            taps.append(tap)
        xc = cb
        for k in range(CONV_TAPS):
            xc = xc + cw[k:k + 1] * taps[k]
        half_z = jnp.dot(xc.astype(BF16), wg_ref[...], preferred_element_type=F32) + bg_ref[...]
        half_xc = 0.5 * xc
        out = []
        for d in range(2):
            th_r = jnp.tanh(half_z[:, (2 * d) * lanes:(2 * d + 1) * lanes])
            th_i = jnp.tanh(half_z[:, (2 * d + 1) * lanes:(2 * d + 2) * lanes])
            scale = (-0.5 * RG_C) * softplus_neg_lam[d:d + 1]
            log_a = scale * th_r + scale
            a = jnp.exp(log_a)
            t = jnp.tanh(log_a)
            u = (-2.0) * t / (1.0 - t)
            mult = jnp.where(u > 0.0, u * lax.rsqrt(u), 0.0)
            out.append((a, mult * half_xc * (1.0 + th_i)))
        return out

    (a_f, b_f), (a_b, b_b) = coeffs(0)
    for c in range(nsub):
        dst = pl.ds(c * ctx_stride, ctx_n)
        src = slice(c * ctx_n, (c + 1) * ctx_n)
        caf[dst, :] = a_f[src]
        cbf[dst, :] = b_f[src]
        cab[dst, :] = a_b[src]
        cbb[dst, :] = b_b[src]

    def fill(blk, _):
        (a_f, b_f), (a_b, b_b) = coeffs(blk)
        dst = pl.ds(pl.multiple_of((blk - 1) * lat_stride, nsub), rb)
        laf[dst, :] = a_f
        lbf[dst, :] = b_f
        lab[dst, :] = a_b
        lbb[dst, :] = b_b
        return 0

    lax.fori_loop(1, nblk, fill, 0, unroll=4)

    zero_state = jnp.zeros((1, lanes), F32)
    hf, pf, hb, pb = _strided_scan((caf, cbf, cab, cbb), (cpf, chf, cpb, chb), ctx_n, ctx_stride)
    ctx_in_f, ctx_state_f = _chunk_carries(hf, pf, zero_state, reverse=False)
    ctx_in_b, ctx_state_b = _chunk_carries(hb, pb, zero_state, reverse=True)
    hf, pf, hb, pb = _strided_scan((laf, lbf, lab, lbb), (lpf, lhf, lpb, lhb), rb, lat_stride)
    lat_in_f, _ = _chunk_carries(hf, pf, ctx_state_f, reverse=False)
    lat_in_b, _ = _chunk_carries(hb, pb, ctx_state_b, reverse=True)
    for s in range(nsub):
        carry_ref[pl.ds(s, 1), :] = lat_in_f[s]
        carry_ref[pl.ds(nsub + s, 1), :] = lat_in_b[s]

    if need_ctx:
        for c in range(nsub):
            src = pl.ds(c * ctx_stride, ctx_n)
            h = chf[src, :] + cpf[src, :] * ctx_in_f[c] + chb[src, :] + cpb[src, :] * ctx_in_b[c]
            rows = pl.ds(c * ctx_n, ctx_n)
            yc_ref[rows, :] = (h * jax.nn.gelu(gc_ref[rows, :])).astype(yc_ref.dtype)

    def emit(blk, _):
        src = pl.ds(pl.multiple_of((blk - 1) * lat_stride, nsub), rb)
        cf = carry_ref[pl.ds(blk - 1, 1), :]
        cbw = carry_ref[pl.ds(nsub + blk - 1, 1), :]
        h = lhf[src, :] + lpf[src, :] * cf + lhb[src, :] + lpb[src, :] * cbw
        rows = pl.ds(pl.multiple_of((blk - 1) * rb, rb), rb)
        y_ref[rows, :] = (h * jax.nn.gelu(gl_ref[rows, :])).astype(y_ref.dtype)
        return 0

    lax.fori_loop(1, nblk, emit, 0, unroll=4)


def _mixer_out(t, w, n_batch, ctx_len, seq_len, lanes, need_ctx):
    specs = [pl.BlockSpec((seq_len, lanes), lambda b, h: (b, h))]
    shapes = [jax.ShapeDtypeStruct((n_batch * seq_len, w), BF16)]
    if need_ctx:
        specs.append(pl.BlockSpec((ctx_len, lanes), lambda b, h: (b, h)))
        shapes.append(jax.ShapeDtypeStruct((n_batch * ctx_len, w), BF16))
    return specs, shapes


def _rg_call(p, conv_w, conv_b, w_gates, b_gates, lam, *, n_batch, ctx_len, seq_len, need_ctx):
    lanes = V7X_LANES
    t = p.shape[1]
    nh = p.shape[0] // 2
    w = nh * lanes
    p4 = p.reshape(2, nh, t, lanes)
    nsub = V7X_SUBLANES
    lb = ctx_len + seq_len
    lat_block0 = n_batch * ctx_len // seq_len
    lat_rows = nsub * (seq_len // nsub + nsub)
    ctx_rows = nsub * (ctx_len // nsub + nsub)
    est = (4 * lb * lanes * 4 + 2 * lb * lanes * 2 + (lb + 2 * nsub) * lanes * 4 + 8 * (lat_rows + ctx_rows) * lanes * 4
           + 48 * ctx_len * lanes * 4 + (4 << 20))
    kern = functools.partial(_rg_kernel, ctx_len=ctx_len, seq_len=seq_len, need_ctx=need_ctx)
    out_specs, out_shapes = _mixer_out(t, w, n_batch, ctx_len, seq_len, lanes, need_ctx)
    outs = pl.pallas_call(
        kern,
        grid=(n_batch, nh),
        in_specs=[pl.BlockSpec((2, None, ctx_len, lanes), lambda b, h: (0, h, b, 0)),
                  pl.BlockSpec((2, None, seq_len, lanes), lambda b, h: (0, h, lat_block0 + b, 0)),
                  pl.BlockSpec((CONV_TAPS, lanes), lambda b, h: (0, h)),
                  pl.BlockSpec((1, lanes), lambda b, h: (0, h)),
                  pl.BlockSpec((None, lanes, 4 * lanes), lambda b, h: (h, 0, 0)),
                  pl.BlockSpec((None, 1, 4 * lanes), lambda b, h: (h, 0, 0)),
                  pl.BlockSpec((2, lanes), lambda b, h: (0, h))],
        out_specs=out_specs,
        out_shape=out_shapes,
        scratch_shapes=[pltpu.VMEM((lb + 2 * nsub, lanes), F32)]
        + [pltpu.VMEM((lat_rows, lanes), F32)] * 8
        + [pltpu.VMEM((ctx_rows, lanes), F32)] * 8
        + [pltpu.VMEM((2 * nsub, lanes), F32)],
        compiler_params=_params(("parallel", "parallel"), est),
        name="rglru_mixer",
    )(p4, p4, conv_w, conv_b, w_gates, b_gates, lam)
    return outs if need_ctx else (outs[0], None)


def _group_cumsum(x, group, reverse):
    n = x.shape[0]
    pos = lax.broadcasted_iota(jnp.int32, (n, 1), 0) % group
    step = 1
    while step < group:
        if reverse:
            shifted = pltpu.roll(x, n - step, axis=0)
            keep = pos < group - step
        else:
            shifted = pltpu.roll(x, step, axis=0)
            keep = pos >= step
        x = x + jnp.where(keep, shifted, 0.0)
        step *= 2
    return x


def _hg_kernel(*refs, ctx_len, seq_len, layer, dk, unroll, need_ctx):
    pc_ref, pl_ref, lb_ref, nw_ref = refs[:4]
    n_out = 2 if need_ctx else 1
    y_ref = refs[4]
    yc_ref = refs[5] if need_ctx else None
    (pitch, qs, fs, bs, vs, qd_f, qd_b, oacc, kv_f, kv_b, st_f, st_b,
     k_slow, cum_slow, o_slow, deepest_ref) = refs[4 + n_out:]
    lanes = qs.shape[1]
    ch = HG_CHUNK
    lb_rows = ctx_len + seq_len
    nchunk = lb_rows // ch
    ctx_chunks = ctx_len // ch
    sb = ctx_len
    nsb = lb_rows // sb
    cps = sb // ch
    ncol = seq_len // ch
    grid_pitch = ncol + V7X_SUBLANES
    qd_refs, kv_refs, st_refs = (qd_f, qd_b), (kv_f, kv_b), (st_f, st_b)

    def to_scan_order(ctx_src, lat_src, dst_ref):
        dst_ref[pl.ds(0, ctx_len), :] = ctx_src[...]

        def spread(r, _):
            src = pl.ds(pl.multiple_of(r * ncol, ncol), ncol)
            pitch[pl.ds(pl.multiple_of(r * grid_pitch, V7X_SUBLANES), ncol), :] = lat_src[src, :]
            return 0

        lax.fori_loop(0, ch, spread, 0, unroll=True)

        def column(n, _):
            dst = pl.ds(pl.multiple_of(ctx_len + n * ch, ch), ch)
            dst_ref[dst, :] = pitch[pl.ds(n, ch, stride=grid_pitch), :]
            return 0

        lax.fori_loop(0, ncol, column, 0, unroll=True)

    for k, dst_ref in enumerate((qs, fs, bs, vs)):
        to_scan_order(pc_ref.at[k], pl_ref.at[k], dst_ref)
    gc_ref, gl_ref = pc_ref.at[4], pl_ref.at[4]

    lbp = lb_ref[...]
    e = jnp.exp(lbp - jnp.max(lbp, axis=0, keepdims=True))
    sm = e / jnp.sum(e, axis=0, keepdims=True)
    lower = jnp.zeros(sm.shape[1:], F32)
    for l in range(1, layer + 1):
        lower = lower + sm[l]

    q_scale = dk ** -0.5
    j_idx = lax.broadcasted_iota(jnp.int32, (cps, ch, ch), 1)
    s_idx = lax.broadcasted_iota(jnp.int32, (cps, ch, ch), 2)
    causal = (j_idx >= s_idx, j_idx <= s_idx)
    aux = 2 * V7X_SUBLANES
    aux_row = lax.broadcasted_iota(jnp.int32, (1, aux, 1), 1)

    def bf16_pieces(x):
        hi = x.astype(BF16)
        rest = x - hi.astype(F32)
        mid = rest.astype(BF16)
        lo = (rest - mid.astype(F32)).astype(BF16)
        return hi, mid, lo

    def gate_feats(rows, d):
        lo = lower[d:d + 1]
        half_open = 0.5 * (1.0 - lo)
        th = jnp.tanh((bs if d else fs)[rows, :])
        log_f = jnp.log(lo + half_open * (1.0 + th))
        k = half_open * (1.0 - th)
        return k, _group_cumsum(log_f, ch, reverse=bool(d))

    def exact_intra(rows):
        q3 = (qs[rows, :] * q_scale).reshape(cps, ch, lanes)
        v3 = vs[rows, :].reshape(cps, ch, lanes)
        pos = lax.broadcasted_iota(jnp.int32, (1, ch, 1), 1)
        for d in range(2):
            k, cum = gate_feats(rows, d)
            k_slow[...] = k.reshape(cps, ch, lanes)
            cum_slow[...] = cum.reshape(cps, ch, lanes)

            def row_j(j, _):
                cum3 = cum_slow[...]
                w = jnp.exp(jnp.minimum(cum_slow[:, pl.ds(j, 1), :] - cum3, 0.0))
                valid = (pos >= j) if d else (pos <= j)
                q_j = jnp.where(pos == j, q3, 0.0).sum(axis=1, keepdims=True)
                col = jnp.sum(jnp.where(valid, q_j * k_slow[...] * w, 0.0), axis=-1, keepdims=True)
                o_j = jnp.sum(col * v3, axis=1, keepdims=True)
                if d:
                    o_slow[:, pl.ds(j, 1), :] += o_j
                else:
                    o_slow[:, pl.ds(j, 1), :] = o_j
                return 0

            lax.fori_loop(0, ch, row_j, 0)
        oacc[rows, :] = o_slow[...].reshape(sb, lanes)

    def local_part(i, _):
        rows = pl.ds(pl.multiple_of(i * sb, sb), sb)
        q = qs[rows, :] * q_scale
        v3 = vs[rows, :].reshape(cps, ch, lanes).astype(BF16)
        zeros_v = jnp.zeros((cps, ch, lanes), BF16)
        pad_rows = jnp.concatenate([jnp.zeros((cps, aux, lanes), BF16), jnp.ones((cps, aux, lanes), BF16)], axis=2)
        kv_rhs = jnp.concatenate([jnp.concatenate([v3, zeros_v], axis=2), pad_rows], axis=1)
        o_sum = None
        deepest = None
        for d in range(2):
            k, cum = gate_feats(rows, d)
            cum_l = cum * LOG2_E
            cum_l3 = cum_l.reshape(cps, ch, lanes)
            total_l3 = cum_l3[:, 0:1, :] if d else cum_l3[:, ch - 1:ch, :]
            qd = (q * jnp.exp2(cum_l)).astype(BF16)
            kd = (k * jnp.exp2(jnp.minimum(cum * (-LOG2_E), EXP_CLAMP * LOG2_E))).astype(BF16)
            kl = (k.reshape(cps, ch, lanes) * jnp.exp2(total_l3 - cum_l3)).astype(BF16)
            att = jnp.einsum("cjd,csd->cjs", qd.reshape(cps, ch, lanes), kd.reshape(cps, ch, lanes),
                             preferred_element_type=F32)
            att = jnp.where(causal[d], att, 0.0)
            o_intra = jnp.einsum("cjs,cse->cje", att.astype(BF16), v3, preferred_element_type=F32)
            o_sum = o_intra if o_sum is None else o_sum + o_intra
            qd_refs[d][rows, :] = qd
            d_hi, d_mid, d_lo = [p.astype(F32) for p in bf16_pieces(jnp.exp2(total_l3))]
            dec_rows = jnp.where(aux_row == 0, d_hi, jnp.where(aux_row == 1, d_mid, jnp.where(aux_row == 2, d_lo, 0.0)))
            kv_lhs = jnp.concatenate([kl, dec_rows.astype(BF16)], axis=1)
            for c in range(cps):
                kv_refs[d][i * cps + c] = lax.dot_general(kv_lhs[c], kv_rhs[c], (((0,), (0,)), ((), ())),
                                                          preferred_element_type=F32)
            low = jnp.min(total_l3, axis=0)
            deepest = low if deepest is None else jnp.minimum(deepest, low)
        oacc[rows, :] = o_sum.reshape(sb, lanes)
        deepest_ref[i] = jnp.min(deepest)
        return 0

    lax.fori_loop(0, nsb, local_part, 0, unroll=unroll)

    def redo_unsafe(i, _):
        @pl.when(deepest_ref[i] < -EXP_CLAMP * LOG2_E)
        def _():
            exact_intra(pl.ds(pl.multiple_of(i * sb, sb), sb))

        return 0

    lax.fori_loop(0, nsb, redo_unsafe, 0)

    def state_step(d, n, st):
        st_refs[d][n] = st.astype(BF16)
        kv = kv_refs[d][n]
        return kv[:, lanes:] * st + kv[:, :lanes]

    zero_state = jnp.zeros((lanes, lanes), F32)
    lax.fori_loop(0, nchunk, functools.partial(state_step, 0), zero_state, unroll=8)
    st = lax.fori_loop(0, ctx_chunks, lambda i, s: state_step(1, ctx_chunks - 1 - i, s), zero_state, unroll=8)
    lax.fori_loop(0, nchunk - ctx_chunks, lambda i, s: state_step(1, nchunk - 1 - i, s), st, unroll=8)

    nw = nw_ref[...]

    def inter_part(i, _):
        rows = pl.ds(pl.multiple_of(i * sb, sb), sb)
        o = oacc[rows, :]
        for d in range(2):
            qd3 = qd_refs[d][rows, :].reshape(cps, ch, lanes)
            st3 = st_refs[d][pl.ds(i * cps, cps)]
            o = o + jnp.einsum("cjd,cde->cje", qd3, st3, preferred_element_type=F32).reshape(sb, lanes)
        oacc[rows, :] = o * lax.rsqrt(jnp.mean(o * o, axis=-1, keepdims=True) + EPS) * nw
        return 0

    first_sb = 0 if need_ctx else ctx_len // sb
    lax.fori_loop(first_sb, nsb, inter_part, 0, unroll=True)

    def gated(o, g):
        return (o * (g * jax.nn.sigmoid(g))).astype(y_ref.dtype)

    if need_ctx:
        yc_ref[...] = gated(oacc[pl.ds(0, ctx_len), :], gc_ref[...])

    def column(n, _):
        src = pl.ds(pl.multiple_of(ctx_len + n * ch, ch), ch)
        pitch[pl.ds(n, ch, stride=grid_pitch), :] = oacc[src, :]
        return 0

    lax.fori_loop(0, ncol, column, 0, unroll=True)

    def gather_row(r, _):
        dst = pl.ds(pl.multiple_of(r * ncol, ncol), ncol)
        y_ref[dst, :] = gated(pitch[pl.ds(pl.multiple_of(r * grid_pitch, V7X_SUBLANES), ncol), :], gl_ref[dst, :])
        return 0

    lax.fori_loop(0, ch, gather_row, 0, unroll=True)


def _hg_call(p, hg_lb, norm_w, *, n_batch, ctx_len, seq_len, layer, need_ctx):
    lanes = V7X_LANES
    t = p.shape[1]
    nh = p.shape[0] // 5
    w = nh * lanes
    p4 = p.reshape(5, nh, t, lanes)
    lb = ctx_len + seq_len
    lat_block0 = n_batch * ctx_len // seq_len
    nchunk = lb // HG_CHUNK
    depth = hg_lb.shape[0]
    blk_bytes = lb * lanes * 4
    cps = ctx_len // HG_CHUNK
    pitch_rows = HG_CHUNK * (seq_len // HG_CHUNK + V7X_SUBLANES)
    unroll = lb // ctx_len
    est = (10 * blk_bytes + 2 * lb * lanes * 2 + 8 * blk_bytes + pitch_rows * lanes * 4 + 2 * nchunk * lanes * lanes * 10
           + unroll * 40 * ctx_len * lanes * 4 + (4 << 20))
    kern = functools.partial(_hg_kernel, ctx_len=ctx_len, seq_len=seq_len, layer=layer, dk=lanes, unroll=unroll,
                             need_ctx=need_ctx)
    in_specs = [pl.BlockSpec((5, None, ctx_len, lanes), lambda b, h: (0, h, b, 0)),
                pl.BlockSpec((5, None, seq_len, lanes), lambda b, h: (0, h, lat_block0 + b, 0))]
    out_specs, out_shapes = _mixer_out(t, w, n_batch, ctx_len, seq_len, lanes, need_ctx)
    outs = pl.pallas_call(
        kern,
        grid=(n_batch, nh),
        in_specs=in_specs + [pl.BlockSpec((depth, 2, lanes), lambda b, h: (0, 0, h)),
                             pl.BlockSpec((1, lanes), lambda b, h: (0, 0))],
        out_specs=out_specs,
        out_shape=out_shapes,
        scratch_shapes=[pltpu.VMEM((pitch_rows, lanes), F32)]
        + [pltpu.VMEM((lb, lanes), F32)] * 4
        + [pltpu.VMEM((lb, lanes), BF16)] * 2
        + [pltpu.VMEM((lb, lanes), F32)]
        + [pltpu.VMEM((nchunk, lanes, 2 * lanes), F32)] * 2
        + [pltpu.VMEM((nchunk, lanes, lanes), BF16)] * 2
        + [pltpu.VMEM((cps, HG_CHUNK, lanes), F32)] * 3
        + [pltpu.SMEM((lb // ctx_len,), F32)],
        compiler_params=_params(("parallel", "parallel"), est),
        name="hgrn2_mixer",
    )(p4, p4, hg_lb, norm_w)
    return outs if need_ctx else (outs[0], None)


def kernel(x, c, ctx, c_ctx, w_ada, b_ada, g_mix, g_ffn, g_final, w_ffn_in, w_ffn_out,
           rg_w_in, rg_conv_w, rg_conv_b, rg_w_a, rg_b_a, rg_w_i, rg_b_i, rg_lam, rg_w_out,
           hg_w_in, hg_lb, hg_norm, hg_w_out):
    n_batch, seq_len, d = x.shape
    ctx_len = ctx.shape[1]
    depth = w_ada.shape[0]
    lanes = V7X_LANES
    assert seq_len == V7X_SUBLANES * ctx_len, "row blocking assumes the latent is 8 context lengths long"
    assert seq_len // GRID_W == HG_CHUNK, "an HGRN2 chunk must be one latent grid column"
    assert (n_batch * ctx_len) % seq_len == 0, "latent blocks must stay block-aligned behind the context rows"
    assert d % lanes == 0 and rg_w_a.shape[-1] == lanes and hg_norm.shape[-1] == lanes
    dims = dict(n_batch=n_batch, ctx_len=ctx_len, seq_len=seq_len)
    rows = _Rows(n_batch, ctx_len, seq_len, ROW_TILE)
    rows_half = _Rows(n_batch, ctx_len, seq_len, ROW_TILE // 2)
    nct = rows.n_ctx_tiles

    pad = (-(n_batch + 1)) % V7X_SUBLANES
    cvec = jnp.concatenate([c, c_ctx[None, :], jnp.zeros((pad, d), F32)], axis=0)
    mods = _ada_call(cvec, w_ada, b_ada).reshape(depth, cvec.shape[0], N_MOD, d)

    w_ffn_in_b, w_ffn_out_b = w_ffn_in.astype(BF16), w_ffn_out.astype(BF16)
    rg_w_in_b, rg_w_out_b = rg_w_in.astype(BF16), rg_w_out.astype(BF16)
    fifth = jnp.arange(hg_w_in.shape[-1]) // d
    hg_w_in_b = (hg_w_in * jnp.where((fifth == 1) | (fifth == 2), 0.5, 1.0)).astype(BF16)
    hg_w_out_b = hg_w_out.astype(BF16)

    x_parts = [(ctx.reshape(n_batch * ctx_len, d), 0), (x.reshape(n_batch * seq_len, d), 0)]
    out = None
    for i in range(depth):
        last = i == depth - 1
        sh1, sc1, ga1, sh2, sc2, ga2 = [mods[i, :, k][:, None, :] for k in range(N_MOD)]
        j = i // N_MIXERS
        if i % N_MIXERS == 0:
            p = _proj_call(x_parts, g_mix[i][None], sh1, sc1, rg_w_in_b, j, rows)
            w_gates = (0.5 * jnp.concatenate([rg_w_a[j, 0], rg_w_i[j, 0], rg_w_a[j, 1], rg_w_i[j, 1]], axis=-1)).astype(BF16)
            b_gates = 0.5 * jnp.concatenate([rg_b_a[j, 0], rg_b_i[j, 0], rg_b_a[j, 1], rg_b_i[j, 1]], axis=-1)[:, None, :]
            y, y_ctx = _rg_call(p, rg_conv_w[j], rg_conv_b[j][None], w_gates, b_gates, rg_lam[j], need_ctx=not last, **dims)
            w_out = rg_w_out_b
        else:
            p = _proj_call(x_parts, g_mix[i][None], sh1, sc1, hg_w_in_b, j, rows)
            y, y_ctx = _hg_call(p, hg_lb, hg_norm[j][None], layer=i, need_ctx=not last, **dims)
            w_out = hg_w_out_b
        if last:
            xs = _outproj_latent(y, w_out, j, x_parts, ga1, rows_half)
            out = _ffn_call(xs, 0, g_ffn[i][None], sh2, sc2, ga2, w_ffn_in_b, w_ffn_out_b, i, rows, nct,
                            final_g=g_final[None])
        else:
            xs = _outproj_call([(y_ctx, 0), (y, 0)], w_out, j, x_parts, ga1, rows_half, 0)
            xs = _ffn_call(xs, 0, g_ffn[i][None], sh2, sc2, ga2, w_ffn_in_b, w_ffn_out_b, i, rows, 0)
            x_parts = [(xs, 0)]
    return out.reshape(n_batch, seq_len, d)


def _outproj_latent(y, w_out, j, x_parts, gate, rows):
    nct = rows.n_ctx_tiles
    if len(x_parts) == 2:
        x_lat = [(x_parts[1][0], -nct)]
    else:
        x_lat = [(x_parts[0][0], 0)]
    return _outproj_call([(y, -nct)], w_out, j, x_lat, gate, rows, nct)
```
